```python
import jax, jax.numpy as jnp
from jax import lax
import numpy as np

D_MODEL = 2048
BATCH = 4
SEQ = 2048
DEPTH = 1
DEC_BATCH = 128
DEC_SEQ = 4
PAST_LEN = 16384
PAGE_SIZE = 128

GLA_HEADS = 4
GLA_DK = D_MODEL // 2 // GLA_HEADS
GLA_DV = D_MODEL // GLA_HEADS
GLA_RANK = 16
GLA_TAU = 16.0
GLA_CHUNK = 16
RET_HEADS = 8
RET_DK = D_MODEL // 2 // RET_HEADS
RET_DV = D_MODEL // RET_HEADS
RET_CHUNK = 64
ROPE_BASE = 10000.0
LN_EPS = 1e-5
HEAD_NORM_EPS = 1e-6
DEEPNORM_ALPHA = (2.0 * DEPTH) ** 0.25
DEEPNORM_BETA = (8.0 * DEPTH) ** -0.25

QK_A = GLA_HEADS * GLA_DK
V_A = GLA_HEADS * GLA_DV
QK_B = RET_HEADS * RET_DK
V_B = RET_HEADS * RET_DV
SPLITS = (QK_A, QK_A, V_A, V_A, GLA_RANK, QK_B, QK_B, V_B, V_B, D_MODEL, D_MODEL)
D_IN = sum(SPLITS)

kernel_name = "gla_retnet_parallel_gated_deepnorm_step"


def _split_cols(h):
    idx, acc = [], 0
    for s in SPLITS[:-1]:
        acc += s
        idx.append(acc)
    return jnp.split(h, idx, axis=-1)


def _pad_time(a, n_pad):
    if n_pad == 0:
        return a
    return jnp.pad(a, [(0, 0), (0, n_pad)] + [(0, 0)] * (a.ndim - 2))


def _to_chunks(t, n, c):
    B = t.shape[0]
    return t.reshape(B, n, c, *t.shape[2:]).transpose(1, 0, 3, 2, *range(4, t.ndim + 1))


def gla_chunked(q, k, v, log_a, s0):
    B, L, H, _ = q.shape
    c = min(GLA_CHUNK, L)
    n = -(-L // c)
    pad = n * c - L
    f32 = jnp.float32
    qc, kc, vc, ac = [_to_chunks(_pad_time(t.astype(f32), pad), n, c) for t in (q, k, v, log_a)]
    mask = jnp.tril(jnp.ones((c, c), dtype=bool))

    def step(S, inp):
        qi, ki, vi, ai = inp
        b = jnp.cumsum(ai, axis=2)
        b_last = b[:, :, -1, :]
        o_inter = jnp.einsum('bhtk,bhkv->bhtv', qi * jnp.exp(b), S)
        rel = b[:, :, :, None, :] - b[:, :, None, :, :]
        rel = jnp.exp(jnp.where(mask[:, :, None], rel, -jnp.inf))
        scores = jnp.einsum('bhtk,bhsk,bhtsk->bhts', qi, ki, rel)
        o_intra = jnp.einsum('bhts,bhsv->bhtv', scores, vi)
        S_new = jnp.exp(b_last)[..., None] * S + jnp.einsum(
            'bhsk,bhsv->bhkv', ki * jnp.exp(b_last[:, :, None, :] - b), vi)
        return S_new, o_inter + o_intra

    S, o = lax.scan(step, s0.astype(f32), (qc, kc, vc, ac))
    o = o.transpose(1, 0, 3, 2, 4).reshape(B, n * c, H, -1)[:, :L]
    return o, S.astype(s0.dtype)


def retention_chunked(q, k, v, log_g, s0):
    B, L, H, _ = q.shape
    c = min(RET_CHUNK, L)
    n = -(-L // c)
    pad = n * c - L
    f32 = jnp.float32
    qc, kc, vc = [_to_chunks(_pad_time(t.astype(f32), pad), n, c) for t in (q, k, v)]
    gc = _pad_time(log_g.astype(f32), pad).reshape(B, n, c, H).transpose(1, 0, 3, 2)
    mask = jnp.tril(jnp.ones((c, c), dtype=bool))

    def step(S, inp):
        qi, ki, vi, gi = inp
        b = jnp.cumsum(gi, axis=-1)
        b_last = b[:, :, -1]
        o_inter = jnp.einsum('bhtk,bhkv->bhtv', qi * jnp.exp(b)[..., None], S)
        decay = jnp.exp(jnp.where(mask, b[:, :, :, None] - b[:, :, None, :], -jnp.inf))
        scores = jnp.einsum('bhtk,bhsk->bhts', qi, ki) * decay
        o_intra = jnp.einsum('bhts,bhsv->bhtv', scores, vi)
        S_new = jnp.exp(b_last)[..., None, None] * S + jnp.einsum(
            'bhsk,bhsv->bhkv', ki * jnp.exp(b_last[..., None] - b)[..., None], vi)
        return S_new, o_inter + o_intra

    S, o = lax.scan(step, s0.astype(f32), (qc, kc, vc, gc))
    o = o.transpose(1, 0, 3, 2, 4).reshape(B, n * c, H, -1)[:, :L]
    return o, S.astype(s0.dtype)


def _rotate(x, pos):
    f32 = jnp.float32
    inv = 1.0 / (ROPE_BASE ** jnp.linspace(0.0, 1.0, RET_DK // 2, dtype=f32))
    ang = pos.astype(f32)[:, None] * inv[None, :]
    sin = jnp.sin(ang)[None, :, None, :]
    cos = jnp.cos(ang)[None, :, None, :]
    xf = x.astype(f32)
    x1, x2 = xf[..., 0::2], xf[..., 1::2]
    return jnp.stack([x1 * cos - x2 * sin, x2 * cos + x1 * sin], axis=-1).reshape(x.shape)


def _ret_log_decay():
    return jnp.log(1.0 - 2.0 ** (-5.0 - jnp.arange(RET_HEADS, dtype=jnp.float32)))


def _rms_heads(o, w):
    o = o * lax.rsqrt(jnp.mean(o * o, axis=-1, keepdims=True) + HEAD_NORM_EPS) * w.astype(jnp.float32)
    return o.reshape(o.shape[0], o.shape[1], -1)


def _group_norm(o, w, b):
    mu = jnp.mean(o, axis=-1, keepdims=True)
    d = o - mu
    var = jnp.mean(d * d, axis=-1, keepdims=True)
    o = d * lax.rsqrt(var + HEAD_NORM_EPS)
    o = o * w.astype(jnp.float32).reshape(RET_HEADS, RET_DV) + b.astype(jnp.float32).reshape(RET_HEADS, RET_DV)
    return o.reshape(o.shape[0], o.shape[1], -1)


def _layer_norm(x, w, b):
    xf = x.astype(jnp.float32)
    mu = jnp.mean(xf, axis=-1, keepdims=True)
    d = xf - mu
    var = jnp.mean(d * d, axis=-1, keepdims=True)
    return d * lax.rsqrt(var + LN_EPS) * w.astype(jnp.float32) + b.astype(jnp.float32)


def hybrid_layer(x, pos, s_gla0, s_ret0, w_in, w_lr, b_lr, gla_norm_w, ret_norm_w, ret_norm_b, w_out, ln_w, ln_b):
    B, L, _ = x.shape
    f32 = jnp.float32
    h = jnp.einsum('bld,de->ble', x, w_in)
    qa, ka, va, za, lra, qb, kb, vb, zb, ga, gb = _split_cols(h)

    qa = qa.reshape(B, L, GLA_HEADS, GLA_DK) * (GLA_DK ** -0.5)
    ka = ka.reshape(B, L, GLA_HEADS, GLA_DK)
    va = va.reshape(B, L, GLA_HEADS, GLA_DV)
    gate_logit = (jnp.einsum('blr,rk->blk', lra, w_lr) + b_lr).astype(f32)
    log_a = (jax.nn.log_sigmoid(gate_logit) / GLA_TAU).reshape(B, L, GLA_HEADS, GLA_DK)
    o_a, s_gla = gla_chunked(qa, ka, va, log_a, s_gla0)
    o_a = _rms_heads(o_a, gla_norm_w) * jax.nn.silu(za.astype(f32))

    qb = _rotate(qb.reshape(B, L, RET_HEADS, RET_DK), pos)
    kb = _rotate(kb.reshape(B, L, RET_HEADS, RET_DK), pos) * (RET_DK ** -0.5)
    vb = vb.reshape(B, L, RET_HEADS, RET_DV)
    log_g = jnp.broadcast_to(_ret_log_decay(), (B, L, RET_HEADS))
    o_b, s_ret = retention_chunked(qb, kb, vb, log_g, s_ret0)
    o_b = _group_norm(o_b, ret_norm_w, ret_norm_b) * jax.nn.silu(zb.astype(f32))

    merged = jax.nn.sigmoid(ga.astype(f32)) * o_a + jax.nn.sigmoid(gb.astype(f32)) * o_b
    y = jnp.einsum('ble,ed->bld', merged.astype(x.dtype), w_out)
    x_new = _layer_norm(DEEPNORM_ALPHA * x + y, ln_w, ln_b)
    return x_new.astype(x.dtype), s_gla, s_ret


def setup_inputs(seed: int = 0) -> dict:
    key = jax.random.key(seed)
    ks = jax.random.split(key, 20)
    f32 = jnp.float32
    x_prompt = jax.random.normal(ks[0], (BATCH, SEQ, D_MODEL), f32)
    x_sample = jax.random.normal(ks[1], (DEC_BATCH, DEC_SEQ, D_MODEL), f32)
    state_gla = 0.5 * jax.random.normal(ks[2], (DEPTH, DEC_BATCH, GLA_HEADS, GLA_DK, GLA_DV), f32)
    state_ret = 0.5 * jax.random.normal(ks[3], (DEPTH, DEC_BATCH, RET_HEADS, RET_DK, RET_DV), f32)
    col_scale = []
    for i, s in enumerate(SPLITS):
        scale = DEEPNORM_BETA if i in (2, 7) else 1.0
        col_scale.append(jnp.full((s,), scale, f32))
    col_scale = jnp.concatenate(col_scale)
    w_in = jax.random.normal(ks[4], (DEPTH, D_MODEL, D_IN), f32) * (D_MODEL ** -0.5) * col_scale
    w_lr = jax.random.normal(ks[5], (DEPTH, GLA_RANK, QK_A), f32) * (GLA_RANK ** -0.5)
    b_lr = 1.0 + 0.5 * jax.random.normal(ks[6], (DEPTH, QK_A), f32)
    gla_norm_w = 1.0 + 0.02 * jax.random.normal(ks[7], (DEPTH, GLA_DV), f32)
    ret_norm_w = 1.0 + 0.02 * jax.random.normal(ks[8], (DEPTH, V_B), f32)
    ret_norm_b = 0.02 * jax.random.normal(ks[9], (DEPTH, V_B), f32)
    w_out = jax.random.normal(ks[10], (DEPTH, V_A, D_MODEL), f32) * (V_A ** -0.5) * DEEPNORM_BETA
    ln_w = 1.0 + 0.02 * jax.random.normal(ks[11], (DEPTH, D_MODEL), f32)
    ln_b = 0.02 * jax.random.normal(ks[12], (DEPTH, D_MODEL), f32)
    return {"x_prompt": x_prompt, "x_sample": x_sample, "state_gla": state_gla, "state_ret": state_ret,
            "w_in": w_in, "w_lr": w_lr, "b_lr": b_lr, "gla_norm_w": gla_norm_w,
            "ret_norm_w": ret_norm_w, "ret_norm_b": ret_norm_b, "w_out": w_out,
            "ln_w": ln_w, "ln_b": ln_b}


def reference(x_prompt, x_sample, state_gla, state_ret, w_in, w_lr, b_lr, gla_norm_w,
              ret_norm_w, ret_norm_b, w_out, ln_w, ln_b):
    pos_prompt = jnp.arange(x_prompt.shape[1], dtype=jnp.int32)
    pos_sample = PAST_LEN + jnp.arange(x_sample.shape[1], dtype=jnp.int32)
    hp, hs = x_prompt, x_sample
    gla_p, ret_p, gla_s, ret_s = [], [], [], []
    for l in range(DEPTH):
        params = (w_in[l], w_lr[l], b_lr[l], gla_norm_w[l], ret_norm_w[l], ret_norm_b[l],
                  w_out[l], ln_w[l], ln_b[l])
        zero_gla = jnp.zeros((hp.shape[0], GLA_HEADS, GLA_DK, GLA_DV), state_gla.dtype)
        zero_ret = jnp.zeros((hp.shape[0], RET_HEADS, RET_DK, RET_DV), state_ret.dtype)
        hp, sg_p, sr_p = hybrid_layer(hp, pos_prompt, zero_gla, zero_ret, *params)
        hs, sg_s, sr_s = hybrid_layer(hs, pos_sample, state_gla[l], state_ret[l], *params)
        gla_p.append(sg_p)
        ret_p.append(sr_p)
        gla_s.append(sg_s)
        ret_s.append(sr_s)
    new_gla_prompt = jnp.stack(gla_p)
    new_ret_prompt = jnp.stack(ret_p)
    new_gla_sample = jnp.stack(gla_s)
    new_ret_sample = jnp.stack(ret_s)
    return (hp, hs, new_gla_prompt, new_ret_prompt, new_gla_sample, new_ret_sample)
```

```python
import functools

import jax
import jax.numpy as jnp
from jax import lax
from jax.experimental import pallas as pl
from jax.experimental.pallas import tpu as pltpu

GLA_HEADS = 4
RET_HEADS = 8
HEAD_GROUPS = 4
GLA_TAU = 16.0
ROPE_BASE = 10000.0
LN_EPS = 1e-5
HEAD_NORM_EPS = 1e-6
PAST_LEN = 16384

LANES = 128
VMEM_LIMIT_BYTES = 56 * 1024 * 1024

PROMPT_CHUNK = 64
GLA_SUB = 16
SAMPLE_SEQS = 4

_F32 = jnp.float32
_BF16 = jnp.bfloat16
_NT = (((1,), (1,)), ((), ()))
_TN = (((0,), (0,)), ((), ()))


def _dot(a, b):
    return jnp.dot(a.astype(_BF16), b.astype(_BF16), preferred_element_type=_F32)


def _dot_nt(a, b):
    return lax.dot_general(a.astype(_BF16), b.astype(_BF16), _NT, preferred_element_type=_F32)


def _dot_tn(a, b):
    return lax.dot_general(a.astype(_BF16), b.astype(_BF16), _TN, preferred_element_type=_F32)


def _log_sigmoid(x):
    return jnp.minimum(x, 0.0) - jnp.log1p(jnp.exp(-jnp.abs(x)))


def _silu(x):
    return x * jax.nn.sigmoid(x)


def _iota(shape, axis):
    return lax.broadcasted_iota(jnp.int32, shape, axis)


def _seg_cumsum(x, seg):
    pos = _iota(x.shape, 0) & (seg - 1)
    s = 1
    while s < seg:
        x = x + jnp.where(pos >= s, pltpu.roll(x, s, axis=0), 0.0)
        s *= 2
    return x


def _rows_to_cols(rows):
    n = rows[0].shape[1]
    rid = _iota((LANES, n), 0)
    stack = jnp.zeros((LANES, n), _F32)
    for i, r in enumerate(rows):
        stack = jnp.where(rid == i, r, stack)
    return stack.T


def _rope(x, cos, sin_signed):
    w = x.shape[1]
    even = (_iota(x.shape, 1) & 1) == 0
    swapped = jnp.where(even, pltpu.roll(x, w - 1, axis=1), pltpu.roll(x, 1, axis=1))
    return x * cos + swapped * sin_signed


def _gla_pair_scores(a, q, k, b, lane0, row_lo_hi):
    n = q.shape[0]
    rows = _iota((n, 1), 0)
    lanes = _iota(a.shape, 1)
    for s in range(n):
        lo, hi = row_lo_hi(s)
        ok = (rows >= lo) & (rows < hi)
        d = jnp.where(ok, b - b[s:s + 1, :], -jnp.inf)
        col = jnp.sum(q * k[s:s + 1, :] * jnp.exp(d), axis=-1, keepdims=True)
        a = jnp.where(lanes == lane0 + s, col, a)
    return a


def _rms_heads(o, w):
    return o * lax.rsqrt(jnp.mean(o * o, axis=-1, keepdims=True) + HEAD_NORM_EPS) * w


def _group_norm(o, w, b):
    mu = jnp.mean(o, axis=-1, keepdims=True)
    d = o - mu
    var = jnp.mean(d * d, axis=-1, keepdims=True)
    return d * lax.rsqrt(var + HEAD_NORM_EPS) * w + b


def _gate_log_decay(lr_ref, wlr_ref, blr_ref):
    logit = jnp.dot(lr_ref[...], wlr_ref[...], precision=lax.Precision.HIGHEST,
                    preferred_element_type=_F32) + blr_ref[...]
    return _log_sigmoid(logit) * (1.0 / GLA_TAU)


def _proj_in_kernel(x_ref, w_ref, wlr_ref, h_ref, lr_ref, xb_ref):
    @pl.when(pl.program_id(1) == 0)
    def _():
        xb = x_ref[...].astype(_BF16)
        xb_ref[...] = xb
        lr_ref[...] = jnp.dot(xb, wlr_ref[...], preferred_element_type=_F32)

    h_ref[...] = jnp.dot(xb_ref[...], w_ref[...], preferred_element_type=_F32)


def _proj_in(x2d, w_main, w_lra, tm, tn):
    m, kdim = x2d.shape
    n = w_main.shape[1]
    return pl.pallas_call(
        _proj_in_kernel,
        grid=(m // tm, n // tn),
        in_specs=[
            pl.BlockSpec((tm, kdim), lambda i, j: (i, 0)),
            pl.BlockSpec((kdim, tn), lambda i, j: (0, j)),
            pl.BlockSpec((kdim, LANES), lambda i, j: (0, 0)),
        ],
        out_specs=[
            pl.BlockSpec((tm, tn), lambda i, j: (i, j)),
            pl.BlockSpec((tm, LANES), lambda i, j: (i, 0)),
        ],
        out_shape=[
            jax.ShapeDtypeStruct((m, n), _F32),
            jax.ShapeDtypeStruct((m, LANES), _F32),
        ],
        scratch_shapes=[pltpu.VMEM((tm, kdim), _BF16)],
        compiler_params=pltpu.CompilerParams(
            dimension_semantics=("arbitrary", "arbitrary"),
            vmem_limit_bytes=VMEM_LIMIT_BYTES),
        name="proj_in",
    )(x2d, w_main, w_lra)


def _mixer_prompt_kernel(qa_ref, ka_ref, va_ref, za_ref, qb_ref, kb_ref, vb_ref, zb_ref,
                         ga_ref, gb_ref, lr_ref, wlr_ref, blr_ref, gnw_ref, rnw_ref, rnb_ref,
                         cos_ref, sin_ref, lg_ref, m_ref, sg_ref, sr_ref):
    chunk, dk_a = qa_ref.shape
    dk_b = qb_ref.shape[1] // 2
    dv_b = vb_ref.shape[1] // 2

    @pl.when(pl.program_id(2) == 0)
    def _():
        sg_ref[...] = jnp.zeros(sg_ref.shape, _F32)
        sr_ref[...] = jnp.zeros(sr_ref.shape, _F32)

    la = _gate_log_decay(lr_ref, wlr_ref, blr_ref)
    b = _seg_cumsum(la, chunk)
    q = qa_ref[...] * (dk_a ** -0.5)
    k = ka_ref[...]
    v = va_ref[...].astype(_BF16)
    s_gla = sg_ref[0, 0, 0]
    o_a = _dot(q * jnp.exp(b), s_gla)
    rows = _iota((chunk, 1), 0)
    intra = []
    for i in range(chunk // GLA_SUB):
        lo = i * GLA_SUB
        qi, ki, bi = q[lo:lo + GLA_SUB], k[lo:lo + GLA_SUB], b[lo:lo + GLA_SUB]
        if i == 0:
            a = jnp.zeros((GLA_SUB, chunk), _F32)
        else:
            r = b[lo - 1:lo, :]
            k_pre = k * jnp.exp(jnp.where(rows < lo, r - b, -jnp.inf))
            a = _dot_nt(qi * jnp.exp(bi - r), k_pre)
        a = _gla_pair_scores(a, qi, ki, bi, lo, lambda s: (s, GLA_SUB))
        intra.append(_dot(a, v))
    o_a = o_a + jnp.concatenate(intra, axis=0)
    b_last = b[chunk - 1:chunk, :]
    decay_col = _rows_to_cols([jnp.exp(b_last)])[:, 0:1]
    sg_ref[0, 0, 0] = decay_col * s_gla + _dot_tn(k * jnp.exp(b_last - b), v)
    o_a = _rms_heads(o_a, gnw_ref[...]) * _silu(za_ref[...])

    cos, sin = cos_ref[...], sin_ref[...]
    qb = _rope(qb_ref[...], cos, sin)
    kb = _rope(kb_ref[...], cos, sin) * (dk_b ** -0.5)
    vb = vb_ref[...].astype(_BF16)
    lg = lg_ref[0]
    tpos = (_iota(qb.shape, 0) + 1).astype(_F32)
    q_dec = qb * jnp.exp(lg * tpos)
    k_dec = kb * jnp.exp(lg * (chunk - tpos))
    tdiff = _iota((chunk, chunk), 0) - _iota((chunk, chunk), 1)
    o_b = []
    for hh in range(2):
        ksl = slice(hh * dk_b, (hh + 1) * dk_b)
        vsl = slice(hh * dv_b, (hh + 1) * dv_b)
        lg1 = lg[:, hh * dk_b:hh * dk_b + 1]
        decay = jnp.exp(jnp.where(tdiff >= 0, lg1 * tdiff.astype(_F32), -jnp.inf))
        s_ret = sr_ref[0, 0, hh]
        scores = _dot_nt(qb[:, ksl], kb[:, ksl]) * decay
        o_h = _dot(q_dec[:, ksl], s_ret) + _dot(scores, vb[:, vsl])
        sr_ref[0, 0, hh] = jnp.exp(lg1 * chunk) * s_ret + _dot_tn(k_dec[:, ksl], vb[:, vsl])
        o_b.append(_group_norm(o_h, rnw_ref[:, vsl], rnb_ref[:, vsl]))
    o_b = jnp.concatenate(o_b, axis=1) * _silu(zb_ref[...])

    merged = jax.nn.sigmoid(ga_ref[...]) * o_a + jax.nn.sigmoid(gb_ref[...]) * o_b
    m_ref[...] = merged.astype(m_ref.dtype)


def _h_specs(rows, wq, wv, d_model, row_map):
    def spec(width, col_off):
        base = col_off // width
        return pl.BlockSpec((rows, width), lambda *ids: (row_map(*ids), base + ids[1]))
    d = d_model
    return [spec(wq, 0), spec(wq, d // 2), spec(wv, d), spec(wv, 2 * d),
            spec(wq, 3 * d), spec(wq, 3 * d + d // 2), spec(wv, 4 * d), spec(wv, 5 * d),
            spec(wv, 6 * d), spec(wv, 7 * d)]


def _param_specs(wq, wv):
    return [
        pl.BlockSpec((LANES, wq), lambda *ids: (0, ids[1])),
        pl.BlockSpec((1, wq), lambda *ids: (0, ids[1])),
        pl.BlockSpec((1, wv), lambda *ids: (0, 0)),
        pl.BlockSpec((1, wv), lambda *ids: (0, ids[1])),
        pl.BlockSpec((1, wv), lambda *ids: (0, ids[1])),
    ]


def _mixer_prompt(h, lr, params, rope, lg, batch, seq, d_model):
    wq, wv = d_model // 8, d_model // 4
    dk_a, dv_a = wq, wv
    dk_b, dv_b = wq // 2, wv // 2
    chunk = PROMPT_CHUNK
    nc = seq // chunk
    row_map = lambda b, g, c: b * nc + c
    in_specs = _h_specs(chunk, wq, wv, d_model, row_map)
    in_specs += [pl.BlockSpec((chunk, LANES), lambda b, g, c: (b * nc + c, 0))]
    in_specs += _param_specs(wq, wv)
    in_specs += [pl.BlockSpec((chunk, wq), lambda b, g, c: (c, 0)),
                 pl.BlockSpec((chunk, wq), lambda b, g, c: (c, 0)),
                 pl.BlockSpec((1, 1, wq), lambda b, g, c: (g, 0, 0))]
    out_specs = [
        pl.BlockSpec((chunk, wv), lambda b, g, c: (b * nc + c, g)),
        pl.BlockSpec((1, 1, 1, dk_a, dv_a), lambda b, g, c: (0, b, g, 0, 0)),
        pl.BlockSpec((1, 1, 2, dk_b, dv_b), lambda b, g, c: (0, b, g, 0, 0)),
    ]
    out_shape = [
        jax.ShapeDtypeStruct((batch * seq, d_model), _BF16),
        jax.ShapeDtypeStruct((1, batch, GLA_HEADS, dk_a, dv_a), _F32),
        jax.ShapeDtypeStruct((1, batch, RET_HEADS, dk_b, dv_b), _F32),
    ]
    return pl.pallas_call(
        _mixer_prompt_kernel,
        grid=(batch, HEAD_GROUPS, nc),
        in_specs=in_specs,
        out_specs=out_specs,
        out_shape=out_shape,
        compiler_params=pltpu.CompilerParams(
            dimension_semantics=("arbitrary", "arbitrary", "arbitrary"),
            vmem_limit_bytes=VMEM_LIMIT_BYTES),
        name="mixer_prompt",
    )(*([h] * 10), lr, *params, *rope, lg)


def _mixer_sample_kernel(qa_ref, ka_ref, va_ref, za_ref, qb_ref, kb_ref, vb_ref, zb_ref,
                         ga_ref, gb_ref, lr_ref, wlr_ref, blr_ref, gnw_ref, rnw_ref, rnb_ref,
                         cos_ref, sin_ref, lg_ref, sg0_ref, sr0_ref, m_ref, sg_ref, sr_ref,
                         *, steps):
    rows_n, dk_a = qa_ref.shape
    nseq = rows_n // steps
    dk_b = qb_ref.shape[1] // 2
    dv_b = vb_ref.shape[1] // 2
    rows = _iota((rows_n, 1), 0)
    pad = LANES - rows_n

    def pad_rows(x):
        return jnp.concatenate([x, jnp.zeros((pad, x.shape[1]), x.dtype)], axis=0)

    def seq_rows(x, n):
        return jnp.where((rows >= n * steps) & (rows < (n + 1) * steps), x, 0.0)

    la = _gate_log_decay(lr_ref, wlr_ref, blr_ref)
    b = _seg_cumsum(la, steps)
    q = qa_ref[...] * (dk_a ** -0.5)
    k = ka_ref[...]
    v = pad_rows(va_ref[...]).astype(_BF16)
    b_last = [b[(n + 1) * steps - 1:(n + 1) * steps, :] for n in range(nseq)]
    b_end = jnp.zeros_like(b)
    for n in range(nseq):
        b_end = b_end + seq_rows(jnp.broadcast_to(b_last[n], b.shape), n)
    a = jnp.zeros((rows_n, LANES), _F32)
    a = _gla_pair_scores(a, q, k, b, 0, lambda s: (s, (s // steps + 1) * steps))
    o_a = _dot(a, v)
    q_dec = q * jnp.exp(b)
    k_dec = pad_rows(k * jnp.exp(b_end - b))
    decay_cols = _rows_to_cols([jnp.exp(r) for r in b_last])
    for n in range(nseq):
        s0 = sg0_ref[0, n, 0]
        o_a = o_a + _dot(seq_rows(q_dec, n), s0)
        k_n = jnp.where((_iota(k_dec.shape, 0) >= n * steps) & (_iota(k_dec.shape, 0) < (n + 1) * steps),
                        k_dec, 0.0)
        sg_ref[0, n, 0] = decay_cols[:, n:n + 1] * s0 + _dot_tn(k_n, v)
    o_a = _rms_heads(o_a, gnw_ref[...]) * _silu(za_ref[...])

    cos, sin = cos_ref[...], sin_ref[...]
    qb = _rope(qb_ref[...], cos, sin)
    kb = _rope(kb_ref[...], cos, sin) * (dk_b ** -0.5)
    vb = pad_rows(vb_ref[...]).astype(_BF16)
    lg = lg_ref[0]
    tpos = ((_iota(qb.shape, 0) & (steps - 1)) + 1).astype(_F32)
    q_dec = qb * jnp.exp(lg * tpos)
    k_dec = pad_rows(kb * jnp.exp(lg * (steps - tpos)))
    kb_p = pad_rows(kb)
    r_i = _iota((rows_n, LANES), 0)
    c_i = _iota((rows_n, LANES), 1)
    same_seq = (r_i >> (steps.bit_length() - 1)) == (c_i >> (steps.bit_length() - 1))
    tdiff = r_i - c_i
    o_b = []
    for hh in range(2):
        ksl = slice(hh * dk_b, (hh + 1) * dk_b)
        vsl = slice(hh * dv_b, (hh + 1) * dv_b)
        lg1 = lg[:, hh * dk_b:hh * dk_b + 1]
        decay = jnp.exp(jnp.where((tdiff >= 0) & same_seq, lg1 * tdiff.astype(_F32), -jnp.inf))
        scores = _dot_nt(qb[:, ksl], kb_p[:, ksl]) * decay
        o_h = _dot(scores, vb[:, vsl])
        for n in range(nseq):
            s0 = sr0_ref[0, n, hh]
            o_h = o_h + _dot(seq_rows(q_dec[:, ksl], n), s0)
            kd = k_dec[:, ksl]
            k_n = jnp.where((_iota(kd.shape, 0) >= n * steps) & (_iota(kd.shape, 0) < (n + 1) * steps),
                            kd, 0.0)
            sr_ref[0, n, hh] = jnp.exp(lg1 * steps) * s0 + _dot_tn(k_n, vb[:, vsl])
        o_b.append(_group_norm(o_h, rnw_ref[:, vsl], rnb_ref[:, vsl]))
    o_b = jnp.concatenate(o_b, axis=1) * _silu(zb_ref[...])

    merged = jax.nn.sigmoid(ga_ref[...]) * o_a + jax.nn.sigmoid(gb_ref[...]) * o_b
    m_ref[...] = merged.astype(m_ref.dtype)


def _mixer_sample(h, lr, params, rope, lg, state_gla, state_ret, layer, nseq_total, steps, d_model):
    wq, wv = d_model // 8, d_model // 4
    dk_a, dv_a = wq, wv
    dk_b, dv_b = wq // 2, wv // 2
    nseq = SAMPLE_SEQS
    rows_n = nseq * steps
    assert steps & (steps - 1) == 0 and nseq_total % nseq == 0
    row_map = lambda p, g: p
    in_specs = _h_specs(rows_n, wq, wv, d_model, row_map)
    in_specs += [pl.BlockSpec((rows_n, LANES), lambda p, g: (p, 0))]
    in_specs += _param_specs(wq, wv)
    in_specs += [pl.BlockSpec((rows_n, wq), lambda p, g: (0, 0)),
                 pl.BlockSpec((rows_n, wq), lambda p, g: (0, 0)),
                 pl.BlockSpec((1, 1, wq), lambda p, g: (g, 0, 0)),
                 pl.BlockSpec((1, nseq, 1, dk_a, dv_a), lambda p, g: (layer, p, g, 0, 0)),
                 pl.BlockSpec((1, nseq, 2, dk_b, dv_b), lambda p, g: (layer, p, g, 0, 0))]
    out_specs = [
        pl.BlockSpec((rows_n, wv), lambda p, g: (p, g)),
        pl.BlockSpec((1, nseq, 1, dk_a, dv_a), lambda p, g: (0, p, g, 0, 0)),
        pl.BlockSpec((1, nseq, 2, dk_b, dv_b), lambda p, g: (0, p, g, 0, 0)),
    ]
    out_shape = [
        jax.ShapeDtypeStruct((nseq_total * steps, d_model), _BF16),
        jax.ShapeDtypeStruct((1, nseq_total, GLA_HEADS, dk_a, dv_a), _F32),
        jax.ShapeDtypeStruct((1, nseq_total, RET_HEADS, dk_b, dv_b), _F32),
    ]
    return pl.pallas_call(
        functools.partial(_mixer_sample_kernel, steps=steps),
        grid=(nseq_total // nseq, HEAD_GROUPS),
        in_specs=in_specs,
        out_specs=out_specs,
        out_shape=out_shape,
        compiler_params=pltpu.CompilerParams(
            dimension_semantics=("arbitrary", "arbitrary"),
            vmem_limit_bytes=VMEM_LIMIT_BYTES),
        name="mixer_sample",
    )(*([h] * 10), lr, *params, *rope, lg, state_gla, state_ret)


def _proj_out_kernel(m_ref, x_ref, w_ref, lnw_ref, lnb_ref, y_ref, *, alpha):
    y = jnp.dot(m_ref[...], w_ref[...], preferred_element_type=_F32)
    r = alpha * x_ref[...] + y
    mu = jnp.mean(r, axis=-1, keepdims=True)
    d = r - mu
    var = jnp.mean(d * d, axis=-1, keepdims=True)
    y_ref[...] = d * lax.rsqrt(var + LN_EPS) * lnw_ref[...] + lnb_ref[...]


def _proj_out(merged, x2d, w_out, ln_w, ln_b, alpha, tm):
    m, d = x2d.shape
    e = merged.shape[1]
    return pl.pallas_call(
        functools.partial(_proj_out_kernel, alpha=alpha),
        grid=(m // tm,),
        in_specs=[
            pl.BlockSpec((tm, e), lambda i: (i, 0)),
            pl.BlockSpec((tm, d), lambda i: (i, 0)),
            pl.BlockSpec((e, d), lambda i: (0, 0)),
            pl.BlockSpec((1, d), lambda i: (0, 0)),
            pl.BlockSpec((1, d), lambda i: (0, 0)),
        ],
        out_specs=pl.BlockSpec((tm, d), lambda i: (i, 0)),
        out_shape=jax.ShapeDtypeStruct((m, d), _F32),
        compiler_params=pltpu.CompilerParams(
            dimension_semantics=("arbitrary",),
            vmem_limit_bytes=VMEM_LIMIT_BYTES),
        name="proj_out",
    )(merged, x2d, w_out, ln_w, ln_b)


def _rope_tables(pos, dk):
    inv = 1.0 / (ROPE_BASE ** jnp.linspace(0.0, 1.0, dk // 2, dtype=_F32))
    ang = pos.astype(_F32)[:, None] * inv[None, :]
    cos = jnp.repeat(jnp.cos(ang), 2, axis=1)
    sin = jnp.stack([-jnp.sin(ang), jnp.sin(ang)], axis=-1).reshape(ang.shape[0], dk)
    return jnp.tile(cos, (1, 2)), jnp.tile(sin, (1, 2))


def _pick_tile(n, pref):
    t = min(n, pref)
    while n % t:
        t //= 2
    return t


def kernel(x_prompt, x_sample, state_gla, state_ret, w_in, w_lr, b_lr, gla_norm_w,
           ret_norm_w, ret_norm_b, w_out, ln_w, ln_b):
    depth, d_model, _ = w_in.shape
    batch, seq, _ = x_prompt.shape
    dec_batch, dec_seq, _ = x_sample.shape
    rank = w_lr.shape[1]
    dk_b = d_model // 2 // RET_HEADS
    assert state_gla.shape[2] == GLA_HEADS and state_ret.shape[2] == RET_HEADS
    assert rank <= LANES and seq % PROMPT_CHUNK == 0
    alpha = (2.0 * depth) ** 0.25

    lg_heads = jnp.log(1.0 - 2.0 ** (-5.0 - jnp.arange(RET_HEADS, dtype=_F32)))
    lg = jnp.repeat(lg_heads, dk_b).reshape(HEAD_GROUPS, 1, 2 * dk_b)
    rope_p = _rope_tables(jnp.arange(seq, dtype=jnp.int32), dk_b)
    pos_s = PAST_LEN + jnp.arange(dec_seq, dtype=jnp.int32)
    rope_s = _rope_tables(jnp.tile(pos_s, SAMPLE_SEQS), dk_b)

    hp = x_prompt.reshape(batch * seq, d_model)
    hs = x_sample.reshape(dec_batch * dec_seq, d_model)
    gla_p, ret_p, gla_s, ret_s = [], [], [], []
    for l in range(depth):
        lr_lo = 3 * d_model
        w_main = jnp.concatenate([w_in[l][:, :lr_lo], w_in[l][:, lr_lo + rank:]], axis=1).astype(_BF16)
        w_lra = jnp.pad(w_in[l][:, lr_lo:lr_lo + rank], ((0, 0), (0, LANES - rank))).astype(_BF16)
        w_o = w_out[l].astype(_BF16)
        params = (jnp.pad(w_lr[l], ((0, LANES - rank), (0, 0))), b_lr[l][None, :],
                  gla_norm_w[l][None, :], ret_norm_w[l][None, :], ret_norm_b[l][None, :])
        lnw, lnb = ln_w[l][None, :], ln_b[l][None, :]

        h, lr = _proj_in(hp, w_main, w_lra, _pick_tile(hp.shape[0], 1024), 1024)
        merged, sg, sr = _mixer_prompt(h, lr, params, rope_p, lg, batch, seq, d_model)
        hp = _proj_out(merged, hp, w_o, lnw, lnb, alpha, _pick_tile(hp.shape[0], 512))
        gla_p.append(sg[0])
        ret_p.append(sr[0])

        h, lr = _proj_in(hs, w_main, w_lra, _pick_tile(hs.shape[0], 1024), 1024)
        merged, sg, sr = _mixer_sample(h, lr, params, rope_s, lg, state_gla, state_ret, l,
                                       dec_batch, dec_seq, d_model)
        hs = _proj_out(merged, hs, w_o, lnw, lnb, alpha, _pick_tile(hs.shape[0], 256))
        gla_s.append(sg[0])
        ret_s.append(sr[0])

    return (hp.reshape(batch, seq, d_model), hs.reshape(dec_batch, dec_seq, d_model),
            jnp.stack(gla_p), jnp.stack(ret_p), jnp.stack(gla_s), jnp.stack(ret_s))
```

```python
import functools

import jax
import jax.numpy as jnp
from jax import lax
from jax.experimental import pallas as pl
from jax.experimental.pallas import tpu as pltpu

GLA_HEADS = 4
RET_HEADS = 8
HEAD_GROUPS = 4
GLA_TAU = 16.0
ROPE_BASE = 10000.0
LN_EPS = 1e-5
HEAD_NORM_EPS = 1e-6
PAST_LEN = 16384

LANES = 128
SUBLANES = 8
VMEM_LIMIT_BYTES = 56 * 1024 * 1024

PROMPT_CHUNK = 64
GLA_SUB = 16
SAMPLE_SEQS = 4

_F32 = jnp.float32
_BF16 = jnp.bfloat16
_NT = (((1,), (1,)), ((), ()))
_TN = (((0,), (0,)), ((), ()))


def _dot(a, b):
    return jnp.dot(a.astype(_BF16), b.astype(_BF16), preferred_element_type=_F32)


def _dot_nt(a, b):
    return lax.dot_general(a.astype(_BF16), b.astype(_BF16), _NT, preferred_element_type=_F32)


def _dot_tn(a, b):
    return lax.dot_general(a.astype(_BF16), b.astype(_BF16), _TN, preferred_element_type=_F32)


def _log_sigmoid(x):
    return jnp.minimum(x, 0.0) - jnp.log(1.0 + jnp.exp(-jnp.abs(x)))


def _silu(x):
    return x * jax.nn.sigmoid(x)


def _iota(shape, axis):
    return lax.broadcasted_iota(jnp.int32, shape, axis)


def _seg_cumsum(x, seg):
    pos = _iota(x.shape, 0) & (seg - 1)
    s = 1
    while s < seg:
        x = x + jnp.where(pos >= s, pltpu.roll(x, s, axis=0), 0.0)
        s *= 2
    return x


def _rows_to_cols(rows):
    n = rows[0].shape[1]
    rid = _iota((LANES, n), 0)
    stack = jnp.zeros((LANES, n), _F32)
    for i, r in enumerate(rows):
        stack = jnp.where(rid == i, r, stack)
    return stack.T


def _rope(x, cos, sin_signed):
    w = x.shape[1]
    even = (_iota(x.shape, 1) & 1) == 0
    swapped = jnp.where(even, pltpu.roll(x, w - 1, axis=1), pltpu.roll(x, 1, axis=1))
    return x * cos + swapped * sin_signed


def _gla_pair_scores(a, q, k, b, lane0, row_lo_hi):
    n = q.shape[0]
    lanes = _iota((SUBLANES, a.shape[1]), 1)
    tiles = [a[r:r + SUBLANES] for r in range(0, n, SUBLANES)]
    for s in range(n):
        lo, hi = row_lo_hi(s)
        for j, r0 in enumerate(range(0, n, SUBLANES)):
            if r0 + SUBLANES <= lo or r0 >= hi:
                continue
            rows = _iota((SUBLANES, 1), 0) + r0
            e = jnp.exp(b[r0:r0 + SUBLANES] - b[s:s + 1, :])
            col = jnp.sum(q[r0:r0 + SUBLANES] * k[s:s + 1, :] * e, axis=-1, keepdims=True)
            col = jnp.where((rows >= lo) & (rows < hi), col, 0.0)
            tiles[j] = jnp.where(lanes == lane0 + s, col, tiles[j])
    return jnp.concatenate(tiles, axis=0)


def _rms_heads(o, w):
    return o * lax.rsqrt(jnp.mean(o * o, axis=-1, keepdims=True) + HEAD_NORM_EPS) * w


def _group_norm(o, w, b):
    mu = jnp.mean(o, axis=-1, keepdims=True)
    d = o - mu
    var = jnp.mean(d * d, axis=-1, keepdims=True)
    return d * lax.rsqrt(var + HEAD_NORM_EPS) * w + b


def _gate_log_decay(lr_ref, wlr_ref, blr_ref):
    logit = jnp.dot(lr_ref[...], wlr_ref[...], precision=lax.Precision.HIGHEST,
                    preferred_element_type=_F32) + blr_ref[...]
    return _log_sigmoid(logit) * (1.0 / GLA_TAU)


def _proj_in_kernel(x_ref, w_ref, wlr_ref, h_ref, lr_ref, xb_ref):
    @pl.when(pl.program_id(1) == 0)
    def _():
        xb = x_ref[...].astype(_BF16)
        xb_ref[...] = xb
        lr_ref[...] = jnp.dot(xb, wlr_ref[...], preferred_element_type=_F32)

    h_ref[...] = jnp.dot(xb_ref[...], w_ref[...], preferred_element_type=_F32)


def _proj_in(x2d, w_main, w_lra, tm, tn):
    m, kdim = x2d.shape
    n = w_main.shape[1]
    return pl.pallas_call(
        _proj_in_kernel,
        grid=(m // tm, n // tn),
        in_specs=[
            pl.BlockSpec((tm, kdim), lambda i, j: (i, 0)),
            pl.BlockSpec((kdim, tn), lambda i, j: (0, j)),
            pl.BlockSpec((kdim, LANES), lambda i, j: (0, 0)),
        ],
        out_specs=[
            pl.BlockSpec((tm, tn), lambda i, j: (i, j)),
            pl.BlockSpec((tm, LANES), lambda i, j: (i, 0)),
        ],
        out_shape=[
            jax.ShapeDtypeStruct((m, n), _F32),
            jax.ShapeDtypeStruct((m, LANES), _F32),
        ],
        scratch_shapes=[pltpu.VMEM((tm, kdim), _BF16)],
        compiler_params=pltpu.CompilerParams(
            dimension_semantics=("arbitrary", "arbitrary"),
            vmem_limit_bytes=VMEM_LIMIT_BYTES),
        name="proj_in",
    )(x2d, w_main, w_lra)


def _h_cols(d_model, g):
    wq, wv = d_model // 8, d_model // 4
    d = d_model
    offs = dict(qa=(0, wq), ka=(d // 2, wq), va=(d, wv), za=(2 * d, wv),
                qb=(3 * d, wq), kb=(3 * d + d // 2, wq), vb=(4 * d, wv), zb=(5 * d, wv),
                ga=(6 * d, wv), gb=(7 * d, wv))
    return {n: slice(o + g * w, o + (g + 1) * w) for n, (o, w) in offs.items()}


def _mixer_prompt_kernel(h_ref, lr_ref, wlr_ref, blr_ref, gnw_ref, rnw_ref, rnb_ref,
                         cos_ref, sin_ref, lg_ref, m_ref, sg_ref, sr_ref):
    chunk = h_ref.shape[0]
    d_model = m_ref.shape[1]
    dk_a, dv_a = sg_ref.shape[3:]
    dk_b, dv_b = sr_ref.shape[3:]

    @pl.when(pl.program_id(1) == 0)
    def _():
        sg_ref[...] = jnp.zeros(sg_ref.shape, _F32)
        sr_ref[...] = jnp.zeros(sr_ref.shape, _F32)

    rows = _iota((chunk, 1), 0)
    cos, sin = cos_ref[...], sin_ref[...]
    tpos = (_iota(cos.shape, 0) + 1).astype(_F32)
    tdiff = _iota((chunk, chunk), 0) - _iota((chunk, chunk), 1)
    for g in range(HEAD_GROUPS):
        c = _h_cols(d_model, g)
        gq = slice(g * dk_a, (g + 1) * dk_a)
        gv = slice(g * dv_a, (g + 1) * dv_a)

        logit = jnp.dot(lr_ref[...], wlr_ref[:, gq], precision=lax.Precision.HIGHEST,
                        preferred_element_type=_F32) + blr_ref[:, gq]
        la = _log_sigmoid(logit) * (1.0 / GLA_TAU)
        b = _seg_cumsum(la, chunk)
        q = h_ref[:, c["qa"]] * (dk_a ** -0.5)
        k = h_ref[:, c["ka"]]
        v = h_ref[:, c["va"]].astype(_BF16)
        s_gla = sg_ref[0, 0, g]
        o_a = _dot(q * jnp.exp(b), s_gla)
        intra = []
        for i in range(chunk // GLA_SUB):
            lo = i * GLA_SUB
            qi, ki, bi = q[lo:lo + GLA_SUB], k[lo:lo + GLA_SUB], b[lo:lo + GLA_SUB]
            if i == 0:
                a = jnp.zeros((GLA_SUB, chunk), _F32)
            else:
                r = b[lo - 1:lo, :]
                k_pre = k * jnp.exp(jnp.where(rows < lo, r - b, -jnp.inf))
                a = _dot_nt(qi * jnp.exp(bi - r), k_pre)
            a = _gla_pair_scores(a, qi, ki, bi, lo, lambda s: (s, GLA_SUB))
            intra.append(_dot(a, v))
        o_a = o_a + jnp.concatenate(intra, axis=0)
        b_last = b[chunk - 1:chunk, :]
        decay_col = _rows_to_cols([jnp.exp(b_last)])[:, 0:1]
        sg_ref[0, 0, g] = decay_col * s_gla + _dot_tn(k * jnp.exp(b_last - b), v)
        o_a = _rms_heads(o_a, gnw_ref[...]) * _silu(h_ref[:, c["za"]])

        qb = _rope(h_ref[:, c["qb"]], cos, sin)
        kb = _rope(h_ref[:, c["kb"]], cos, sin) * (dk_b ** -0.5)
        vb = h_ref[:, c["vb"]].astype(_BF16)
        lg = lg_ref[g]
        q_dec = qb * jnp.exp(lg * tpos)
        k_dec = kb * jnp.exp(lg * (chunk - tpos))
        o_b = []
        for hh in range(2):
            ksl = slice(hh * dk_b, (hh + 1) * dk_b)
            vsl = slice(hh * dv_b, (hh + 1) * dv_b)
            nsl = slice(g * dv_a + hh * dv_b, g * dv_a + (hh + 1) * dv_b)
            lg1 = lg[:, hh * dk_b:hh * dk_b + 1]
            decay = jnp.exp(jnp.where(tdiff >= 0, lg1 * tdiff.astype(_F32), -jnp.inf))
            s_ret = sr_ref[0, 0, 2 * g + hh]
            scores = _dot_nt(qb[:, ksl], kb[:, ksl]) * decay
            o_h = _dot(q_dec[:, ksl], s_ret) + _dot(scores, vb[:, vsl])
            sr_ref[0, 0, 2 * g + hh] = (jnp.exp(lg1 * chunk) * s_ret
                                        + _dot_tn(k_dec[:, ksl], vb[:, vsl]))
            o_b.append(_group_norm(o_h, rnw_ref[:, nsl], rnb_ref[:, nsl]))
        o_b = jnp.concatenate(o_b, axis=1) * _silu(h_ref[:, c["zb"]])

        merged = (jax.nn.sigmoid(h_ref[:, c["ga"]]) * o_a
                  + jax.nn.sigmoid(h_ref[:, c["gb"]]) * o_b)
        m_ref[:, gv] = merged.astype(m_ref.dtype)


def _h_specs(rows, wq, wv, d_model, row_map):
    def spec(width, col_off):
        base = col_off // width
        return pl.BlockSpec((rows, width), lambda *ids: (row_map(*ids), base + ids[1]))
    d = d_model
    return [spec(wq, 0), spec(wq, d // 2), spec(wv, d), spec(wv, 2 * d),
            spec(wq, 3 * d), spec(wq, 3 * d + d // 2), spec(wv, 4 * d), spec(wv, 5 * d),
            spec(wv, 6 * d), spec(wv, 7 * d)]


def _param_specs(wq, wv):
    return [
        pl.BlockSpec((LANES, wq), lambda *ids: (0, ids[1])),
        pl.BlockSpec((1, wq), lambda *ids: (0, ids[1])),
        pl.BlockSpec((1, wv), lambda *ids: (0, 0)),
        pl.BlockSpec((1, wv), lambda *ids: (0, ids[1])),
        pl.BlockSpec((1, wv), lambda *ids: (0, ids[1])),
    ]


def _mixer_prompt(h, lr, params, rope, lg, batch, seq, d_model):
    wq, wv = d_model // 8, d_model // 4
    dk_a, dv_a = wq, wv
    dk_b, dv_b = wq // 2, wv // 2
    chunk = PROMPT_CHUNK
    nc = seq // chunk
    whole = lambda a: pl.BlockSpec(a.shape, lambda b, c: (0,) * a.ndim)
    in_specs = [pl.BlockSpec((chunk, h.shape[1]), lambda b, c: (b * nc + c, 0)),
                pl.BlockSpec((chunk, LANES), lambda b, c: (b * nc + c, 0))]
    in_specs += [whole(p) for p in params]
    in_specs += [pl.BlockSpec((chunk, wq), lambda b, c: (c, 0)),
                 pl.BlockSpec((chunk, wq), lambda b, c: (c, 0)),
                 whole(lg)]
    out_specs = [
        pl.BlockSpec((chunk, d_model), lambda b, c: (b * nc + c, 0)),
        pl.BlockSpec((1, 1, GLA_HEADS, dk_a, dv_a), lambda b, c: (0, b, 0, 0, 0)),
        pl.BlockSpec((1, 1, RET_HEADS, dk_b, dv_b), lambda b, c: (0, b, 0, 0, 0)),
    ]
    out_shape = [
        jax.ShapeDtypeStruct((batch * seq, d_model), _BF16),
        jax.ShapeDtypeStruct((1, batch, GLA_HEADS, dk_a, dv_a), _F32),
        jax.ShapeDtypeStruct((1, batch, RET_HEADS, dk_b, dv_b), _F32),
    ]
    return pl.pallas_call(
        _mixer_prompt_kernel,
        grid=(batch, nc),
        in_specs=in_specs,
        out_specs=out_specs,
        out_shape=out_shape,
        compiler_params=pltpu.CompilerParams(
            dimension_semantics=("arbitrary", "arbitrary"),
            vmem_limit_bytes=VMEM_LIMIT_BYTES),
        name="mixer_prompt",
    )(h, lr, *params, *rope, lg)


def _mixer_sample_kernel(qa_ref, ka_ref, va_ref, za_ref, qb_ref, kb_ref, vb_ref, zb_ref,
                         ga_ref, gb_ref, lr_ref, wlr_ref, blr_ref, gnw_ref, rnw_ref, rnb_ref,
                         cos_ref, sin_ref, lg_ref, sg0_ref, sr0_ref, m_ref, sg_ref, sr_ref,
                         *, steps):
    rows_n, dk_a = qa_ref.shape
    nseq = rows_n // steps
    dk_b = qb_ref.shape[1] // 2
    dv_b = vb_ref.shape[1] // 2
    rows = _iota((rows_n, 1), 0)
    pad = LANES - rows_n

    def pad_rows(x):
        return jnp.concatenate([x, jnp.zeros((pad, x.shape[1]), x.dtype)], axis=0)

    def seq_rows(x, n):
        return jnp.where((rows >= n * steps) & (rows < (n + 1) * steps), x, 0.0)

    la = _gate_log_decay(lr_ref, wlr_ref, blr_ref)
    b = _seg_cumsum(la, steps)
    q = qa_ref[...] * (dk_a ** -0.5)
    k = ka_ref[...]
    v = pad_rows(va_ref[...]).astype(_BF16)
    b_last = [b[(n + 1) * steps - 1:(n + 1) * steps, :] for n in range(nseq)]
    b_end = jnp.zeros_like(b)
    for n in range(nseq):
        b_end = b_end + seq_rows(jnp.broadcast_to(b_last[n], b.shape), n)
    a = jnp.zeros((rows_n, LANES), _F32)
    a = _gla_pair_scores(a, q, k, b, 0, lambda s: (s, (s // steps + 1) * steps))
    o_a = _dot(a, v)
    q_dec = q * jnp.exp(b)
    k_dec = pad_rows(k * jnp.exp(b_end - b))
    decay_cols = _rows_to_cols([jnp.exp(r) for r in b_last])
    for n in range(nseq):
        s0 = sg0_ref[0, n, 0]
        o_a = o_a + _dot(seq_rows(q_dec, n), s0)
        k_n = jnp.where((_iota(k_dec.shape, 0) >= n * steps) & (_iota(k_dec.shape, 0) < (n + 1) * steps),
                        k_dec, 0.0)
        sg_ref[0, n, 0] = decay_cols[:, n:n + 1] * s0 + _dot_tn(k_n, v)
    o_a = _rms_heads(o_a, gnw_ref[...]) * _silu(za_ref[...])

    cos, sin = cos_ref[...], sin_ref[...]
    qb = _rope(qb_ref[...], cos, sin)
    kb = _rope(kb_ref[...], cos, sin) * (dk_b ** -0.5)
    vb = pad_rows(vb_ref[...]).astype(_BF16)
    lg = lg_ref[0]
    tpos = ((_iota(qb.shape, 0) & (steps - 1)) + 1).astype(_F32)
    q_dec = qb * jnp.exp(lg * tpos)
    k_dec = pad_rows(kb * jnp.exp(lg * (steps - tpos)))
    kb_p = pad_rows(kb)
    r_i = _iota((rows_n, LANES), 0)
    c_i = _iota((rows_n, LANES), 1)
    same_seq = (r_i >> (steps.bit_length() - 1)) == (c_i >> (steps.bit_length() - 1))
    tdiff = r_i - c_i
    o_b = []
    for hh in range(2):
        ksl = slice(hh * dk_b, (hh + 1) * dk_b)
        vsl = slice(hh * dv_b, (hh + 1) * dv_b)
        lg1 = lg[:, hh * dk_b:hh * dk_b + 1]
        decay = jnp.exp(jnp.where((tdiff >= 0) & same_seq, lg1 * tdiff.astype(_F32), -jnp.inf))
        scores = _dot_nt(qb[:, ksl], kb_p[:, ksl]) * decay
        o_h = _dot(scores, vb[:, vsl])
        for n in range(nseq):
            s0 = sr0_ref[0, n, hh]
            o_h = o_h + _dot(seq_rows(q_dec[:, ksl], n), s0)
            kd = k_dec[:, ksl]
            k_n = jnp.where((_iota(kd.shape, 0) >= n * steps) & (_iota(kd.shape, 0) < (n + 1) * steps),
                            kd, 0.0)
            sr_ref[0, n, hh] = jnp.exp(lg1 * steps) * s0 + _dot_tn(k_n, vb[:, vsl])
        o_b.append(_group_norm(o_h, rnw_ref[:, vsl], rnb_ref[:, vsl]))
    o_b = jnp.concatenate(o_b, axis=1) * _silu(zb_ref[...])

    merged = jax.nn.sigmoid(ga_ref[...]) * o_a + jax.nn.sigmoid(gb_ref[...]) * o_b
    m_ref[...] = merged.astype(m_ref.dtype)


def _mixer_sample(h, lr, params, rope, lg, state_gla, state_ret, layer, nseq_total, steps, d_model):
    wq, wv = d_model // 8, d_model // 4
    dk_a, dv_a = wq, wv
    dk_b, dv_b = wq // 2, wv // 2
    nseq = SAMPLE_SEQS
    rows_n = nseq * steps
    assert steps & (steps - 1) == 0 and nseq_total % nseq == 0
    row_map = lambda p, g: p
    in_specs = _h_specs(rows_n, wq, wv, d_model, row_map)
    in_specs += [pl.BlockSpec((rows_n, LANES), lambda p, g: (p, 0))]
    in_specs += _param_specs(wq, wv)
    in_specs += [pl.BlockSpec((rows_n, wq), lambda p, g: (0, 0)),
                 pl.BlockSpec((rows_n, wq), lambda p, g: (0, 0)),
                 pl.BlockSpec((1, 1, wq), lambda p, g: (g, 0, 0)),
                 pl.BlockSpec((1, nseq, 1, dk_a, dv_a), lambda p, g: (layer, p, g, 0, 0)),
                 pl.BlockSpec((1, nseq, 2, dk_b, dv_b), lambda p, g: (layer, p, g, 0, 0))]
    out_specs = [
        pl.BlockSpec((rows_n, wv), lambda p, g: (p, g)),
        pl.BlockSpec((1, nseq, 1, dk_a, dv_a), lambda p, g: (0, p, g, 0, 0)),
        pl.BlockSpec((1, nseq, 2, dk_b, dv_b), lambda p, g: (0, p, g, 0, 0)),
    ]
    out_shape = [
        jax.ShapeDtypeStruct((nseq_total * steps, d_model), _BF16),
        jax.ShapeDtypeStruct((1, nseq_total, GLA_HEADS, dk_a, dv_a), _F32),
        jax.ShapeDtypeStruct((1, nseq_total, RET_HEADS, dk_b, dv_b), _F32),
    ]
    return pl.pallas_call(
        functools.partial(_mixer_sample_kernel, steps=steps),
        grid=(nseq_total // nseq, HEAD_GROUPS),
        in_specs=in_specs,
        out_specs=out_specs,
        out_shape=out_shape,
        compiler_params=pltpu.CompilerParams(
            dimension_semantics=("arbitrary", "arbitrary"),
            vmem_limit_bytes=VMEM_LIMIT_BYTES),
        name="mixer_sample",
    )(*([h] * 10), lr, *params, *rope, lg, state_gla, state_ret)


def _proj_out_kernel(m_ref, x_ref, w_ref, lnw_ref, lnb_ref, y_ref, *, alpha):
    y = jnp.dot(m_ref[...], w_ref[...], preferred_element_type=_F32)
    r = alpha * x_ref[...] + y
    mu = jnp.mean(r, axis=-1, keepdims=True)
    d = r - mu
    var = jnp.mean(d * d, axis=-1, keepdims=True)
    y_ref[...] = d * lax.rsqrt(var + LN_EPS) * lnw_ref[...] + lnb_ref[...]


def _proj_out(merged, x2d, w_out, ln_w, ln_b, alpha, tm):
    m, d = x2d.shape
    e = merged.shape[1]
    return pl.pallas_call(
        functools.partial(_proj_out_kernel, alpha=alpha),
        grid=(m // tm,),
        in_specs=[
            pl.BlockSpec((tm, e), lambda i: (i, 0)),
            pl.BlockSpec((tm, d), lambda i: (i, 0)),
            pl.BlockSpec((e, d), lambda i: (0, 0)),
            pl.BlockSpec((1, d), lambda i: (0, 0)),
            pl.BlockSpec((1, d), lambda i: (0, 0)),
        ],
        out_specs=pl.BlockSpec((tm, d), lambda i: (i, 0)),
        out_shape=jax.ShapeDtypeStruct((m, d), _F32),
        compiler_params=pltpu.CompilerParams(
            dimension_semantics=("arbitrary",),
            vmem_limit_bytes=VMEM_LIMIT_BYTES),
        name="proj_out",
    )(merged, x2d, w_out, ln_w, ln_b)


def _rope_tables(pos, dk):
    inv = 1.0 / (ROPE_BASE ** jnp.linspace(0.0, 1.0, dk // 2, dtype=_F32))
    ang = pos.astype(_F32)[:, None] * inv[None, :]
    cos = jnp.repeat(jnp.cos(ang), 2, axis=1)
    sin = jnp.stack([-jnp.sin(ang), jnp.sin(ang)], axis=-1).reshape(ang.shape[0], dk)
    return jnp.tile(cos, (1, 2)), jnp.tile(sin, (1, 2))


def _pick_tile(n, pref):
    t = min(n, pref)
    while n % t:
        t //= 2
    return t


def kernel(x_prompt, x_sample, state_gla, state_ret, w_in, w_lr, b_lr, gla_norm_w,
           ret_norm_w, ret_norm_b, w_out, ln_w, ln_b):
    depth, d_model, _ = w_in.shape
    batch, seq, _ = x_prompt.shape
    dec_batch, dec_seq, _ = x_sample.shape
    rank = w_lr.shape[1]
    dk_b = d_model // 2 // RET_HEADS
    assert state_gla.shape[2] == GLA_HEADS and state_ret.shape[2] == RET_HEADS
    assert rank <= LANES and seq % PROMPT_CHUNK == 0
    alpha = (2.0 * depth) ** 0.25

    lg_heads = jnp.log(1.0 - 2.0 ** (-5.0 - jnp.arange(RET_HEADS, dtype=_F32)))
    lg = jnp.repeat(lg_heads, dk_b).reshape(HEAD_GROUPS, 1, 2 * dk_b)
    rope_p = _rope_tables(jnp.arange(seq, dtype=jnp.int32), dk_b)
    pos_s = PAST_LEN + jnp.arange(dec_seq, dtype=jnp.int32)
    rope_s = _rope_tables(jnp.tile(pos_s, SAMPLE_SEQS), dk_b)

    hp = x_prompt.reshape(batch * seq, d_model)
    hs = x_sample.reshape(dec_batch * dec_seq, d_model)
    gla_p, ret_p, gla_s, ret_s = [], [], [], []
    for l in range(depth):
        lr_lo = 3 * d_model
        w_main = jnp.concatenate([w_in[l][:, :lr_lo], w_in[l][:, lr_lo + rank:]], axis=1).astype(_BF16)
        w_lra = jnp.pad(w_in[l][:, lr_lo:lr_lo + rank], ((0, 0), (0, LANES - rank))).astype(_BF16)
        w_o = w_out[l].astype(_BF16)
        params = (jnp.pad(w_lr[l], ((0, LANES - rank), (0, 0))), b_lr[l][None, :],
                  gla_norm_w[l][None, :], ret_norm_w[l][None, :], ret_norm_b[l][None, :])
        lnw, lnb = ln_w[l][None, :], ln_b[l][None, :]

        h, lr = _proj_in(hp, w_main, w_lra, _pick_tile(hp.shape[0], 1024), 1024)
        merged, sg, sr = _mixer_prompt(h, lr, params, rope_p, lg, batch, seq, d_model)
        hp = _proj_out(merged, hp, w_o, lnw, lnb, alpha, _pick_tile(hp.shape[0], 512))
        gla_p.append(sg)
        ret_p.append(sr)

        h, lr = _proj_in(hs, w_main, w_lra, _pick_tile(hs.shape[0], 1024), 1024)
        merged, sg, sr = _mixer_sample(h, lr, params, rope_s, lg, state_gla, state_ret, l,
                                       dec_batch, dec_seq, d_model)
        hs = _proj_out(merged, hs, w_o, lnw, lnb, alpha, _pick_tile(hs.shape[0], 256))
        gla_s.append(sg)
        ret_s.append(sr)

    cat = lambda parts: parts[0] if len(parts) == 1 else jnp.concatenate(parts, axis=0)
    return (hp.reshape(batch, seq, d_model), hs.reshape(dec_batch, dec_seq, d_model),
            cat(gla_p), cat(ret_p), cat(gla_s), cat(ret_s))
```

```python
import functools

import jax
import jax.numpy as jnp
from jax import lax
from jax.experimental import pallas as pl
from jax.experimental.pallas import tpu as pltpu

GLA_HEADS = 4
RET_HEADS = 8
HEAD_GROUPS = 4
GLA_TAU = 16.0
ROPE_BASE = 10000.0
LN_EPS = 1e-5
HEAD_NORM_EPS = 1e-6
PAST_LEN = 16384

LANES = 128
SUBLANES = 8
VMEM_LIMIT_BYTES = 56 * 1024 * 1024

PROMPT_CHUNK = 64
GLA_SUB = 16
SAMPLE_SEQS = 4

_F32 = jnp.float32
_BF16 = jnp.bfloat16
_NT = (((1,), (1,)), ((), ()))
_TN = (((0,), (0,)), ((), ()))


def _dot(a, b):
    return jnp.dot(a.astype(_BF16), b.astype(_BF16), preferred_element_type=_F32)


def _dot_nt(a, b):
    return lax.dot_general(a.astype(_BF16), b.astype(_BF16), _NT, preferred_element_type=_F32)


def _dot_tn(a, b):
    return lax.dot_general(a.astype(_BF16), b.astype(_BF16), _TN, preferred_element_type=_F32)


def _log_sigmoid(x):
    return jnp.minimum(x, 0.0) - jnp.log(1.0 + jnp.exp(-jnp.abs(x)))


def _silu(x):
    return x * jax.nn.sigmoid(x)


def _iota(shape, axis):
    return lax.broadcasted_iota(jnp.int32, shape, axis)


def _seg_cumsum(x, seg):
    pos = _iota(x.shape, 0) & (seg - 1)
    s = 1
    while s < seg:
        x = x + jnp.where(pos >= s, pltpu.roll(x, s, axis=0), 0.0)
        s *= 2
    return x


def _rows_to_cols(rows):
    n = rows[0].shape[1]
    rid = _iota((LANES, n), 0)
    stack = jnp.zeros((LANES, n), _F32)
    for i, r in enumerate(rows):
        stack = jnp.where(rid == i, r, stack)
    return stack.T


def _rope(x, cos, sin_signed):
    w = x.shape[1]
    even = (_iota(x.shape, 1) & 1) == 0
    swapped = jnp.where(even, pltpu.roll(x, w - 1, axis=1), pltpu.roll(x, 1, axis=1))
    return x * cos + swapped * sin_signed


def _gla_pair_scores(a, q, k, b, lane0, row_lo_hi):
    n = q.shape[0]
    lanes = _iota((SUBLANES, a.shape[1]), 1)
    tiles = [a[r:r + SUBLANES] for r in range(0, n, SUBLANES)]
    for s in range(n):
        lo, hi = row_lo_hi(s)
        for j, r0 in enumerate(range(0, n, SUBLANES)):
            if r0 + SUBLANES <= lo or r0 >= hi:
                continue
            rows = _iota((SUBLANES, 1), 0) + r0
            e = jnp.exp(b[r0:r0 + SUBLANES] - b[s:s + 1, :])
            col = jnp.sum(q[r0:r0 + SUBLANES] * k[s:s + 1, :] * e, axis=-1, keepdims=True)
            col = jnp.where((rows >= lo) & (rows < hi), col, 0.0)
            tiles[j] = jnp.where(lanes == lane0 + s, col, tiles[j])
    return jnp.concatenate(tiles, axis=0)


def _rms_heads(o, w):
    return o * lax.rsqrt(jnp.mean(o * o, axis=-1, keepdims=True) + HEAD_NORM_EPS) * w


def _group_norm(o, w, b):
    mu = jnp.mean(o, axis=-1, keepdims=True)
    d = o - mu
    var = jnp.mean(d * d, axis=-1, keepdims=True)
    return d * lax.rsqrt(var + HEAD_NORM_EPS) * w + b


def _prep_x_kernel(x_ref, wlra_ref, wlr_ref, blr_ref, xb_ref, la_ref):
    xb = x_ref[...].astype(_BF16)
    xb_ref[...] = xb
    lr = jnp.dot(xb, wlra_ref[...], preferred_element_type=_F32)
    logit = jnp.dot(lr, wlr_ref[...], precision=lax.Precision.HIGHEST,
                    preferred_element_type=_F32) + blr_ref[...]
    la_ref[...] = _log_sigmoid(logit) * (1.0 / GLA_TAU)


def _prep_x(x2d, w_lra, w_lr, b_lr, tm):
    m, kdim = x2d.shape
    qk = w_lr.shape[1]
    whole = lambda a: pl.BlockSpec(a.shape, lambda i: (0,) * a.ndim)
    return pl.pallas_call(
        _prep_x_kernel,
        grid=(m // tm,),
        in_specs=[pl.BlockSpec((tm, kdim), lambda i: (i, 0)), whole(w_lra), whole(w_lr), whole(b_lr)],
        out_specs=[pl.BlockSpec((tm, kdim), lambda i: (i, 0)),
                   pl.BlockSpec((tm, qk), lambda i: (i, 0))],
        out_shape=[jax.ShapeDtypeStruct((m, kdim), _BF16),
                   jax.ShapeDtypeStruct((m, qk), _F32)],
        compiler_params=pltpu.CompilerParams(
            dimension_semantics=("arbitrary",), vmem_limit_bytes=VMEM_LIMIT_BYTES),
        name="prep_x",
    )(x2d, w_lra, w_lr, b_lr)


def _proj_in_kernel(xp_ref, xs_ref, wa_ref, we_ref, hp_ref, hs_ref, wbf_ref,
                    *, n_plain, shift, silu_tiles, sigmoid_tiles):
    j = pl.program_id(0)
    i = pl.program_id(1)
    n_prompt = pl.num_programs(1) - 1
    kdim, tn = wbf_ref.shape
    w_rows = 256

    @pl.when(i == 0)
    def _():
        @pl.when(j < n_plain)
        def _():
            wbf_ref[...] = wa_ref[0].astype(_BF16)

        @pl.when(j >= n_plain)
        def _():
            def body(r, carry):
                rs = pl.ds(pl.multiple_of(r * w_rows, w_rows), w_rows)
                a = pltpu.roll(wa_ref[0, rs, :], tn - shift, axis=1)
                e = pltpu.roll(we_ref[0, rs, :], LANES - shift, axis=1)
                tail = jnp.where(_iota(e.shape, 1) >= LANES - shift, e, a[:, tn - LANES:])
                wbf_ref[rs, :] = jnp.concatenate([a[:, :tn - LANES], tail], axis=1).astype(_BF16)
                return carry
            lax.fori_loop(0, kdim // w_rows, body, 0)

    is_silu = functools.reduce(jnp.logical_or, [j == t for t in silu_tiles])
    is_sigmoid = functools.reduce(jnp.logical_or, [j == t for t in sigmoid_tiles])

    def tile(x_ref, h_ref):
        acc = jnp.dot(x_ref[...], wbf_ref[...], preferred_element_type=_F32)
        s = jax.nn.sigmoid(acc)
        h_ref[...] = jnp.where(is_sigmoid, s, jnp.where(is_silu, acc * s, acc))

    @pl.when(i < n_prompt)
    def _():
        tile(xp_ref, hp_ref)

    @pl.when(i == n_prompt)
    def _():
        tile(xs_ref, hs_ref)


def _proj_in(xp, xs, w_in, layer, rank):
    mp, kdim = xp.shape
    ms = xs.shape[0]
    d_model = kdim
    tn = d_model // 2
    nj = 16
    n_plain = 6
    assert rank < LANES and tn % LANES == 0
    tm = _pick_tile(mp, 1024)
    ni = mp // tm
    kern = functools.partial(_proj_in_kernel, n_plain=n_plain, shift=rank,
                             silu_tiles=(4, 5, 10, 11), sigmoid_tiles=(12, 13, 14, 15))
    return pl.pallas_call(
        kern,
        grid=(nj, ni + 1),
        in_specs=[
            pl.BlockSpec((tm, kdim), lambda j, i: (jnp.minimum(i, ni - 1), 0)),
            pl.BlockSpec((ms, kdim), lambda j, i: (0, 0)),
            pl.BlockSpec((1, kdim, tn), lambda j, i: (layer, 0, j)),
            pl.BlockSpec((1, kdim, LANES), lambda j, i: (layer, 0, (j + 1) * (tn // LANES))),
        ],
        out_specs=[
            pl.BlockSpec((tm, tn), lambda j, i: (jnp.minimum(i, ni - 1), j)),
            pl.BlockSpec((ms, tn), lambda j, i: (0, j)),
        ],
        out_shape=[
            jax.ShapeDtypeStruct((mp, nj * tn), _F32),
            jax.ShapeDtypeStruct((ms, nj * tn), _F32),
        ],
        scratch_shapes=[pltpu.VMEM((kdim, tn), _BF16)],
        compiler_params=pltpu.CompilerParams(
            dimension_semantics=("arbitrary", "arbitrary"),
            vmem_limit_bytes=VMEM_LIMIT_BYTES),
        name="proj_in",
    )(xp, xs, w_in, w_in)


def _h_cols(d_model, g):
    wq, wv = d_model // 8, d_model // 4
    d = d_model
    offs = dict(qa=(0, wq), ka=(d // 2, wq), va=(d, wv), za=(2 * d, wv),
                qb=(3 * d, wq), kb=(3 * d + d // 2, wq), vb=(4 * d, wv), zb=(5 * d, wv),
                ga=(6 * d, wv), gb=(7 * d, wv))
    return {n: slice(o + g * w, o + (g + 1) * w) for n, (o, w) in offs.items()}


def _mixer_prompt_kernel(h_ref, la_ref, gnw_ref, rnw_ref, rnb_ref,
                         cos_ref, sin_ref, lg_ref, m_ref, sg_ref, sr_ref):
    chunk = h_ref.shape[0]
    d_model = m_ref.shape[1]
    dk_a, dv_a = sg_ref.shape[3:]
    dk_b, dv_b = sr_ref.shape[3:]

    @pl.when(pl.program_id(1) == 0)
    def _():
        sg_ref[...] = jnp.zeros(sg_ref.shape, _F32)
        sr_ref[...] = jnp.zeros(sr_ref.shape, _F32)

    rows = _iota((chunk, 1), 0)
    cos, sin = cos_ref[...], sin_ref[...]
    tpos = (_iota(cos.shape, 0) + 1).astype(_F32)
    tdiff = _iota((chunk, chunk), 0) - _iota((chunk, chunk), 1)
    for g in range(HEAD_GROUPS):
        c = _h_cols(d_model, g)
        gq = slice(g * dk_a, (g + 1) * dk_a)
        gv = slice(g * dv_a, (g + 1) * dv_a)

        b = _seg_cumsum(la_ref[:, gq], chunk)
        q = h_ref[:, c["qa"]] * (dk_a ** -0.5)
        k = h_ref[:, c["ka"]]
        v = h_ref[:, c["va"]].astype(_BF16)
        s_gla = sg_ref[0, 0, g]
        o_a = _dot(q * jnp.exp(b), s_gla)
        intra = []
        for i in range(chunk // GLA_SUB):
            lo = i * GLA_SUB
            qi, ki, bi = q[lo:lo + GLA_SUB], k[lo:lo + GLA_SUB], b[lo:lo + GLA_SUB]
            if i == 0:
                a = jnp.zeros((GLA_SUB, chunk), _F32)
            else:
                r = b[lo - 1:lo, :]
                k_pre = k * jnp.exp(jnp.where(rows < lo, r - b, -jnp.inf))
                a = _dot_nt(qi * jnp.exp(bi - r), k_pre)
            a = _gla_pair_scores(a, qi, ki, bi, lo, lambda s: (s, GLA_SUB))
            intra.append(_dot(a, v))
        o_a = o_a + jnp.concatenate(intra, axis=0)
        b_last = b[chunk - 1:chunk, :]
        decay_col = _rows_to_cols([jnp.exp(b_last)])[:, 0:1]
        sg_ref[0, 0, g] = decay_col * s_gla + _dot_tn(k * jnp.exp(b_last - b), v)
        o_a = _rms_heads(o_a, gnw_ref[...]) * h_ref[:, c["za"]]

        qb = _rope(h_ref[:, c["qb"]], cos, sin)
        kb = _rope(h_ref[:, c["kb"]], cos, sin) * (dk_b ** -0.5)
        vb = h_ref[:, c["vb"]].astype(_BF16)
        lg = lg_ref[g]
        q_dec = qb * jnp.exp(lg * tpos)
        k_dec = kb * jnp.exp(lg * (chunk - tpos))
        o_b = []
        for hh in range(2):
            ksl = slice(hh * dk_b, (hh + 1) * dk_b)
            vsl = slice(hh * dv_b, (hh + 1) * dv_b)
            nsl = slice(g * dv_a + hh * dv_b, g * dv_a + (hh + 1) * dv_b)
            lg1 = lg[:, hh * dk_b:hh * dk_b + 1]
            decay = jnp.exp(jnp.where(tdiff >= 0, lg1 * tdiff.astype(_F32), -jnp.inf))
            s_ret = sr_ref[0, 0, 2 * g + hh]
            scores = _dot_nt(qb[:, ksl], kb[:, ksl]) * decay
            o_h = _dot(q_dec[:, ksl], s_ret) + _dot(scores, vb[:, vsl])
            sr_ref[0, 0, 2 * g + hh] = (jnp.exp(lg1 * chunk) * s_ret
                                        + _dot_tn(k_dec[:, ksl], vb[:, vsl]))
            o_b.append(_group_norm(o_h, rnw_ref[:, nsl], rnb_ref[:, nsl]))
        o_b = jnp.concatenate(o_b, axis=1) * h_ref[:, c["zb"]]

        merged = h_ref[:, c["ga"]] * o_a + h_ref[:, c["gb"]] * o_b
        m_ref[:, gv] = merged.astype(m_ref.dtype)


def _h_specs(rows, wq, wv, d_model, row_map):
    def spec(width, col_off):
        base = col_off // width
        return pl.BlockSpec((rows, width), lambda *ids: (row_map(*ids), base + ids[1]))
    d = d_model
    return [spec(wq, 0), spec(wq, d // 2), spec(wv, d), spec(wv, 2 * d),
            spec(wq, 3 * d), spec(wq, 3 * d + d // 2), spec(wv, 4 * d), spec(wv, 5 * d),
            spec(wv, 6 * d), spec(wv, 7 * d)]


def _param_specs(wq, wv):
    return [
        pl.BlockSpec((1, wv), lambda *ids: (0, 0)),
        pl.BlockSpec((1, wv), lambda *ids: (0, ids[1])),
        pl.BlockSpec((1, wv), lambda *ids: (0, ids[1])),
    ]


def _mixer_prompt(h, la, params, rope, lg, batch, seq, d_model):
    wq, wv = d_model // 8, d_model // 4
    dk_a, dv_a = wq, wv
    dk_b, dv_b = wq // 2, wv // 2
    chunk = PROMPT_CHUNK
    nc = seq // chunk
    whole = lambda a: pl.BlockSpec(a.shape, lambda b, c: (0,) * a.ndim)
    in_specs = [pl.BlockSpec((chunk, h.shape[1]), lambda b, c: (b * nc + c, 0)),
                pl.BlockSpec((chunk, la.shape[1]), lambda b, c: (b * nc + c, 0))]
    in_specs += [whole(p) for p in params]
    in_specs += [pl.BlockSpec((chunk, wq), lambda b, c: (c, 0)),
                 pl.BlockSpec((chunk, wq), lambda b, c: (c, 0)),
                 whole(lg)]
    out_specs = [
        pl.BlockSpec((chunk, d_model), lambda b, c: (b * nc + c, 0)),
        pl.BlockSpec((1, 1, GLA_HEADS, dk_a, dv_a), lambda b, c: (0, b, 0, 0, 0)),
        pl.BlockSpec((1, 1, RET_HEADS, dk_b, dv_b), lambda b, c: (0, b, 0, 0, 0)),
    ]
    out_shape = [
        jax.ShapeDtypeStruct((batch * seq, d_model), _BF16),
        jax.ShapeDtypeStruct((1, batch, GLA_HEADS, dk_a, dv_a), _F32),
        jax.ShapeDtypeStruct((1, batch, RET_HEADS, dk_b, dv_b), _F32),
    ]
    return pl.pallas_call(
        _mixer_prompt_kernel,
        grid=(batch, nc),
        in_specs=in_specs,
        out_specs=out_specs,
        out_shape=out_shape,
        compiler_params=pltpu.CompilerParams(
            dimension_semantics=("arbitrary", "arbitrary"),
            vmem_limit_bytes=VMEM_LIMIT_BYTES),
        name="mixer_prompt",
    )(h, la, *params, *rope, lg)


def _mixer_sample_kernel(qa_ref, ka_ref, va_ref, za_ref, qb_ref, kb_ref, vb_ref, zb_ref,
                         ga_ref, gb_ref, la_ref, gnw_ref, rnw_ref, rnb_ref,
                         cos_ref, sin_ref, lg_ref, sg0_ref, sr0_ref, m_ref, sg_ref, sr_ref,
                         *, steps):
    rows_n, dk_a = qa_ref.shape
    nseq = rows_n // steps
    dk_b = qb_ref.shape[1] // 2
    dv_b = vb_ref.shape[1] // 2
    rows = _iota((rows_n, 1), 0)
    pad = LANES - rows_n

    def pad_rows(x):
        return jnp.concatenate([x, jnp.zeros((pad, x.shape[1]), x.dtype)], axis=0)

    def seq_rows(x, n):
        return jnp.where((rows >= n * steps) & (rows < (n + 1) * steps), x, 0.0)

    b = _seg_cumsum(la_ref[...], steps)
    q = qa_ref[...] * (dk_a ** -0.5)
    k = ka_ref[...]
    v = pad_rows(va_ref[...]).astype(_BF16)
    b_last = [b[(n + 1) * steps - 1:(n + 1) * steps, :] for n in range(nseq)]
    b_end = jnp.zeros_like(b)
    for n in range(nseq):
        b_end = b_end + seq_rows(jnp.broadcast_to(b_last[n], b.shape), n)
    a = jnp.zeros((rows_n, LANES), _F32)
    a = _gla_pair_scores(a, q, k, b, 0, lambda s: (s, (s // steps + 1) * steps))
    o_a = _dot(a, v)
    q_dec = q * jnp.exp(b)
    k_dec = pad_rows(k * jnp.exp(b_end - b))
    decay_cols = _rows_to_cols([jnp.exp(r) for r in b_last])
    for n in range(nseq):
        s0 = sg0_ref[0, n, 0]
        o_a = o_a + _dot(seq_rows(q_dec, n), s0)
        k_n = jnp.where((_iota(k_dec.shape, 0) >= n * steps) & (_iota(k_dec.shape, 0) < (n + 1) * steps),
                        k_dec, 0.0)
        sg_ref[0, n, 0] = decay_cols[:, n:n + 1] * s0 + _dot_tn(k_n, v)
    o_a = _rms_heads(o_a, gnw_ref[...]) * za_ref[...]

    cos, sin = cos_ref[...], sin_ref[...]
    qb = _rope(qb_ref[...], cos, sin)
    kb = _rope(kb_ref[...], cos, sin) * (dk_b ** -0.5)
    vb = pad_rows(vb_ref[...]).astype(_BF16)
    lg = lg_ref[0]
    tpos = ((_iota(qb.shape, 0) & (steps - 1)) + 1).astype(_F32)
    q_dec = qb * jnp.exp(lg * tpos)
    k_dec = pad_rows(kb * jnp.exp(lg * (steps - tpos)))
    kb_p = pad_rows(kb)
    r_i = _iota((rows_n, LANES), 0)
    c_i = _iota((rows_n, LANES), 1)
    same_seq = (r_i >> (steps.bit_length() - 1)) == (c_i >> (steps.bit_length() - 1))
    tdiff = r_i - c_i
    o_b = []
    for hh in range(2):
        ksl = slice(hh * dk_b, (hh + 1) * dk_b)
        vsl = slice(hh * dv_b, (hh + 1) * dv_b)
        lg1 = lg[:, hh * dk_b:hh * dk_b + 1]
        decay = jnp.exp(jnp.where((tdiff >= 0) & same_seq, lg1 * tdiff.astype(_F32), -jnp.inf))
        scores = _dot_nt(qb[:, ksl], kb_p[:, ksl]) * decay
        o_h = _dot(scores, vb[:, vsl])
        for n in range(nseq):
            s0 = sr0_ref[0, n, hh]
            o_h = o_h + _dot(seq_rows(q_dec[:, ksl], n), s0)
            kd = k_dec[:, ksl]
            k_n = jnp.where((_iota(kd.shape, 0) >= n * steps) & (_iota(kd.shape, 0) < (n + 1) * steps),
                            kd, 0.0)
            sr_ref[0, n, hh] = jnp.exp(lg1 * steps) * s0 + _dot_tn(k_n, vb[:, vsl])
        o_b.append(_group_norm(o_h, rnw_ref[:, vsl], rnb_ref[:, vsl]))
    o_b = jnp.concatenate(o_b, axis=1) * zb_ref[...]

    merged = ga_ref[...] * o_a + gb_ref[...] * o_b
    m_ref[...] = merged.astype(m_ref.dtype)


def _mixer_sample(h, la, params, rope, lg, state_gla, state_ret, layer, nseq_total, steps, d_model):
    wq, wv = d_model // 8, d_model // 4
    dk_a, dv_a = wq, wv
    dk_b, dv_b = wq // 2, wv // 2
    nseq = SAMPLE_SEQS
    rows_n = nseq * steps
    assert steps & (steps - 1) == 0 and nseq_total % nseq == 0
    row_map = lambda p, g: p
    in_specs = _h_specs(rows_n, wq, wv, d_model, row_map)
    in_specs += [pl.BlockSpec((rows_n, wq), lambda p, g: (p, g))]
    in_specs += _param_specs(wq, wv)
    in_specs += [pl.BlockSpec((rows_n, wq), lambda p, g: (0, 0)),
                 pl.BlockSpec((rows_n, wq), lambda p, g: (0, 0)),
                 pl.BlockSpec((1, 1, wq), lambda p, g: (g, 0, 0)),
                 pl.BlockSpec((1, nseq, 1, dk_a, dv_a), lambda p, g: (layer, p, g, 0, 0)),
                 pl.BlockSpec((1, nseq, 2, dk_b, dv_b), lambda p, g: (layer, p, g, 0, 0))]
    out_specs = [
        pl.BlockSpec((rows_n, wv), lambda p, g: (p, g)),
        pl.BlockSpec((1, nseq, 1, dk_a, dv_a), lambda p, g: (0, p, g, 0, 0)),
        pl.BlockSpec((1, nseq, 2, dk_b, dv_b), lambda p, g: (0, p, g, 0, 0)),
    ]
    out_shape = [
        jax.ShapeDtypeStruct((nseq_total * steps, d_model), _BF16),
        jax.ShapeDtypeStruct((1, nseq_total, GLA_HEADS, dk_a, dv_a), _F32),
        jax.ShapeDtypeStruct((1, nseq_total, RET_HEADS, dk_b, dv_b), _F32),
    ]
    return pl.pallas_call(
        functools.partial(_mixer_sample_kernel, steps=steps),
        grid=(nseq_total // nseq, HEAD_GROUPS),
        in_specs=in_specs,
        out_specs=out_specs,
        out_shape=out_shape,
        compiler_params=pltpu.CompilerParams(
            dimension_semantics=("arbitrary", "arbitrary"),
            vmem_limit_bytes=VMEM_LIMIT_BYTES),
        name="mixer_sample",
    )(*([h] * 10), la, *params, *rope, lg, state_gla, state_ret)


def _proj_out_kernel(m_ref, x_ref, w_ref, lnw_ref, lnb_ref, y_ref, *, alpha):
    y = jnp.dot(m_ref[...], w_ref[...], preferred_element_type=_F32)
    r = alpha * x_ref[...] + y
    mu = jnp.mean(r, axis=-1, keepdims=True)
    d = r - mu
    var = jnp.mean(d * d, axis=-1, keepdims=True)
    y_ref[...] = d * lax.rsqrt(var + LN_EPS) * lnw_ref[...] + lnb_ref[...]


def _proj_out(merged, x2d, w_out, ln_w, ln_b, alpha, tm):
    m, d = x2d.shape
    e = merged.shape[1]
    return pl.pallas_call(
        functools.partial(_proj_out_kernel, alpha=alpha),
        grid=(m // tm,),
        in_specs=[
            pl.BlockSpec((tm, e), lambda i: (i, 0)),
            pl.BlockSpec((tm, d), lambda i: (i, 0)),
            pl.BlockSpec((e, d), lambda i: (0, 0)),
            pl.BlockSpec((1, d), lambda i: (0, 0)),
            pl.BlockSpec((1, d), lambda i: (0, 0)),
        ],
        out_specs=pl.BlockSpec((tm, d), lambda i: (i, 0)),
        out_shape=jax.ShapeDtypeStruct((m, d), _F32),
        compiler_params=pltpu.CompilerParams(
            dimension_semantics=("arbitrary",),
            vmem_limit_bytes=VMEM_LIMIT_BYTES),
        name="proj_out",
    )(merged, x2d, w_out, ln_w, ln_b)


def _rope_tables(pos, dk):
    inv = 1.0 / (ROPE_BASE ** jnp.linspace(0.0, 1.0, dk // 2, dtype=_F32))
    ang = pos.astype(_F32)[:, None] * inv[None, :]
    cos = jnp.repeat(jnp.cos(ang), 2, axis=1)
    sin = jnp.stack([-jnp.sin(ang), jnp.sin(ang)], axis=-1).reshape(ang.shape[0], dk)
    return jnp.tile(cos, (1, 2)), jnp.tile(sin, (1, 2))


def _pick_tile(n, pref):
    t = min(n, pref)
    while n % t:
        t //= 2
    return t


def kernel(x_prompt, x_sample, state_gla, state_ret, w_in, w_lr, b_lr, gla_norm_w,
           ret_norm_w, ret_norm_b, w_out, ln_w, ln_b):
    depth, d_model, _ = w_in.shape
    batch, seq, _ = x_prompt.shape
    dec_batch, dec_seq, _ = x_sample.shape
    rank = w_lr.shape[1]
    dk_b = d_model // 2 // RET_HEADS
    assert state_gla.shape[2] == GLA_HEADS and state_ret.shape[2] == RET_HEADS
    assert rank <= LANES and seq % PROMPT_CHUNK == 0
    alpha = (2.0 * depth) ** 0.25

    lg_heads = jnp.log(1.0 - 2.0 ** (-5.0 - jnp.arange(RET_HEADS, dtype=_F32)))
    lg = jnp.repeat(lg_heads, dk_b).reshape(HEAD_GROUPS, 1, 2 * dk_b)
    rope_p = _rope_tables(jnp.arange(seq, dtype=jnp.int32), dk_b)
    pos_s = PAST_LEN + jnp.arange(dec_seq, dtype=jnp.int32)
    rope_s = _rope_tables(jnp.tile(pos_s, SAMPLE_SEQS), dk_b)

    hp = x_prompt.reshape(batch * seq, d_model)
    hs = x_sample.reshape(dec_batch * dec_seq, d_model)
    gla_p, ret_p, gla_s, ret_s = [], [], [], []
    for l in range(depth):
        lr_lo = 3 * d_model
        w_lra = jnp.pad(lax.slice(w_in, (l, 0, lr_lo), (l + 1, d_model, lr_lo + rank))[0],
                        ((0, 0), (0, LANES - rank))).astype(_BF16)
        w_lr_p = jnp.pad(w_lr[l], ((0, LANES - rank), (0, 0)))
        w_o = w_out[l].astype(_BF16)
        params = (gla_norm_w[l][None, :], ret_norm_w[l][None, :], ret_norm_b[l][None, :])
        lnw, lnb = ln_w[l][None, :], ln_b[l][None, :]

        xp_b, la_p = _prep_x(hp, w_lra, w_lr_p, b_lr[l][None, :], _pick_tile(hp.shape[0], 512))
        xs_b, la_s = _prep_x(hs, w_lra, w_lr_p, b_lr[l][None, :], _pick_tile(hs.shape[0], 512))
        h_p, h_s = _proj_in(xp_b, xs_b, w_in, l, rank)

        merged, sg, sr = _mixer_prompt(h_p, la_p, params, rope_p, lg, batch, seq, d_model)
        hp = _proj_out(merged, hp, w_o, lnw, lnb, alpha, _pick_tile(hp.shape[0], 512))
        gla_p.append(sg)
        ret_p.append(sr)

        merged, sg, sr = _mixer_sample(h_s, la_s, params, rope_s, lg, state_gla, state_ret, l,
                                       dec_batch, dec_seq, d_model)
        hs = _proj_out(merged, hs, w_o, lnw, lnb, alpha, _pick_tile(hs.shape[0], 256))
        gla_s.append(sg)
        ret_s.append(sr)

    cat = lambda parts: parts[0] if len(parts) == 1 else jnp.concatenate(parts, axis=0)
    return (hp.reshape(batch, seq, d_model), hs.reshape(dec_batch, dec_seq, d_model),
            cat(gla_p), cat(ret_p), cat(gla_s), cat(ret_s))
```

```python
import functools

import jax
import jax.numpy as jnp
from jax import lax
from jax.experimental import pallas as pl
from jax.experimental.pallas import tpu as pltpu

GLA_HEADS = 4
RET_HEADS = 8
HEAD_GROUPS = 4
GLA_TAU = 16.0
ROPE_BASE = 10000.0
LN_EPS = 1e-5
HEAD_NORM_EPS = 1e-6
PAST_LEN = 16384

LANES = 128
SUBLANES = 8
VMEM_LIMIT_BYTES = 56 * 1024 * 1024

PROMPT_CHUNK = 64
GLA_SUB = 16
SAMPLE_SEQS = 4

_F32 = jnp.float32
_BF16 = jnp.bfloat16
_NT = (((1,), (1,)), ((), ()))
_TN = (((0,), (0,)), ((), ()))


def _dot(a, b):
    return jnp.dot(a.astype(_BF16), b.astype(_BF16), preferred_element_type=_F32)


def _dot_nt(a, b):
    return lax.dot_general(a.astype(_BF16), b.astype(_BF16), _NT, preferred_element_type=_F32)


def _dot_tn(a, b):
    return lax.dot_general(a.astype(_BF16), b.astype(_BF16), _TN, preferred_element_type=_F32)


def _log_sigmoid(x):
    return jnp.minimum(x, 0.0) - jnp.log(1.0 + jnp.exp(-jnp.abs(x)))


def _silu(x):
    return x * jax.nn.sigmoid(x)


def _iota(shape, axis):
    return lax.broadcasted_iota(jnp.int32, shape, axis)


def _seg_cumsum(x, seg):
    pos = _iota(x.shape, 0) & (seg - 1)
    s = 1
    while s < seg:
        x = x + jnp.where(pos >= s, pltpu.roll(x, s, axis=0), 0.0)
        s *= 2
    return x


def _rows_to_cols(rows):
    n = rows[0].shape[1]
    rid = _iota((LANES, n), 0)
    stack = jnp.zeros((LANES, n), _F32)
    for i, r in enumerate(rows):
        stack = jnp.where(rid == i, r, stack)
    return stack.T


def _rope(x, cos, sin_signed):
    w = x.shape[1]
    even = (_iota(x.shape, 1) & 1) == 0
    swapped = jnp.where(even, pltpu.roll(x, w - 1, axis=1), pltpu.roll(x, 1, axis=1))
    return x * cos + swapped * sin_signed


def _gla_pair_scores(a, q, k, b, lane0, row_lo_hi):
    n = q.shape[0]
    lanes = _iota((SUBLANES, a.shape[1]), 1)
    tiles = [a[r:r + SUBLANES] for r in range(0, n, SUBLANES)]
    for s in range(n):
        lo, hi = row_lo_hi(s)
        for j, r0 in enumerate(range(0, n, SUBLANES)):
            if r0 + SUBLANES <= lo or r0 >= hi:
                continue
            rows = _iota((SUBLANES, 1), 0) + r0
            e = jnp.exp(b[r0:r0 + SUBLANES] - b[s:s + 1, :])
            col = jnp.sum(q[r0:r0 + SUBLANES] * k[s:s + 1, :] * e, axis=-1, keepdims=True)
            col = jnp.where((rows >= lo) & (rows < hi), col, 0.0)
            tiles[j] = jnp.where(lanes == lane0 + s, col, tiles[j])
    return jnp.concatenate(tiles, axis=0)


def _rms_heads(o, w):
    return o * lax.rsqrt(jnp.mean(o * o, axis=-1, keepdims=True) + HEAD_NORM_EPS) * w


def _group_norm(o, w, b):
    mu = jnp.mean(o, axis=-1, keepdims=True)
    d = o - mu
    var = jnp.mean(d * d, axis=-1, keepdims=True)
    return d * lax.rsqrt(var + HEAD_NORM_EPS) * w + b


def _prep_x_kernel(x_ref, wcol_ref, wlr_ref, blr_ref, xb_ref, la_ref, *, rank):
    xb = x_ref[...].astype(_BF16)
    xb_ref[...] = xb
    wrow = wcol_ref[0]
    w_lra = jnp.concatenate([wrow, jnp.zeros((LANES - rank, wrow.shape[1]), _F32)], axis=0)
    lr = _dot_nt(xb, w_lra)
    logit = jnp.dot(lr, wlr_ref[...], precision=lax.Precision.HIGHEST,
                    preferred_element_type=_F32) + blr_ref[...]
    la_ref[...] = _log_sigmoid(logit) * (1.0 / GLA_TAU)


def _prep_x(x2d, w_t, layer, lr_col, rank, w_lr, b_lr, tm):
    m, kdim = x2d.shape
    qk = w_lr.shape[1]
    assert lr_col % rank == 0 and rank % SUBLANES == 0 and rank <= LANES
    whole = lambda a: pl.BlockSpec(a.shape, lambda i: (0,) * a.ndim)
    return pl.pallas_call(
        functools.partial(_prep_x_kernel, rank=rank),
        grid=(m // tm,),
        in_specs=[pl.BlockSpec((tm, kdim), lambda i: (i, 0)),
                  pl.BlockSpec((1, rank, kdim), lambda i: (layer, lr_col // rank, 0)),
                  whole(w_lr), whole(b_lr)],
        out_specs=[pl.BlockSpec((tm, kdim), lambda i: (i, 0)),
                   pl.BlockSpec((tm, qk), lambda i: (i, 0))],
        out_shape=[jax.ShapeDtypeStruct((m, kdim), _BF16),
                   jax.ShapeDtypeStruct((m, qk), _F32)],
        compiler_params=pltpu.CompilerParams(
            dimension_semantics=("arbitrary",), vmem_limit_bytes=VMEM_LIMIT_BYTES),
        name="prep_x",
    )(x2d, w_t, w_lr, b_lr)


def _proj_in_kernel(xp_ref, xs_ref, wa_ref, we_ref, hp_ref, hs_ref, wbf_ref,
                    *, n_plain, shift, silu_tiles, sigmoid_tiles):
    j = pl.program_id(0)
    i = pl.program_id(1)
    n_prompt = pl.num_programs(1) - 1
    tn, kdim = wbf_ref.shape

    @pl.when(i == 0)
    def _():
        @pl.when(j < n_plain)
        def _():
            wbf_ref[...] = wa_ref[0].astype(_BF16)

        @pl.when(j >= n_plain)
        def _():
            wbf_ref[0:tn - shift, :] = wa_ref[0, shift:tn, :].astype(_BF16)
            wbf_ref[tn - shift:tn, :] = we_ref[0].astype(_BF16)

    is_silu = functools.reduce(jnp.logical_or, [j == t for t in silu_tiles])
    is_sigmoid = functools.reduce(jnp.logical_or, [j == t for t in sigmoid_tiles])

    def tile(x_ref, h_ref):
        acc = lax.dot_general(x_ref[...], wbf_ref[...], _NT, preferred_element_type=_F32)
        s = jax.nn.sigmoid(acc)
        h_ref[...] = jnp.where(is_sigmoid, s, jnp.where(is_silu, acc * s, acc))

    @pl.when(i < n_prompt)
    def _():
        tile(xp_ref, hp_ref)

    @pl.when(i == n_prompt)
    def _():
        tile(xs_ref, hs_ref)


def _proj_in(xp, xs, w_t, layer, rank):
    mp, kdim = xp.shape
    ms = xs.shape[0]
    d_model = kdim
    tn = d_model // 2
    nj = 16
    n_plain = 6
    bf16_rows = 2 * SUBLANES
    assert rank % bf16_rows == 0 and tn % rank == 0
    tm = _pick_tile(mp, 1024)
    ni = mp // tm
    kern = functools.partial(_proj_in_kernel, n_plain=n_plain, shift=rank,
                             silu_tiles=(4, 5, 10, 11), sigmoid_tiles=(12, 13, 14, 15))
    return pl.pallas_call(
        kern,
        grid=(nj, ni + 1),
        in_specs=[
            pl.BlockSpec((tm, kdim), lambda j, i: (jnp.minimum(i, ni - 1), 0)),
            pl.BlockSpec((ms, kdim), lambda j, i: (0, 0)),
            pl.BlockSpec((1, tn, kdim), lambda j, i: (layer, j, 0)),
            pl.BlockSpec((1, rank, kdim), lambda j, i: (layer, (j + 1) * (tn // rank), 0)),
        ],
        out_specs=[
            pl.BlockSpec((tm, tn), lambda j, i: (jnp.minimum(i, ni - 1), j)),
            pl.BlockSpec((ms, tn), lambda j, i: (0, j)),
        ],
        out_shape=[
            jax.ShapeDtypeStruct((mp, nj * tn), _F32),
            jax.ShapeDtypeStruct((ms, nj * tn), _F32),
        ],
        scratch_shapes=[pltpu.VMEM((tn, kdim), _BF16)],
        compiler_params=pltpu.CompilerParams(
            dimension_semantics=("arbitrary", "arbitrary"),
            vmem_limit_bytes=VMEM_LIMIT_BYTES),
        name="proj_in",
    )(xp, xs, w_t, w_t)


def _h_cols(d_model, g):
    wq, wv = d_model // 8, d_model // 4
    d = d_model
    offs = dict(qa=(0, wq), ka=(d // 2, wq), va=(d, wv), za=(2 * d, wv),
                qb=(3 * d, wq), kb=(3 * d + d // 2, wq), vb=(4 * d, wv), zb=(5 * d, wv),
                ga=(6 * d, wv), gb=(7 * d, wv))
    return {n: slice(o + g * w, o + (g + 1) * w) for n, (o, w) in offs.items()}


def _mixer_prompt_kernel(h_ref, la_ref, gnw_ref, rnw_ref, rnb_ref,
                         cos_ref, sin_ref, lg_ref, m_ref, sg_ref, sr_ref):
    chunk = h_ref.shape[0]
    d_model = m_ref.shape[1]
    dk_a, dv_a = sg_ref.shape[3:]
    dk_b, dv_b = sr_ref.shape[3:]

    @pl.when(pl.program_id(1) == 0)
    def _():
        sg_ref[...] = jnp.zeros(sg_ref.shape, _F32)
        sr_ref[...] = jnp.zeros(sr_ref.shape, _F32)

    rows = _iota((chunk, 1), 0)
    cos, sin = cos_ref[...], sin_ref[...]
    tpos = (_iota(cos.shape, 0) + 1).astype(_F32)
    tdiff = _iota((chunk, chunk), 0) - _iota((chunk, chunk), 1)
    for g in range(HEAD_GROUPS):
        c = _h_cols(d_model, g)
        gq = slice(g * dk_a, (g + 1) * dk_a)
        gv = slice(g * dv_a, (g + 1) * dv_a)

        b = _seg_cumsum(la_ref[:, gq], chunk)
        q = h_ref[:, c["qa"]] * (dk_a ** -0.5)
        k = h_ref[:, c["ka"]]
        v = h_ref[:, c["va"]].astype(_BF16)
        s_gla = sg_ref[0, 0, g]
        o_a = _dot(q * jnp.exp(b), s_gla)
        intra = []
        for i in range(chunk // GLA_SUB):
            lo = i * GLA_SUB
            qi, ki, bi = q[lo:lo + GLA_SUB], k[lo:lo + GLA_SUB], b[lo:lo + GLA_SUB]
            if i == 0:
                a = jnp.zeros((GLA_SUB, chunk), _F32)
            else:
                r = b[lo - 1:lo, :]
                k_pre = k * jnp.exp(jnp.where(rows < lo, r - b, -jnp.inf))
                a = _dot_nt(qi * jnp.exp(bi - r), k_pre)
            a = _gla_pair_scores(a, qi, ki, bi, lo, lambda s: (s, GLA_SUB))
            intra.append(_dot(a, v))
        o_a = o_a + jnp.concatenate(intra, axis=0)
        b_last = b[chunk - 1:chunk, :]
        decay_col = _rows_to_cols([jnp.exp(b_last)])[:, 0:1]
        sg_ref[0, 0, g] = decay_col * s_gla + _dot_tn(k * jnp.exp(b_last - b), v)
        o_a = _rms_heads(o_a, gnw_ref[...]) * h_ref[:, c["za"]]

        qb = _rope(h_ref[:, c["qb"]], cos, sin)
        kb = _rope(h_ref[:, c["kb"]], cos, sin) * (dk_b ** -0.5)
        vb = h_ref[:, c["vb"]].astype(_BF16)
        lg = lg_ref[g]
        q_dec = qb * jnp.exp(lg * tpos)
        k_dec = kb * jnp.exp(lg * (chunk - tpos))
        o_b = []
        for hh in range(2):
            ksl = slice(hh * dk_b, (hh + 1) * dk_b)
            vsl = slice(hh * dv_b, (hh + 1) * dv_b)
            nsl = slice(g * dv_a + hh * dv_b, g * dv_a + (hh + 1) * dv_b)
            lg1 = lg[:, hh * dk_b:hh * dk_b + 1]
            decay = jnp.exp(jnp.where(tdiff >= 0, lg1 * tdiff.astype(_F32), -jnp.inf))
            s_ret = sr_ref[0, 0, 2 * g + hh]
            scores = _dot_nt(qb[:, ksl], kb[:, ksl]) * decay
            o_h = _dot(q_dec[:, ksl], s_ret) + _dot(scores, vb[:, vsl])
            sr_ref[0, 0, 2 * g + hh] = (jnp.exp(lg1 * chunk) * s_ret
                                        + _dot_tn(k_dec[:, ksl], vb[:, vsl]))
            o_b.append(_group_norm(o_h, rnw_ref[:, nsl], rnb_ref[:, nsl]))
        o_b = jnp.concatenate(o_b, axis=1) * h_ref[:, c["zb"]]

        merged = h_ref[:, c["ga"]] * o_a + h_ref[:, c["gb"]] * o_b
        m_ref[:, gv] = merged.astype(m_ref.dtype)


def _h_specs(rows, wq, wv, d_model, row_map):
    def spec(width, col_off):
        base = col_off // width
        return pl.BlockSpec((rows, width), lambda *ids: (row_map(*ids), base + ids[1]))
    d = d_model
    return [spec(wq, 0), spec(wq, d // 2), spec(wv, d), spec(wv, 2 * d),
            spec(wq, 3 * d), spec(wq, 3 * d + d // 2), spec(wv, 4 * d), spec(wv, 5 * d),
            spec(wv, 6 * d), spec(wv, 7 * d)]


def _param_specs(wq, wv):
    return [
        pl.BlockSpec((1, wv), lambda *ids: (0, 0)),
        pl.BlockSpec((1, wv), lambda *ids: (0, ids[1])),
        pl.BlockSpec((1, wv), lambda *ids: (0, ids[1])),
    ]


def _mixer_prompt(h, la, params, rope, lg, batch, seq, d_model):
    wq, wv = d_model // 8, d_model // 4
    dk_a, dv_a = wq, wv
    dk_b, dv_b = wq // 2, wv // 2
    chunk = PROMPT_CHUNK
    nc = seq // chunk
    whole = lambda a: pl.BlockSpec(a.shape, lambda b, c: (0,) * a.ndim)
    in_specs = [pl.BlockSpec((chunk, h.shape[1]), lambda b, c: (b * nc + c, 0)),
                pl.BlockSpec((chunk, la.shape[1]), lambda b, c: (b * nc + c, 0))]
    in_specs += [whole(p) for p in params]
    in_specs += [pl.BlockSpec((chunk, wq), lambda b, c: (c, 0)),
                 pl.BlockSpec((chunk, wq), lambda b, c: (c, 0)),
                 whole(lg)]
    out_specs = [
        pl.BlockSpec((chunk, d_model), lambda b, c: (b * nc + c, 0)),
        pl.BlockSpec((1, 1, GLA_HEADS, dk_a, dv_a), lambda b, c: (0, b, 0, 0, 0)),
        pl.BlockSpec((1, 1, RET_HEADS, dk_b, dv_b), lambda b, c: (0, b, 0, 0, 0)),
    ]
    out_shape = [
        jax.ShapeDtypeStruct((batch * seq, d_model), _BF16),
        jax.ShapeDtypeStruct((1, batch, GLA_HEADS, dk_a, dv_a), _F32),
        jax.ShapeDtypeStruct((1, batch, RET_HEADS, dk_b, dv_b), _F32),
    ]
    return pl.pallas_call(
        _mixer_prompt_kernel,
        grid=(batch, nc),
        in_specs=in_specs,
        out_specs=out_specs,
        out_shape=out_shape,
        compiler_params=pltpu.CompilerParams(
            dimension_semantics=("arbitrary", "arbitrary"),
            vmem_limit_bytes=VMEM_LIMIT_BYTES),
        name="mixer_prompt",
    )(h, la, *params, *rope, lg)


def _mixer_sample_kernel(qa_ref, ka_ref, va_ref, za_ref, qb_ref, kb_ref, vb_ref, zb_ref,
                         ga_ref, gb_ref, la_ref, gnw_ref, rnw_ref, rnb_ref,
                         cos_ref, sin_ref, lg_ref, sg0_ref, sr0_ref, m_ref, sg_ref, sr_ref,
                         *, steps):
    rows_n, dk_a = qa_ref.shape
    nseq = rows_n // steps
    dk_b = qb_ref.shape[1] // 2
    dv_b = vb_ref.shape[1] // 2
    rows = _iota((rows_n, 1), 0)
    pad = LANES - rows_n

    def pad_rows(x):
        return jnp.concatenate([x, jnp.zeros((pad, x.shape[1]), x.dtype)], axis=0)

    def seq_rows(x, n):
        return jnp.where((rows >= n * steps) & (rows < (n + 1) * steps), x, 0.0)

    b = _seg_cumsum(la_ref[...], steps)
    q = qa_ref[...] * (dk_a ** -0.5)
    k = ka_ref[...]
    v = pad_rows(va_ref[...]).astype(_BF16)
    b_last = [b[(n + 1) * steps - 1:(n + 1) * steps, :] for n in range(nseq)]
    b_end = jnp.zeros_like(b)
    for n in range(nseq):
        b_end = b_end + seq_rows(jnp.broadcast_to(b_last[n], b.shape), n)
    a = jnp.zeros((rows_n, LANES), _F32)
    a = _gla_pair_scores(a, q, k, b, 0, lambda s: (s, (s // steps + 1) * steps))
    o_a = _dot(a, v)
    q_dec = q * jnp.exp(b)
    k_dec = pad_rows(k * jnp.exp(b_end - b))
    decay_cols = _rows_to_cols([jnp.exp(r) for r in b_last])
    for n in range(nseq):
        s0 = sg0_ref[0, n, 0]
        o_a = o_a + _dot(seq_rows(q_dec, n), s0)
        k_n = jnp.where((_iota(k_dec.shape, 0) >= n * steps) & (_iota(k_dec.shape, 0) < (n + 1) * steps),
                        k_dec, 0.0)
        sg_ref[0, n, 0] = decay_cols[:, n:n + 1] * s0 + _dot_tn(k_n, v)
    o_a = _rms_heads(o_a, gnw_ref[...]) * za_ref[...]

    cos, sin = cos_ref[...], sin_ref[...]
    qb = _rope(qb_ref[...], cos, sin)
    kb = _rope(kb_ref[...], cos, sin) * (dk_b ** -0.5)
    vb = pad_rows(vb_ref[...]).astype(_BF16)
    lg = lg_ref[0]
    tpos = ((_iota(qb.shape, 0) & (steps - 1)) + 1).astype(_F32)
    q_dec = qb * jnp.exp(lg * tpos)
    k_dec = pad_rows(kb * jnp.exp(lg * (steps - tpos)))
    kb_p = pad_rows(kb)
    r_i = _iota((rows_n, LANES), 0)
    c_i = _iota((rows_n, LANES), 1)
    same_seq = (r_i >> (steps.bit_length() - 1)) == (c_i >> (steps.bit_length() - 1))
    tdiff = r_i - c_i
    o_b = []
    for hh in range(2):
        ksl = slice(hh * dk_b, (hh + 1) * dk_b)
        vsl = slice(hh * dv_b, (hh + 1) * dv_b)
        lg1 = lg[:, hh * dk_b:hh * dk_b + 1]
        decay = jnp.exp(jnp.where((tdiff >= 0) & same_seq, lg1 * tdiff.astype(_F32), -jnp.inf))
        scores = _dot_nt(qb[:, ksl], kb_p[:, ksl]) * decay
        o_h = _dot(scores, vb[:, vsl])
        for n in range(nseq):
            s0 = sr0_ref[0, n, hh]
            o_h = o_h + _dot(seq_rows(q_dec[:, ksl], n), s0)
            kd = k_dec[:, ksl]
            k_n = jnp.where((_iota(kd.shape, 0) >= n * steps) & (_iota(kd.shape, 0) < (n + 1) * steps),
                            kd, 0.0)
            sr_ref[0, n, hh] = jnp.exp(lg1 * steps) * s0 + _dot_tn(k_n, vb[:, vsl])
        o_b.append(_group_norm(o_h, rnw_ref[:, vsl], rnb_ref[:, vsl]))
    o_b = jnp.concatenate(o_b, axis=1) * zb_ref[...]

    merged = ga_ref[...] * o_a + gb_ref[...] * o_b
    m_ref[...] = merged.astype(m_ref.dtype)


def _mixer_sample(h, la, params, rope, lg, state_gla, state_ret, layer, nseq_total, steps, d_model):
    wq, wv = d_model // 8, d_model // 4
    dk_a, dv_a = wq, wv
    dk_b, dv_b = wq // 2, wv // 2
    nseq = SAMPLE_SEQS
    rows_n = nseq * steps
    assert steps & (steps - 1) == 0 and nseq_total % nseq == 0
    row_map = lambda p, g: p
    in_specs = _h_specs(rows_n, wq, wv, d_model, row_map)
    in_specs += [pl.BlockSpec((rows_n, wq), lambda p, g: (p, g))]
    in_specs += _param_specs(wq, wv)
    in_specs += [pl.BlockSpec((rows_n, wq), lambda p, g: (0, 0)),
                 pl.BlockSpec((rows_n, wq), lambda p, g: (0, 0)),
                 pl.BlockSpec((1, 1, wq), lambda p, g: (g, 0, 0)),
                 pl.BlockSpec((1, nseq, 1, dk_a, dv_a), lambda p, g: (layer, p, g, 0, 0)),
                 pl.BlockSpec((1, nseq, 2, dk_b, dv_b), lambda p, g: (layer, p, g, 0, 0))]
    out_specs = [
        pl.BlockSpec((rows_n, wv), lambda p, g: (p, g)),
        pl.BlockSpec((1, nseq, 1, dk_a, dv_a), lambda p, g: (0, p, g, 0, 0)),
        pl.BlockSpec((1, nseq, 2, dk_b, dv_b), lambda p, g: (0, p, g, 0, 0)),
    ]
    out_shape = [
        jax.ShapeDtypeStruct((nseq_total * steps, d_model), _BF16),
        jax.ShapeDtypeStruct((1, nseq_total, GLA_HEADS, dk_a, dv_a), _F32),
        jax.ShapeDtypeStruct((1, nseq_total, RET_HEADS, dk_b, dv_b), _F32),
    ]
    return pl.pallas_call(
        functools.partial(_mixer_sample_kernel, steps=steps),
        grid=(nseq_total // nseq, HEAD_GROUPS),
        in_specs=in_specs,
        out_specs=out_specs,
        out_shape=out_shape,
        compiler_params=pltpu.CompilerParams(
            dimension_semantics=("arbitrary", "arbitrary"),
            vmem_limit_bytes=VMEM_LIMIT_BYTES),
        name="mixer_sample",
    )(*([h] * 10), la, *params, *rope, lg, state_gla, state_ret)


def _proj_out_kernel(m_ref, x_ref, w_ref, lnw_ref, lnb_ref, y_ref, *, alpha):
    y = jnp.dot(m_ref[...], w_ref[...], preferred_element_type=_F32)
    r = alpha * x_ref[...] + y
    mu = jnp.mean(r, axis=-1, keepdims=True)
    d = r - mu
    var = jnp.mean(d * d, axis=-1, keepdims=True)
    y_ref[...] = d * lax.rsqrt(var + LN_EPS) * lnw_ref[...] + lnb_ref[...]


def _proj_out(merged, x2d, w_out, ln_w, ln_b, alpha, tm):
    m, d = x2d.shape
    e = merged.shape[1]
    return pl.pallas_call(
        functools.partial(_proj_out_kernel, alpha=alpha),
        grid=(m // tm,),
        in_specs=[
            pl.BlockSpec((tm, e), lambda i: (i, 0)),
            pl.BlockSpec((tm, d), lambda i: (i, 0)),
            pl.BlockSpec((e, d), lambda i: (0, 0)),
            pl.BlockSpec((1, d), lambda i: (0, 0)),
            pl.BlockSpec((1, d), lambda i: (0, 0)),
        ],
        out_specs=pl.BlockSpec((tm, d), lambda i: (i, 0)),
        out_shape=jax.ShapeDtypeStruct((m, d), _F32),
        compiler_params=pltpu.CompilerParams(
            dimension_semantics=("arbitrary",),
            vmem_limit_bytes=VMEM_LIMIT_BYTES),
        name="proj_out",
    )(merged, x2d, w_out, ln_w, ln_b)


def _rope_tables(pos, dk):
    inv = 1.0 / (ROPE_BASE ** jnp.linspace(0.0, 1.0, dk // 2, dtype=_F32))
    ang = pos.astype(_F32)[:, None] * inv[None, :]
    cos = jnp.repeat(jnp.cos(ang), 2, axis=1)
    sin = jnp.stack([-jnp.sin(ang), jnp.sin(ang)], axis=-1).reshape(ang.shape[0], dk)
    return jnp.tile(cos, (1, 2)), jnp.tile(sin, (1, 2))


def _pick_tile(n, pref):
    t = min(n, pref)
    while n % t:
        t //= 2
    return t


def kernel(x_prompt, x_sample, state_gla, state_ret, w_in, w_lr, b_lr, gla_norm_w,
           ret_norm_w, ret_norm_b, w_out, ln_w, ln_b):
    depth, d_model, _ = w_in.shape
    batch, seq, _ = x_prompt.shape
    dec_batch, dec_seq, _ = x_sample.shape
    rank = w_lr.shape[1]
    dk_b = d_model // 2 // RET_HEADS
    assert state_gla.shape[2] == GLA_HEADS and state_ret.shape[2] == RET_HEADS
    assert rank <= LANES and seq % PROMPT_CHUNK == 0
    alpha = (2.0 * depth) ** 0.25

    lg_heads = jnp.log(1.0 - 2.0 ** (-5.0 - jnp.arange(RET_HEADS, dtype=_F32)))
    lg = jnp.repeat(lg_heads, dk_b).reshape(HEAD_GROUPS, 1, 2 * dk_b)
    rope_p = _rope_tables(jnp.arange(seq, dtype=jnp.int32), dk_b)
    pos_s = PAST_LEN + jnp.arange(dec_seq, dtype=jnp.int32)
    rope_s = _rope_tables(jnp.tile(pos_s, SAMPLE_SEQS), dk_b)

    hp = x_prompt.reshape(batch * seq, d_model)
    hs = x_sample.reshape(dec_batch * dec_seq, d_model)
    w_t = jnp.swapaxes(w_in, 1, 2)
    gla_p, ret_p, gla_s, ret_s = [], [], [], []
    for l in range(depth):
        lr_lo = 3 * d_model
        w_lr_p = jnp.pad(w_lr[l], ((0, LANES - rank), (0, 0)))
        w_o = w_out[l].astype(_BF16)
        params = (gla_norm_w[l][None, :], ret_norm_w[l][None, :], ret_norm_b[l][None, :])
        lnw, lnb = ln_w[l][None, :], ln_b[l][None, :]

        xp_b, la_p = _prep_x(hp, w_t, l, lr_lo, rank, w_lr_p, b_lr[l][None, :],
                             _pick_tile(hp.shape[0], 512))
        xs_b, la_s = _prep_x(hs, w_t, l, lr_lo, rank, w_lr_p, b_lr[l][None, :],
                             _pick_tile(hs.shape[0], 512))
        h_p, h_s = _proj_in(xp_b, xs_b, w_t, l, rank)

        merged, sg, sr = _mixer_prompt(h_p, la_p, params, rope_p, lg, batch, seq, d_model)
        hp = _proj_out(merged, hp, w_o, lnw, lnb, alpha, _pick_tile(hp.shape[0], 512))
        gla_p.append(sg)
        ret_p.append(sr)

        merged, sg, sr = _mixer_sample(h_s, la_s, params, rope_s, lg, state_gla, state_ret, l,
                                       dec_batch, dec_seq, d_model)
        hs = _proj_out(merged, hs, w_o, lnw, lnb, alpha, _pick_tile(hs.shape[0], 256))
        gla_s.append(sg)
        ret_s.append(sr)

    cat = lambda parts: parts[0] if len(parts) == 1 else jnp.concatenate(parts, axis=0)
    return (hp.reshape(batch, seq, d_model), hs.reshape(dec_batch, dec_seq, d_model),
            cat(gla_p), cat(ret_p), cat(gla_s), cat(ret_s))
```

```python
import functools

import jax
import jax.numpy as jnp
from jax import lax
from jax.experimental import pallas as pl
from jax.experimental.pallas import tpu as pltpu

GLA_HEADS = 4
RET_HEADS = 8
HEAD_GROUPS = 4
GLA_TAU = 16.0
LOG2_E = 1.4426950408889634
ROPE_BASE = 10000.0
LN_EPS = 1e-5
HEAD_NORM_EPS = 1e-6
PAST_LEN = 16384

LANES = 128
SUBLANES = 8
VMEM_LIMIT_BYTES = 56 * 1024 * 1024

PROMPT_CHUNK = 64
GLA_SUB = 16
SAMPLE_SEQS = 4

_F32 = jnp.float32
_BF16 = jnp.bfloat16
_NT = (((1,), (1,)), ((), ()))
_TN = (((0,), (0,)), ((), ()))


def _dot(a, b):
    return jnp.dot(a.astype(_BF16), b.astype(_BF16), preferred_element_type=_F32)


def _dot_nt(a, b):
    return lax.dot_general(a.astype(_BF16), b.astype(_BF16), _NT, preferred_element_type=_F32)


def _dot_tn(a, b):
    return lax.dot_general(a.astype(_BF16), b.astype(_BF16), _TN, preferred_element_type=_F32)


def _log_sigmoid(x):
    return jnp.minimum(x, 0.0) - jnp.log(1.0 + jnp.exp(-jnp.abs(x)))


def _silu(x):
    return x * jax.nn.sigmoid(x)


def _iota(shape, axis):
    return lax.broadcasted_iota(jnp.int32, shape, axis)


def _seg_cumsum(x, seg):
    pos = _iota(x.shape, 0) & (seg - 1)
    s = 1
    while s < seg:
        x = x + jnp.where(pos >= s, pltpu.roll(x, s, axis=0), 0.0)
        s *= 2
    return x


def _rows_to_cols(rows):
    n = rows[0].shape[1]
    rid = _iota((LANES, n), 0)
    stack = jnp.zeros((LANES, n), _F32)
    for i, r in enumerate(rows):
        stack = jnp.where(rid == i, r, stack)
    return stack.T


def _rope(x, cos, sin_signed):
    w = x.shape[1]
    even = (_iota(x.shape, 1) & 1) == 0
    swapped = jnp.where(even, pltpu.roll(x, w - 1, axis=1), pltpu.roll(x, 1, axis=1))
    return x * cos + swapped * sin_signed


def _gla_pair_scores(a, q, k, b2, lane0, row_lo_hi):
    n = q.shape[0]
    lanes = _iota((SUBLANES, a.shape[1]), 1)
    tiles = [a[r:r + SUBLANES] for r in range(0, n, SUBLANES)]
    for s in range(n):
        lo, hi = row_lo_hi(s)
        for j, r0 in enumerate(range(0, n, SUBLANES)):
            if r0 + SUBLANES <= lo or r0 >= hi:
                continue
            e = jnp.exp2(b2[r0:r0 + SUBLANES] - b2[s:s + 1, :])
            col = jnp.sum(q[r0:r0 + SUBLANES] * k[s:s + 1, :] * e, axis=-1, keepdims=True)
            take = lanes == lane0 + s
            if lo > r0 or hi < r0 + SUBLANES:
                rows = _iota((SUBLANES, 1), 0) + r0
                take = take & (rows >= lo) & (rows < hi)
            tiles[j] = jnp.where(take, col, tiles[j])
    return jnp.concatenate(tiles, axis=0)


def _rms_heads(o, w):
    return o * lax.rsqrt(jnp.mean(o * o, axis=-1, keepdims=True) + HEAD_NORM_EPS) * w


def _group_norm(o, w, b):
    mu = jnp.mean(o, axis=-1, keepdims=True)
    d = o - mu
    var = jnp.mean(d * d, axis=-1, keepdims=True)
    return d * lax.rsqrt(var + HEAD_NORM_EPS) * w + b


def _prep_x_kernel(x_ref, wcol_ref, wlr_ref, blr_ref, xb_ref, la_ref, *, rank):
    xb = x_ref[...].astype(_BF16)
    xb_ref[...] = xb
    wrow = wcol_ref[0]
    w_lra = jnp.concatenate([wrow, jnp.zeros((LANES - rank, wrow.shape[1]), _F32)], axis=0)
    lr = _dot_nt(xb, w_lra)
    logit = jnp.dot(lr, wlr_ref[...], precision=lax.Precision.HIGHEST,
                    preferred_element_type=_F32) + blr_ref[...]
    la_ref[...] = _log_sigmoid(logit) * (1.0 / GLA_TAU)


def _prep_x(x2d, w_t, layer, lr_col, rank, w_lr, b_lr, tm):
    m, kdim = x2d.shape
    qk = w_lr.shape[1]
    assert lr_col % rank == 0 and rank % SUBLANES == 0 and rank <= LANES
    whole = lambda a: pl.BlockSpec(a.shape, lambda i: (0,) * a.ndim)
    return pl.pallas_call(
        functools.partial(_prep_x_kernel, rank=rank),
        grid=(m // tm,),
        in_specs=[pl.BlockSpec((tm, kdim), lambda i: (i, 0)),
                  pl.BlockSpec((1, rank, kdim), lambda i: (layer, lr_col // rank, 0)),
                  whole(w_lr), whole(b_lr)],
        out_specs=[pl.BlockSpec((tm, kdim), lambda i: (i, 0)),
                   pl.BlockSpec((tm, qk), lambda i: (i, 0))],
        out_shape=[jax.ShapeDtypeStruct((m, kdim), _BF16),
                   jax.ShapeDtypeStruct((m, qk), _F32)],
        compiler_params=pltpu.CompilerParams(
            dimension_semantics=("arbitrary",), vmem_limit_bytes=VMEM_LIMIT_BYTES),
        name="prep_x",
    )(x2d, w_t, w_lr, b_lr)


N_SAMPLE_IN = 19


def _proj_in_kernel(*refs, n_plain, shift, silu_tiles, sigmoid_tiles, sample_steps):
    x_ref, wa_ref, we_ref = refs[:3]
    wbf_ref = refs[-1]
    if sample_steps:
        unit_in = refs[3:3 + N_SAMPLE_IN]
        h_ref = refs[3 + N_SAMPLE_IN]
        unit_out = refs[4 + N_SAMPLE_IN:7 + N_SAMPLE_IN]
    else:
        h_ref = refs[3]
    j = pl.program_id(0)
    i = pl.program_id(1)
    tn, kdim = wbf_ref.shape

    @pl.when(i == 0)
    def _():
        @pl.when(j < n_plain)
        def _():
            wbf_ref[...] = wa_ref[0].astype(_BF16)

        @pl.when(j >= n_plain)
        def _():
            wbf_ref[0:tn - shift, :] = wa_ref[0, shift:tn, :].astype(_BF16)
            wbf_ref[tn - shift:tn, :] = we_ref[0].astype(_BF16)

    is_silu = functools.reduce(jnp.logical_or, [j == t for t in silu_tiles])
    is_sigmoid = functools.reduce(jnp.logical_or, [j == t for t in sigmoid_tiles])

    if sample_steps:
        _mixer_sample_unit(*unit_in, *unit_out, steps=sample_steps)

    acc = lax.dot_general(x_ref[...], wbf_ref[...], _NT, preferred_element_type=_F32)
    s = jax.nn.sigmoid(acc)
    h_ref[...] = jnp.where(is_sigmoid, s, jnp.where(is_silu, acc * s, acc))


PROJ_TILES = 16


def _proj_in(x, w_t, layer, rank, tm, sample=None):
    m, kdim = x.shape
    d_model = kdim
    tn = d_model // 2
    nj = PROJ_TILES
    n_plain = 6
    bf16_rows = 2 * SUBLANES
    assert rank % bf16_rows == 0 and tn % rank == 0 and m % tm == 0
    ni = m // tm
    in_specs = [
        pl.BlockSpec((tm, kdim), lambda j, i: (i, 0)),
        pl.BlockSpec((1, tn, kdim), lambda j, i: (layer, j, 0)),
        pl.BlockSpec((1, rank, kdim), lambda j, i: (layer, (j + 1) * (tn // rank), 0)),
    ]
    out_specs = [pl.BlockSpec((tm, tn), lambda j, i: (i, j))]
    out_shape = [jax.ShapeDtypeStruct((m, nj * tn), _F32)]
    operands = [x, w_t, w_t]
    sample_steps = 0
    if sample is not None:
        unit_operands, sample_steps, n_units, unit_specs = sample
        assert n_units <= nj * ni

        def unit_of(j, i):
            u = jnp.minimum(j * ni + i, n_units - 1)
            return u // HEAD_GROUPS, u % HEAD_GROUPS
        unit_in, unit_out, unit_shape = unit_specs(unit_of)
        assert len(unit_in) == N_SAMPLE_IN
        in_specs += unit_in
        out_specs += unit_out
        out_shape += unit_shape
        operands += unit_operands
    kern = functools.partial(_proj_in_kernel, n_plain=n_plain, shift=rank,
                             silu_tiles=(4, 5, 10, 11), sigmoid_tiles=(12, 13, 14, 15),
                             sample_steps=sample_steps)
    return pl.pallas_call(
        kern,
        grid=(nj, ni),
        in_specs=in_specs,
        out_specs=out_specs,
        out_shape=out_shape,
        scratch_shapes=[pltpu.VMEM((tn, kdim), _BF16)],
        compiler_params=pltpu.CompilerParams(
            dimension_semantics=("arbitrary", "arbitrary"),
            vmem_limit_bytes=VMEM_LIMIT_BYTES),
        name="proj_in_mix" if sample is not None else "proj_in",
    )(*operands)


def _h_cols(d_model, g):
    wq, wv = d_model // 8, d_model // 4
    d = d_model
    offs = dict(qa=(0, wq), ka=(d // 2, wq), va=(d, wv), za=(2 * d, wv),
                qb=(3 * d, wq), kb=(3 * d + d // 2, wq), vb=(4 * d, wv), zb=(5 * d, wv),
                ga=(6 * d, wv), gb=(7 * d, wv))
    return {n: slice(o + g * w, o + (g + 1) * w) for n, (o, w) in offs.items()}


def _mixer_prompt_kernel(h_ref, la_ref, gnw_ref, rnw_ref, rnb_ref,
                         cos_ref, sin_ref, lg_ref, m_ref, sg_ref, sr_ref):
    chunk = h_ref.shape[0]
    d_model = m_ref.shape[1]
    dk_a, dv_a = sg_ref.shape[3:]
    dk_b, dv_b = sr_ref.shape[3:]

    @pl.when(pl.program_id(1) == 0)
    def _():
        sg_ref[...] = jnp.zeros(sg_ref.shape, _F32)
        sr_ref[...] = jnp.zeros(sr_ref.shape, _F32)

    rows = _iota((chunk, 1), 0)
    cos, sin = cos_ref[...], sin_ref[...]
    tpos = (_iota(cos.shape, 0) + 1).astype(_F32)
    tdiff = _iota((chunk, chunk), 0) - _iota((chunk, chunk), 1)
    for g in range(HEAD_GROUPS):
        c = _h_cols(d_model, g)
        gq = slice(g * dk_a, (g + 1) * dk_a)
        gv = slice(g * dv_a, (g + 1) * dv_a)

        b = _seg_cumsum(la_ref[:, gq], chunk) * LOG2_E
        q = h_ref[:, c["qa"]] * (dk_a ** -0.5)
        k = h_ref[:, c["ka"]]
        v = h_ref[:, c["va"]].astype(_BF16)
        s_gla = sg_ref[0, 0, g]
        o_a = _dot(q * jnp.exp2(b), s_gla)
        intra = []
        for i in range(chunk // GLA_SUB):
            lo = i * GLA_SUB
            qi, ki, bi = q[lo:lo + GLA_SUB], k[lo:lo + GLA_SUB], b[lo:lo + GLA_SUB]
            if i == 0:
                a = jnp.zeros((GLA_SUB, chunk), _F32)
            else:
                r = b[lo - 1:lo, :]
                k_pre = k * jnp.exp2(jnp.where(rows < lo, r - b, -jnp.inf))
                a = _dot_nt(qi * jnp.exp2(bi - r), k_pre)
            a = _gla_pair_scores(a, qi, ki, bi, lo, lambda s: (s, GLA_SUB))
            intra.append(_dot(a, v))
        o_a = o_a + jnp.concatenate(intra, axis=0)
        b_last = b[chunk - 1:chunk, :]
        decay_col = _rows_to_cols([jnp.exp2(b_last)])[:, 0:1]
        sg_ref[0, 0, g] = decay_col * s_gla + _dot_tn(k * jnp.exp2(b_last - b), v)
        o_a = _rms_heads(o_a, gnw_ref[...]) * h_ref[:, c["za"]]

        qb = _rope(h_ref[:, c["qb"]], cos, sin)
        kb = _rope(h_ref[:, c["kb"]], cos, sin) * (dk_b ** -0.5)
        vb = h_ref[:, c["vb"]].astype(_BF16)
        lg = lg_ref[g]
        q_dec = qb * jnp.exp(lg * tpos)
        k_dec = kb * jnp.exp(lg * (chunk - tpos))
        o_b = []
        for hh in range(2):
            ksl = slice(hh * dk_b, (hh + 1) * dk_b)
            vsl = slice(hh * dv_b, (hh + 1) * dv_b)
            nsl = slice(g * dv_a + hh * dv_b, g * dv_a + (hh + 1) * dv_b)
            lg1 = lg[:, hh * dk_b:hh * dk_b + 1]
            decay = jnp.exp(jnp.where(tdiff >= 0, lg1 * tdiff.astype(_F32), -jnp.inf))
            s_ret = sr_ref[0, 0, 2 * g + hh]
            scores = _dot_nt(qb[:, ksl], kb[:, ksl]) * decay
            o_h = _dot(q_dec[:, ksl], s_ret) + _dot(scores, vb[:, vsl])
            sr_ref[0, 0, 2 * g + hh] = (jnp.exp(lg1 * chunk) * s_ret
                                        + _dot_tn(k_dec[:, ksl], vb[:, vsl]))
            o_b.append(_group_norm(o_h, rnw_ref[:, nsl], rnb_ref[:, nsl]))
        o_b = jnp.concatenate(o_b, axis=1) * h_ref[:, c["zb"]]

        merged = h_ref[:, c["ga"]] * o_a + h_ref[:, c["gb"]] * o_b
        m_ref[:, gv] = merged.astype(m_ref.dtype)


def _mixer_prompt(h, la, params, rope, lg, batch, seq, d_model):
    wq, wv = d_model // 8, d_model // 4
    dk_a, dv_a = wq, wv
    dk_b, dv_b = wq // 2, wv // 2
    chunk = PROMPT_CHUNK
    nc = seq // chunk
    whole = lambda a: pl.BlockSpec(a.shape, lambda b, c: (0,) * a.ndim)
    in_specs = [pl.BlockSpec((chunk, h.shape[1]), lambda b, c: (b * nc + c, 0)),
                pl.BlockSpec((chunk, la.shape[1]), lambda b, c: (b * nc + c, 0))]
    in_specs += [whole(p) for p in params]
    in_specs += [pl.BlockSpec((chunk, wq), lambda b, c: (c, 0)),
                 pl.BlockSpec((chunk, wq), lambda b, c: (c, 0)),
                 whole(lg)]
    out_specs = [
        pl.BlockSpec((chunk, d_model), lambda b, c: (b * nc + c, 0)),
        pl.BlockSpec((1, 1, GLA_HEADS, dk_a, dv_a), lambda b, c: (0, b, 0, 0, 0)),
        pl.BlockSpec((1, 1, RET_HEADS, dk_b, dv_b), lambda b, c: (0, b, 0, 0, 0)),
    ]
    out_shape = [
        jax.ShapeDtypeStruct((batch * seq, d_model), _BF16),
        jax.ShapeDtypeStruct((1, batch, GLA_HEADS, dk_a, dv_a), _F32),
        jax.ShapeDtypeStruct((1, batch, RET_HEADS, dk_b, dv_b), _F32),
    ]
    return pl.pallas_call(
        _mixer_prompt_kernel,
        grid=(batch, nc),
        in_specs=in_specs,
        out_specs=out_specs,
        out_shape=out_shape,
        compiler_params=pltpu.CompilerParams(
            dimension_semantics=("arbitrary", "arbitrary"),
            vmem_limit_bytes=VMEM_LIMIT_BYTES),
        name="mixer_prompt",
    )(h, la, *params, *rope, lg)


def _mixer_sample_unit(qa_ref, ka_ref, va_ref, za_ref, qb_ref, kb_ref, vb_ref, zb_ref,
                       ga_ref, gb_ref, la_ref, gnw_ref, rnw_ref, rnb_ref,
                       cos_ref, sin_ref, lg_ref, sg0_ref, sr0_ref, m_ref, sg_ref, sr_ref,
                       *, steps):
    rows_n, dk_a = qa_ref.shape
    nseq = rows_n // steps
    dk_b = qb_ref.shape[1] // 2
    dv_b = vb_ref.shape[1] // 2
    rows = _iota((rows_n, 1), 0)
    pad = LANES - rows_n

    def pad_rows(x):
        return jnp.concatenate([x, jnp.zeros((pad, x.shape[1]), x.dtype)], axis=0)

    def seq_rows(x, n):
        return jnp.where((rows >= n * steps) & (rows < (n + 1) * steps), x, 0.0)

    b = _seg_cumsum(la_ref[...], steps) * LOG2_E
    q = qa_ref[...] * (dk_a ** -0.5)
    k = ka_ref[...]
    v = pad_rows(va_ref[...]).astype(_BF16)
    b_last = [b[(n + 1) * steps - 1:(n + 1) * steps, :] for n in range(nseq)]
    b_end = jnp.zeros_like(b)
    for n in range(nseq):
        b_end = b_end + seq_rows(jnp.broadcast_to(b_last[n], b.shape), n)
    a = jnp.zeros((rows_n, LANES), _F32)
    a = _gla_pair_scores(a, q, k, b, 0, lambda s: (s, (s // steps + 1) * steps))
    o_a = _dot(a, v)
    q_dec = q * jnp.exp2(b)
    k_dec = pad_rows(k * jnp.exp2(b_end - b))
    decay_cols = _rows_to_cols([jnp.exp2(r) for r in b_last])
    for n in range(nseq):
        s0 = sg0_ref[0, n, 0]
        o_a = o_a + _dot(seq_rows(q_dec, n), s0)
        k_n = jnp.where((_iota(k_dec.shape, 0) >= n * steps) & (_iota(k_dec.shape, 0) < (n + 1) * steps),
                        k_dec, 0.0)
        sg_ref[0, n, 0] = decay_cols[:, n:n + 1] * s0 + _dot_tn(k_n, v)
    o_a = _rms_heads(o_a, gnw_ref[...]) * za_ref[...]

    cos, sin = cos_ref[...], sin_ref[...]
    qb = _rope(qb_ref[...], cos, sin)
    kb = _rope(kb_ref[...], cos, sin) * (dk_b ** -0.5)
    vb = pad_rows(vb_ref[...]).astype(_BF16)
    lg = lg_ref[0]
    tpos = ((_iota(qb.shape, 0) & (steps - 1)) + 1).astype(_F32)
    q_dec = qb * jnp.exp(lg * tpos)
    k_dec = pad_rows(kb * jnp.exp(lg * (steps - tpos)))
    kb_p = pad_rows(kb)
    r_i = _iota((rows_n, LANES), 0)
    c_i = _iota((rows_n, LANES), 1)
    same_seq = (r_i >> (steps.bit_length() - 1)) == (c_i >> (steps.bit_length() - 1))
    tdiff = r_i - c_i
    o_b = []
    for hh in range(2):
        ksl = slice(hh * dk_b, (hh + 1) * dk_b)
        vsl = slice(hh * dv_b, (hh + 1) * dv_b)
        lg1 = lg[:, hh * dk_b:hh * dk_b + 1]
        decay = jnp.exp(jnp.where((tdiff >= 0) & same_seq, lg1 * tdiff.astype(_F32), -jnp.inf))
        scores = _dot_nt(qb[:, ksl], kb_p[:, ksl]) * decay
        o_h = _dot(scores, vb[:, vsl])
        for n in range(nseq):
            s0 = sr0_ref[0, n, hh]
            o_h = o_h + _dot(seq_rows(q_dec[:, ksl], n), s0)
            kd = k_dec[:, ksl]
            k_n = jnp.where((_iota(kd.shape, 0) >= n * steps) & (_iota(kd.shape, 0) < (n + 1) * steps),
                            kd, 0.0)
            sr_ref[0, n, hh] = jnp.exp(lg1 * steps) * s0 + _dot_tn(k_n, vb[:, vsl])
        o_b.append(_group_norm(o_h, rnw_ref[:, vsl], rnb_ref[:, vsl]))
    o_b = jnp.concatenate(o_b, axis=1) * zb_ref[...]

    merged = ga_ref[...] * o_a + gb_ref[...] * o_b
    m_ref[...] = merged.astype(m_ref.dtype)


def _sample_units(h, la, params, rope, lg, state_gla, state_ret, layer, nseq_total, steps, d_model):
    wq, wv = d_model // 8, d_model // 4
    dk_a, dv_a = wq, wv
    dk_b, dv_b = wq // 2, wv // 2
    nseq = SAMPLE_SEQS
    rows_n = nseq * steps
    assert steps & (steps - 1) == 0 and nseq_total % nseq == 0
    d = d_model

    def specs(unit_of):
        def at(fn):
            return lambda *ids: fn(*unit_of(*ids))

        def h_spec(width, col_off):
            base = col_off // width
            return pl.BlockSpec((rows_n, width), at(lambda p, g: (p, base + g)))
        in_specs = [h_spec(wq, 0), h_spec(wq, d // 2), h_spec(wv, d), h_spec(wv, 2 * d),
                    h_spec(wq, 3 * d), h_spec(wq, 3 * d + d // 2), h_spec(wv, 4 * d),
                    h_spec(wv, 5 * d), h_spec(wv, 6 * d), h_spec(wv, 7 * d)]
        in_specs += [
            pl.BlockSpec((rows_n, wq), at(lambda p, g: (p, g))),
            pl.BlockSpec((1, wv), at(lambda p, g: (0, 0))),
            pl.BlockSpec((1, wv), at(lambda p, g: (0, g))),
            pl.BlockSpec((1, wv), at(lambda p, g: (0, g))),
            pl.BlockSpec((rows_n, wq), at(lambda p, g: (0, 0))),
            pl.BlockSpec((rows_n, wq), at(lambda p, g: (0, 0))),
            pl.BlockSpec((1, 1, wq), at(lambda p, g: (g, 0, 0))),
            pl.BlockSpec((1, nseq, 1, dk_a, dv_a), at(lambda p, g: (layer, p, g, 0, 0))),
            pl.BlockSpec((1, nseq, 2, dk_b, dv_b), at(lambda p, g: (layer, p, g, 0, 0))),
        ]
        out_specs = [
            pl.BlockSpec((rows_n, wv), at(lambda p, g: (p, g))),
            pl.BlockSpec((1, nseq, 1, dk_a, dv_a), at(lambda p, g: (0, p, g, 0, 0))),
            pl.BlockSpec((1, nseq, 2, dk_b, dv_b), at(lambda p, g: (0, p, g, 0, 0))),
        ]
        out_shape = [
            jax.ShapeDtypeStruct((nseq_total * steps, d_model), _BF16),
            jax.ShapeDtypeStruct((1, nseq_total, GLA_HEADS, dk_a, dv_a), _F32),
            jax.ShapeDtypeStruct((1, nseq_total, RET_HEADS, dk_b, dv_b), _F32),
        ]
        return in_specs, out_specs, out_shape

    operands = [h] * 10 + [la, *params, *rope, lg, state_gla, state_ret]
    n_units = (nseq_total // nseq) * HEAD_GROUPS
    return operands, steps, n_units, specs


def _proj_out_kernel(m_ref, x_ref, w_ref, lnw_ref, lnb_ref, y_ref, *, alpha):
    y = jnp.dot(m_ref[...], w_ref[...], preferred_element_type=_F32)
    r = alpha * x_ref[...] + y
    mu = jnp.mean(r, axis=-1, keepdims=True)
    d = r - mu
    var = jnp.mean(d * d, axis=-1, keepdims=True)
    y_ref[...] = d * lax.rsqrt(var + LN_EPS) * lnw_ref[...] + lnb_ref[...]


def _proj_out(merged, x2d, w_out, ln_w, ln_b, alpha, tm):
    m, d = x2d.shape
    e = merged.shape[1]
    return pl.pallas_call(
        functools.partial(_proj_out_kernel, alpha=alpha),
        grid=(m // tm,),
        in_specs=[
            pl.BlockSpec((tm, e), lambda i: (i, 0)),
            pl.BlockSpec((tm, d), lambda i: (i, 0)),
            pl.BlockSpec((e, d), lambda i: (0, 0)),
            pl.BlockSpec((1, d), lambda i: (0, 0)),
            pl.BlockSpec((1, d), lambda i: (0, 0)),
        ],
        out_specs=pl.BlockSpec((tm, d), lambda i: (i, 0)),
        out_shape=jax.ShapeDtypeStruct((m, d), _F32),
        compiler_params=pltpu.CompilerParams(
            dimension_semantics=("arbitrary",),
            vmem_limit_bytes=VMEM_LIMIT_BYTES),
        name="proj_out",
    )(merged, x2d, w_out, ln_w, ln_b)


def _rope_tables(pos, dk):
    inv = 1.0 / (ROPE_BASE ** jnp.linspace(0.0, 1.0, dk // 2, dtype=_F32))
    ang = pos.astype(_F32)[:, None] * inv[None, :]
    cos = jnp.repeat(jnp.cos(ang), 2, axis=1)
    sin = jnp.stack([-jnp.sin(ang), jnp.sin(ang)], axis=-1).reshape(ang.shape[0], dk)
    return jnp.tile(cos, (1, 2)), jnp.tile(sin, (1, 2))


def _pick_tile(n, pref):
    t = min(n, pref)
    while n % t:
        t //= 2
    return t


def kernel(x_prompt, x_sample, state_gla, state_ret, w_in, w_lr, b_lr, gla_norm_w,
           ret_norm_w, ret_norm_b, w_out, ln_w, ln_b):
    depth, d_model, _ = w_in.shape
    batch, seq, _ = x_prompt.shape
    dec_batch, dec_seq, _ = x_sample.shape
    rank = w_lr.shape[1]
    dk_b = d_model // 2 // RET_HEADS
    assert state_gla.shape[2] == GLA_HEADS and state_ret.shape[2] == RET_HEADS
    assert rank <= LANES and seq % PROMPT_CHUNK == 0
    alpha = (2.0 * depth) ** 0.25

    lg_heads = jnp.log(1.0 - 2.0 ** (-5.0 - jnp.arange(RET_HEADS, dtype=_F32)))
    lg = jnp.repeat(lg_heads, dk_b).reshape(HEAD_GROUPS, 1, 2 * dk_b)
    rope_p = _rope_tables(jnp.arange(seq, dtype=jnp.int32), dk_b)
    pos_s = PAST_LEN + jnp.arange(dec_seq, dtype=jnp.int32)
    rope_s = _rope_tables(jnp.tile(pos_s, SAMPLE_SEQS), dk_b)

    hp = x_prompt.reshape(batch * seq, d_model)
    hs = x_sample.reshape(dec_batch * dec_seq, d_model)
    w_t = jnp.swapaxes(w_in, 1, 2)
    gla_p, ret_p, gla_s, ret_s = [], [], [], []
    for l in range(depth):
        lr_lo = 3 * d_model
        w_lr_p = jnp.pad(w_lr[l], ((0, LANES - rank), (0, 0)))
        w_o = w_out[l].astype(_BF16)
        params = (gla_norm_w[l][None, :], ret_norm_w[l][None, :], ret_norm_b[l][None, :])
        lnw, lnb = ln_w[l][None, :], ln_b[l][None, :]

        xp_b, la_p = _prep_x(hp, w_t, l, lr_lo, rank, w_lr_p, b_lr[l][None, :],
                             _pick_tile(hp.shape[0], 512))
        xs_b, la_s = _prep_x(hs, w_t, l, lr_lo, rank, w_lr_p, b_lr[l][None, :],
                             _pick_tile(hs.shape[0], 512))
        (h_s,) = _proj_in(xs_b, w_t, l, rank, _pick_tile(xs_b.shape[0], 1024))
        units = _sample_units(h_s, la_s, params, rope_s, lg, state_gla, state_ret, l,
                              dec_batch, dec_seq, d_model)
        h_p, merged_s, sg_s, sr_s = _proj_in(xp_b, w_t, l, rank, _pick_tile(xp_b.shape[0], 1024),
                                             sample=units)

        merged, sg, sr = _mixer_prompt(h_p, la_p, params, rope_p, lg, batch, seq, d_model)
        hp = _proj_out(merged, hp, w_o, lnw, lnb, alpha, _pick_tile(hp.shape[0], 512))
        gla_p.append(sg)
        ret_p.append(sr)

        merged, sg, sr = merged_s, sg_s, sr_s
        hs = _proj_out(merged, hs, w_o, lnw, lnb, alpha, _pick_tile(hs.shape[0], 256))
        gla_s.append(sg)
        ret_s.append(sr)

    cat = lambda parts: parts[0] if len(parts) == 1 else jnp.concatenate(parts, axis=0)
    return (hp.reshape(batch, seq, d_model), hs.reshape(dec_batch, dec_seq, d_model),
            cat(gla_p), cat(ret_p), cat(gla_s), cat(ret_s))
```

```python
import functools

import jax
import jax.numpy as jnp
from jax import lax
from jax.experimental import pallas as pl
from jax.experimental.pallas import tpu as pltpu

GLA_HEADS = 4
RET_HEADS = 8
HEAD_GROUPS = 4
GLA_TAU = 16.0
LOG2_E = 1.4426950408889634
ROPE_BASE = 10000.0
LN_EPS = 1e-5
HEAD_NORM_EPS = 1e-6
PAST_LEN = 16384

LANES = 128
SUBLANES = 8
VMEM_LIMIT_BYTES = 56 * 1024 * 1024

PROMPT_CHUNK = 64
GLA_SUB = 16
SAMPLE_SEQS = 4

_F32 = jnp.float32
_BF16 = jnp.bfloat16
_NT = (((1,), (1,)), ((), ()))
_TN = (((0,), (0,)), ((), ()))


def _dot(a, b):
    return jnp.dot(a.astype(_BF16), b.astype(_BF16), preferred_element_type=_F32)


def _dot_nt(a, b):
    return lax.dot_general(a.astype(_BF16), b.astype(_BF16), _NT, preferred_element_type=_F32)


def _dot_tn(a, b):
    return lax.dot_general(a.astype(_BF16), b.astype(_BF16), _TN, preferred_element_type=_F32)


def _log_sigmoid(x):
    return jnp.minimum(x, 0.0) - jnp.log(1.0 + jnp.exp(-jnp.abs(x)))


def _silu(x):
    return x * jax.nn.sigmoid(x)


def _iota(shape, axis):
    return lax.broadcasted_iota(jnp.int32, shape, axis)


def _seg_cumsum(x, seg):
    pos = _iota(x.shape, 0) & (seg - 1)
    s = 1
    while s < seg:
        x = x + jnp.where(pos >= s, pltpu.roll(x, s, axis=0), 0.0)
        s *= 2
    return x


def _rows_to_cols(rows):
    n = rows[0].shape[1]
    rid = _iota((LANES, n), 0)
    stack = jnp.zeros((LANES, n), _F32)
    for i, r in enumerate(rows):
        stack = jnp.where(rid == i, r, stack)
    return stack.T


def _rope(x, cos, sin_signed):
    w = x.shape[1]
    even = (_iota(x.shape, 1) & 1) == 0
    swapped = jnp.where(even, pltpu.roll(x, w - 1, axis=1), pltpu.roll(x, 1, axis=1))
    return x * cos + swapped * sin_signed


def _gla_pair_scores(a, q, k, b2, lane0, row_lo_hi):
    n = q.shape[0]
    lanes = _iota((SUBLANES, a.shape[1]), 1)
    tiles = [a[r:r + SUBLANES] for r in range(0, n, SUBLANES)]
    for s in range(n):
        lo, hi = row_lo_hi(s)
        for j, r0 in enumerate(range(0, n, SUBLANES)):
            if r0 + SUBLANES <= lo or r0 >= hi:
                continue
            e = jnp.exp2(b2[r0:r0 + SUBLANES] - b2[s:s + 1, :])
            col = jnp.sum(q[r0:r0 + SUBLANES] * k[s:s + 1, :] * e, axis=-1, keepdims=True)
            take = lanes == lane0 + s
            if lo > r0 or hi < r0 + SUBLANES:
                rows = _iota((SUBLANES, 1), 0) + r0
                take = take & (rows >= lo) & (rows < hi)
            tiles[j] = jnp.where(take, col, tiles[j])
    return jnp.concatenate(tiles, axis=0)


def _rms_heads(o, w):
    return o * lax.rsqrt(jnp.mean(o * o, axis=-1, keepdims=True) + HEAD_NORM_EPS) * w


def _group_norm(o, w, b):
    mu = jnp.mean(o, axis=-1, keepdims=True)
    d = o - mu
    var = jnp.mean(d * d, axis=-1, keepdims=True)
    return d * lax.rsqrt(var + HEAD_NORM_EPS) * w + b


def _prep_x_kernel(x_ref, wcol_ref, wlr_ref, blr_ref, xb_ref, la_ref, *, rank):
    xb = x_ref[...].astype(_BF16)
    xb_ref[...] = xb
    wrow = wcol_ref[0]
    w_lra = jnp.concatenate([wrow, jnp.zeros((LANES - rank, wrow.shape[1]), _F32)], axis=0)
    lr = _dot_nt(xb, w_lra)
    w = wlr_ref[...]
    w_hi = w.astype(_BF16).astype(_F32)
    w_lo = w - w_hi
    lr_hi = lr.astype(_BF16).astype(_F32)
    lr_lo = lr - lr_hi
    lr_cat = lr_hi + pltpu.roll(lr_hi, rank, axis=1) + pltpu.roll(lr_lo, 2 * rank, axis=1)
    w_cat = jnp.concatenate([w_hi[:rank], w_lo[:rank], w_hi[:rank],
                             jnp.zeros((LANES - 3 * rank, w.shape[1]), _F32)], axis=0)
    logit = _dot(lr_cat, w_cat) + blr_ref[...]
    la_ref[...] = _log_sigmoid(logit) * (1.0 / GLA_TAU)


def _prep_x(x2d, w_t, layer, lr_col, rank, w_lr, b_lr, tm):
    m, kdim = x2d.shape
    qk = w_lr.shape[1]
    assert lr_col % rank == 0 and rank % SUBLANES == 0 and 3 * rank <= LANES
    whole = lambda a: pl.BlockSpec(a.shape, lambda i: (0,) * a.ndim)
    return pl.pallas_call(
        functools.partial(_prep_x_kernel, rank=rank),
        grid=(m // tm,),
        in_specs=[pl.BlockSpec((tm, kdim), lambda i: (i, 0)),
                  pl.BlockSpec((1, rank, kdim), lambda i: (layer, lr_col // rank, 0)),
                  whole(w_lr), whole(b_lr)],
        out_specs=[pl.BlockSpec((tm, kdim), lambda i: (i, 0)),
                   pl.BlockSpec((tm, qk), lambda i: (i, 0))],
        out_shape=[jax.ShapeDtypeStruct((m, kdim), _BF16),
                   jax.ShapeDtypeStruct((m, qk), _F32)],
        compiler_params=pltpu.CompilerParams(
            dimension_semantics=("arbitrary",), vmem_limit_bytes=VMEM_LIMIT_BYTES),
        name="prep_x",
    )(x2d, w_t, w_lr, b_lr)


N_SAMPLE_IN = 19


def _proj_in_kernel(*refs, n_plain, shift, silu_tiles, sigmoid_tiles, sample_steps):
    x_ref, wa_ref, we_ref = refs[:3]
    wbf_ref = refs[-1]
    if sample_steps:
        unit_in = refs[3:3 + N_SAMPLE_IN]
        h_ref = refs[3 + N_SAMPLE_IN]
        unit_out = refs[4 + N_SAMPLE_IN:7 + N_SAMPLE_IN]
    else:
        h_ref = refs[3]
    j = pl.program_id(0)
    i = pl.program_id(1)
    tn, kdim = wbf_ref.shape

    @pl.when(i == 0)
    def _():
        @pl.when(j < n_plain)
        def _():
            wbf_ref[...] = wa_ref[0].astype(_BF16)

        @pl.when(j >= n_plain)
        def _():
            wbf_ref[0:tn - shift, :] = wa_ref[0, shift:tn, :].astype(_BF16)
            wbf_ref[tn - shift:tn, :] = we_ref[0].astype(_BF16)

    is_silu = functools.reduce(jnp.logical_or, [j == t for t in silu_tiles])
    is_sigmoid = functools.reduce(jnp.logical_or, [j == t for t in sigmoid_tiles])

    def tile(activated):
        if sample_steps:
            _mixer_sample_unit(*unit_in, *unit_out, steps=sample_steps)
        acc = lax.dot_general(x_ref[...], wbf_ref[...], _NT, preferred_element_type=_F32)
        if activated:
            s = 0.5 * jnp.tanh(0.5 * acc) + 0.5
            acc = jnp.where(is_sigmoid, s, acc * s)
        h_ref[...] = acc

    @pl.when(is_silu | is_sigmoid)
    def _():
        tile(True)

    @pl.when(jnp.logical_not(is_silu | is_sigmoid))
    def _():
        tile(False)


PROJ_TILES = 16


def _proj_in(x, w_t, layer, rank, tm, sample=None):
    m, kdim = x.shape
    d_model = kdim
    tn = d_model // 2
    nj = PROJ_TILES
    n_plain = 6
    bf16_rows = 2 * SUBLANES
    assert rank % bf16_rows == 0 and tn % rank == 0 and m % tm == 0
    ni = m // tm
    in_specs = [
        pl.BlockSpec((tm, kdim), lambda j, i: (i, 0)),
        pl.BlockSpec((1, tn, kdim), lambda j, i: (layer, j, 0)),
        pl.BlockSpec((1, rank, kdim), lambda j, i: (layer, (j + 1) * (tn // rank), 0)),
    ]
    out_specs = [pl.BlockSpec((tm, tn), lambda j, i: (i, j))]
    out_shape = [jax.ShapeDtypeStruct((m, nj * tn), _F32)]
    operands = [x, w_t, w_t]
    sample_steps = 0
    if sample is not None:
        unit_operands, sample_steps, n_units, unit_specs = sample
        assert n_units <= nj * ni

        def unit_of(j, i):
            u = jnp.minimum(j * ni + i, n_units - 1)
            return u // HEAD_GROUPS, u % HEAD_GROUPS
        unit_in, unit_out, unit_shape = unit_specs(unit_of)
        assert len(unit_in) == N_SAMPLE_IN
        in_specs += unit_in
        out_specs += unit_out
        out_shape += unit_shape
        operands += unit_operands
    kern = functools.partial(_proj_in_kernel, n_plain=n_plain, shift=rank,
                             silu_tiles=(4, 5, 10, 11), sigmoid_tiles=(12, 13, 14, 15),
                             sample_steps=sample_steps)
    return pl.pallas_call(
        kern,
        grid=(nj, ni),
        in_specs=in_specs,
        out_specs=out_specs,
        out_shape=out_shape,
        scratch_shapes=[pltpu.VMEM((tn, kdim), _BF16)],
        compiler_params=pltpu.CompilerParams(
            dimension_semantics=("arbitrary", "arbitrary"),
            vmem_limit_bytes=VMEM_LIMIT_BYTES),
        name="proj_in_mix" if sample is not None else "proj_in",
    )(*operands)


def _h_cols(d_model, g):
    wq, wv = d_model // 8, d_model // 4
    d = d_model
    offs = dict(qa=(0, wq), ka=(d // 2, wq), va=(d, wv), za=(2 * d, wv),
                qb=(3 * d, wq), kb=(3 * d + d // 2, wq), vb=(4 * d, wv), zb=(5 * d, wv),
                ga=(6 * d, wv), gb=(7 * d, wv))
    return {n: slice(o + g * w, o + (g + 1) * w) for n, (o, w) in offs.items()}


def _mixer_prompt_kernel(h_ref, la_ref, gnw_ref, rnw_ref, rnb_ref,
                         cos_ref, sin_ref, lg_ref, m_ref, sg_ref, sr_ref):
    chunk = h_ref.shape[0]
    d_model = m_ref.shape[1]
    dk_a, dv_a = sg_ref.shape[3:]
    dk_b, dv_b = sr_ref.shape[3:]

    @pl.when(pl.program_id(1) == 0)
    def _():
        sg_ref[...] = jnp.zeros(sg_ref.shape, _F32)
        sr_ref[...] = jnp.zeros(sr_ref.shape, _F32)

    rows = _iota((chunk, 1), 0)
    cos, sin = cos_ref[...], sin_ref[...]
    tpos = (_iota(cos.shape, 0) + 1).astype(_F32)
    tdiff = _iota((chunk, chunk), 0) - _iota((chunk, chunk), 1)
    for g in range(HEAD_GROUPS):
        c = _h_cols(d_model, g)
        gq = slice(g * dk_a, (g + 1) * dk_a)
        gv = slice(g * dv_a, (g + 1) * dv_a)

        b = _seg_cumsum(la_ref[:, gq], chunk) * LOG2_E
        q = h_ref[:, c["qa"]] * (dk_a ** -0.5)
        k = h_ref[:, c["ka"]]
        v = h_ref[:, c["va"]].astype(_BF16)
        s_gla = sg_ref[0, 0, g]
        o_a = _dot(q * jnp.exp2(b), s_gla)
        intra = []
        for i in range(chunk // GLA_SUB):
            lo = i * GLA_SUB
            qi, ki, bi = q[lo:lo + GLA_SUB], k[lo:lo + GLA_SUB], b[lo:lo + GLA_SUB]
            if i == 0:
                a = jnp.zeros((GLA_SUB, chunk), _F32)
            else:
                r = b[lo - 1:lo, :]
                k_pre = k * jnp.exp2(jnp.where(rows < lo, r - b, -jnp.inf))
                a = _dot_nt(qi * jnp.exp2(bi - r), k_pre)
            a = _gla_pair_scores(a, qi, ki, bi, lo, lambda s: (s, GLA_SUB))
            intra.append(_dot(a, v))
        o_a = o_a + jnp.concatenate(intra, axis=0)
        b_last = b[chunk - 1:chunk, :]
        decay_col = _rows_to_cols([jnp.exp2(b_last)])[:, 0:1]
        sg_ref[0, 0, g] = decay_col * s_gla + _dot_tn(k * jnp.exp2(b_last - b), v)
        o_a = _rms_heads(o_a, gnw_ref[...]) * h_ref[:, c["za"]]

        qb = _rope(h_ref[:, c["qb"]], cos, sin)
        kb = _rope(h_ref[:, c["kb"]], cos, sin) * (dk_b ** -0.5)
        vb = h_ref[:, c["vb"]].astype(_BF16)
        lg = lg_ref[g]
        q_dec = qb * jnp.exp(lg * tpos)
        k_dec = kb * jnp.exp(lg * (chunk - tpos))
        o_b = []
        for hh in range(2):
            ksl = slice(hh * dk_b, (hh + 1) * dk_b)
            vsl = slice(hh * dv_b, (hh + 1) * dv_b)
            nsl = slice(g * dv_a + hh * dv_b, g * dv_a + (hh + 1) * dv_b)
            lg1 = lg[:, hh * dk_b:hh * dk_b + 1]
            decay = jnp.exp(jnp.where(tdiff >= 0, lg1 * tdiff.astype(_F32), -jnp.inf))
            s_ret = sr_ref[0, 0, 2 * g + hh]
            scores = _dot_nt(qb[:, ksl], kb[:, ksl]) * decay
            o_h = _dot(q_dec[:, ksl], s_ret) + _dot(scores, vb[:, vsl])
            sr_ref[0, 0, 2 * g + hh] = (jnp.exp(lg1 * chunk) * s_ret
                                        + _dot_tn(k_dec[:, ksl], vb[:, vsl]))
            o_b.append(_group_norm(o_h, rnw_ref[:, nsl], rnb_ref[:, nsl]))
        o_b = jnp.concatenate(o_b, axis=1) * h_ref[:, c["zb"]]

        merged = h_ref[:, c["ga"]] * o_a + h_ref[:, c["gb"]] * o_b
        m_ref[:, gv] = merged.astype(m_ref.dtype)


def _mixer_prompt(h, la, params, rope, lg, batch, seq, d_model):
    wq, wv = d_model // 8, d_model // 4
    dk_a, dv_a = wq, wv
    dk_b, dv_b = wq // 2, wv // 2
    chunk = PROMPT_CHUNK
    nc = seq // chunk
    whole = lambda a: pl.BlockSpec(a.shape, lambda b, c: (0,) * a.ndim)
    in_specs = [pl.BlockSpec((chunk, h.shape[1]), lambda b, c: (b * nc + c, 0)),
                pl.BlockSpec((chunk, la.shape[1]), lambda b, c: (b * nc + c, 0))]
    in_specs += [whole(p) for p in params]
    in_specs += [pl.BlockSpec((chunk, wq), lambda b, c: (c, 0)),
                 pl.BlockSpec((chunk, wq), lambda b, c: (c, 0)),
                 whole(lg)]
    out_specs = [
        pl.BlockSpec((chunk, d_model), lambda b, c: (b * nc + c, 0)),
        pl.BlockSpec((1, 1, GLA_HEADS, dk_a, dv_a), lambda b, c: (0, b, 0, 0, 0)),
        pl.BlockSpec((1, 1, RET_HEADS, dk_b, dv_b), lambda b, c: (0, b, 0, 0, 0)),
    ]
    out_shape = [
        jax.ShapeDtypeStruct((batch * seq, d_model), _BF16),
        jax.ShapeDtypeStruct((1, batch, GLA_HEADS, dk_a, dv_a), _F32),
        jax.ShapeDtypeStruct((1, batch, RET_HEADS, dk_b, dv_b), _F32),
    ]
    return pl.pallas_call(
        _mixer_prompt_kernel,
        grid=(batch, nc),
        in_specs=in_specs,
        out_specs=out_specs,
        out_shape=out_shape,
        compiler_params=pltpu.CompilerParams(
            dimension_semantics=("arbitrary", "arbitrary"),
            vmem_limit_bytes=VMEM_LIMIT_BYTES),
        name="mixer_prompt",
    )(h, la, *params, *rope, lg)


def _mixer_sample_unit(qa_ref, ka_ref, va_ref, za_ref, qb_ref, kb_ref, vb_ref, zb_ref,
                       ga_ref, gb_ref, la_ref, gnw_ref, rnw_ref, rnb_ref,
                       cos_ref, sin_ref, lg_ref, sg0_ref, sr0_ref, m_ref, sg_ref, sr_ref,
                       *, steps):
    rows_n, dk_a = qa_ref.shape
    nseq = rows_n // steps
    dk_b = qb_ref.shape[1] // 2
    dv_b = vb_ref.shape[1] // 2
    rows = _iota((rows_n, 1), 0)
    pad = LANES - rows_n

    def pad_rows(x):
        return jnp.concatenate([x, jnp.zeros((pad, x.shape[1]), x.dtype)], axis=0)

    def seq_rows(x, n):
        return jnp.where((rows >= n * steps) & (rows < (n + 1) * steps), x, 0.0)

    b = _seg_cumsum(la_ref[...], steps) * LOG2_E
    q = qa_ref[...] * (dk_a ** -0.5)
    k = ka_ref[...]
    v = pad_rows(va_ref[...]).astype(_BF16)
    b_last = [b[(n + 1) * steps - 1:(n + 1) * steps, :] for n in range(nseq)]
    b_end = jnp.zeros_like(b)
    for n in range(nseq):
        b_end = b_end + seq_rows(jnp.broadcast_to(b_last[n], b.shape), n)
    a = jnp.zeros((rows_n, LANES), _F32)
    a = _gla_pair_scores(a, q, k, b, 0, lambda s: (s, (s // steps + 1) * steps))
    o_a = _dot(a, v)
    q_dec = q * jnp.exp2(b)
    k_dec = pad_rows(k * jnp.exp2(b_end - b))
    decay_cols = _rows_to_cols([jnp.exp2(r) for r in b_last])
    for n in range(nseq):
        s0 = sg0_ref[0, n, 0]
        o_a = o_a + _dot(seq_rows(q_dec, n), s0)
        k_n = jnp.where((_iota(k_dec.shape, 0) >= n * steps) & (_iota(k_dec.shape, 0) < (n + 1) * steps),
                        k_dec, 0.0)
        sg_ref[0, n, 0] = decay_cols[:, n:n + 1] * s0 + _dot_tn(k_n, v)
    o_a = _rms_heads(o_a, gnw_ref[...]) * za_ref[...]

    cos, sin = cos_ref[...], sin_ref[...]
    qb = _rope(qb_ref[...], cos, sin)
    kb = _rope(kb_ref[...], cos, sin) * (dk_b ** -0.5)
    vb = pad_rows(vb_ref[...]).astype(_BF16)
    lg = lg_ref[0]
    tpos = ((_iota(qb.shape, 0) & (steps - 1)) + 1).astype(_F32)
    q_dec = qb * jnp.exp(lg * tpos)
    k_dec = pad_rows(kb * jnp.exp(lg * (steps - tpos)))
    kb_p = pad_rows(kb)
    r_i = _iota((rows_n, LANES), 0)
    c_i = _iota((rows_n, LANES), 1)
    same_seq = (r_i >> (steps.bit_length() - 1)) == (c_i >> (steps.bit_length() - 1))
    tdiff = r_i - c_i
    o_b = []
    for hh in range(2):
        ksl = slice(hh * dk_b, (hh + 1) * dk_b)
        vsl = slice(hh * dv_b, (hh + 1) * dv_b)
        lg1 = lg[:, hh * dk_b:hh * dk_b + 1]
        decay = jnp.exp(jnp.where((tdiff >= 0) & same_seq, lg1 * tdiff.astype(_F32), -jnp.inf))
        scores = _dot_nt(qb[:, ksl], kb_p[:, ksl]) * decay
        o_h = _dot(scores, vb[:, vsl])
        for n in range(nseq):
            s0 = sr0_ref[0, n, hh]
            o_h = o_h + _dot(seq_rows(q_dec[:, ksl], n), s0)
            kd = k_dec[:, ksl]
            k_n = jnp.where((_iota(kd.shape, 0) >= n * steps) & (_iota(kd.shape, 0) < (n + 1) * steps),
                            kd, 0.0)
            sr_ref[0, n, hh] = jnp.exp(lg1 * steps) * s0 + _dot_tn(k_n, vb[:, vsl])
        o_b.append(_group_norm(o_h, rnw_ref[:, vsl], rnb_ref[:, vsl]))
    o_b = jnp.concatenate(o_b, axis=1) * zb_ref[...]

    merged = ga_ref[...] * o_a + gb_ref[...] * o_b
    m_ref[...] = merged.astype(m_ref.dtype)


def _sample_units(h, la, params, rope, lg, state_gla, state_ret, layer, nseq_total, steps, d_model):
    wq, wv = d_model // 8, d_model // 4
    dk_a, dv_a = wq, wv
    dk_b, dv_b = wq // 2, wv // 2
    nseq = SAMPLE_SEQS
    rows_n = nseq * steps
    assert steps & (steps - 1) == 0 and nseq_total % nseq == 0
    d = d_model

    def specs(unit_of):
        def at(fn):
            return lambda *ids: fn(*unit_of(*ids))

        def h_spec(width, col_off):
            base = col_off // width
            return pl.BlockSpec((rows_n, width), at(lambda p, g: (p, base + g)))
        in_specs = [h_spec(wq, 0), h_spec(wq, d // 2), h_spec(wv, d), h_spec(wv, 2 * d),
                    h_spec(wq, 3 * d), h_spec(wq, 3 * d + d // 2), h_spec(wv, 4 * d),
                    h_spec(wv, 5 * d), h_spec(wv, 6 * d), h_spec(wv, 7 * d)]
        in_specs += [
            pl.BlockSpec((rows_n, wq), at(lambda p, g: (p, g))),
            pl.BlockSpec((1, wv), at(lambda p, g: (0, 0))),
            pl.BlockSpec((1, wv), at(lambda p, g: (0, g))),
            pl.BlockSpec((1, wv), at(lambda p, g: (0, g))),
            pl.BlockSpec((rows_n, wq), at(lambda p, g: (0, 0))),
            pl.BlockSpec((rows_n, wq), at(lambda p, g: (0, 0))),
            pl.BlockSpec((1, 1, wq), at(lambda p, g: (g, 0, 0))),
            pl.BlockSpec((1, nseq, 1, dk_a, dv_a), at(lambda p, g: (layer, p, g, 0, 0))),
            pl.BlockSpec((1, nseq, 2, dk_b, dv_b), at(lambda p, g: (layer, p, g, 0, 0))),
        ]
        out_specs = [
            pl.BlockSpec((rows_n, wv), at(lambda p, g: (p, g))),
            pl.BlockSpec((1, nseq, 1, dk_a, dv_a), at(lambda p, g: (0, p, g, 0, 0))),
            pl.BlockSpec((1, nseq, 2, dk_b, dv_b), at(lambda p, g: (0, p, g, 0, 0))),
        ]
        out_shape = [
            jax.ShapeDtypeStruct((nseq_total * steps, d_model), _BF16),
            jax.ShapeDtypeStruct((1, nseq_total, GLA_HEADS, dk_a, dv_a), _F32),
            jax.ShapeDtypeStruct((1, nseq_total, RET_HEADS, dk_b, dv_b), _F32),
        ]
        return in_specs, out_specs, out_shape

    operands = [h] * 10 + [la, *params, *rope, lg, state_gla, state_ret]
    n_units = (nseq_total // nseq) * HEAD_GROUPS
    return operands, steps, n_units, specs


def _out_proj_ln(m, x, w, ln_w, ln_b, alpha):
    r = alpha * x + jnp.dot(m, w, preferred_element_type=_F32)
    mu = jnp.mean(r, axis=-1, keepdims=True)
    d = r - mu
    var = jnp.mean(d * d, axis=-1, keepdims=True)
    return d * lax.rsqrt(var + LN_EPS) * ln_w + ln_b


def _proj_out_kernel(m_ref, x_ref, w_ref, lnw_ref, lnb_ref, y_ref, *, alpha):
    y_ref[...] = _out_proj_ln(m_ref[...], x_ref[...], w_ref[...], lnw_ref[...], lnb_ref[...], alpha)


def _proj_out(merged, x2d, w_out, ln_w, ln_b, alpha, tm):
    m, d = x2d.shape
    e = merged.shape[1]
    return pl.pallas_call(
        functools.partial(_proj_out_kernel, alpha=alpha),
        grid=(m // tm,),
        in_specs=[
            pl.BlockSpec((tm, e), lambda i: (i, 0)),
            pl.BlockSpec((tm, d), lambda i: (i, 0)),
            pl.BlockSpec((e, d), lambda i: (0, 0)),
            pl.BlockSpec((1, d), lambda i: (0, 0)),
            pl.BlockSpec((1, d), lambda i: (0, 0)),
        ],
        out_specs=pl.BlockSpec((tm, d), lambda i: (i, 0)),
        out_shape=jax.ShapeDtypeStruct((m, d), _F32),
        compiler_params=pltpu.CompilerParams(
            dimension_semantics=("arbitrary",),
            vmem_limit_bytes=VMEM_LIMIT_BYTES),
        name="proj_out",
    )(merged, x2d, w_out, ln_w, ln_b)


def _rope_tables(pos, dk):
    inv = 1.0 / (ROPE_BASE ** jnp.linspace(0.0, 1.0, dk // 2, dtype=_F32))
    ang = pos.astype(_F32)[:, None] * inv[None, :]
    cos = jnp.repeat(jnp.cos(ang), 2, axis=1)
    sin = jnp.stack([-jnp.sin(ang), jnp.sin(ang)], axis=-1).reshape(ang.shape[0], dk)
    return jnp.tile(cos, (1, 2)), jnp.tile(sin, (1, 2))


def _pick_tile(n, pref):
    t = min(n, pref)
    while n % t:
        t //= 2
    return t


def kernel(x_prompt, x_sample, state_gla, state_ret, w_in, w_lr, b_lr, gla_norm_w,
           ret_norm_w, ret_norm_b, w_out, ln_w, ln_b):
    depth, d_model, _ = w_in.shape
    batch, seq, _ = x_prompt.shape
    dec_batch, dec_seq, _ = x_sample.shape
    rank = w_lr.shape[1]
    dk_b = d_model // 2 // RET_HEADS
    assert state_gla.shape[2] == GLA_HEADS and state_ret.shape[2] == RET_HEADS
    assert rank <= LANES and seq % PROMPT_CHUNK == 0
    alpha = (2.0 * depth) ** 0.25

    lg_heads = jnp.log(1.0 - 2.0 ** (-5.0 - jnp.arange(RET_HEADS, dtype=_F32)))
    lg = jnp.repeat(lg_heads, dk_b).reshape(HEAD_GROUPS, 1, 2 * dk_b)
    rope_p = _rope_tables(jnp.arange(seq, dtype=jnp.int32), dk_b)
    pos_s = PAST_LEN + jnp.arange(dec_seq, dtype=jnp.int32)
    rope_s = _rope_tables(jnp.tile(pos_s, SAMPLE_SEQS), dk_b)

    hp = x_prompt.reshape(batch * seq, d_model)
    hs = x_sample.reshape(dec_batch * dec_seq, d_model)
    w_t = jnp.swapaxes(w_in, 1, 2)
    gla_p, ret_p, gla_s, ret_s = [], [], [], []
    for l in range(depth):
        lr_lo = 3 * d_model
        w_lr_p = jnp.pad(w_lr[l], ((0, LANES - rank), (0, 0)))
        w_o = w_out[l].astype(_BF16)
        params = (gla_norm_w[l][None, :], ret_norm_w[l][None, :], ret_norm_b[l][None, :])
        lnw, lnb = ln_w[l][None, :], ln_b[l][None, :]

        xp_b, la_p = _prep_x(hp, w_t, l, lr_lo, rank, w_lr_p, b_lr[l][None, :],
                             _pick_tile(hp.shape[0], 512))
        xs_b, la_s = _prep_x(hs, w_t, l, lr_lo, rank, w_lr_p, b_lr[l][None, :],
                             _pick_tile(hs.shape[0], 512))
        (h_s,) = _proj_in(xs_b, w_t, l, rank, _pick_tile(xs_b.shape[0], 1024))
        units = _sample_units(h_s, la_s, params, rope_s, lg, state_gla, state_ret, l,
                              dec_batch, dec_seq, d_model)
        h_p, merged_s, sg_s, sr_s = _proj_in(xp_b, w_t, l, rank, _pick_tile(xp_b.shape[0], 1024),
                                             sample=units)

        merged, sg, sr = _mixer_prompt(h_p, la_p, params, rope_p, lg, batch, seq, d_model)
        hp = _proj_out(merged, hp, w_o, lnw, lnb, alpha, _pick_tile(hp.shape[0], 512))
        gla_p.append(sg)
        ret_p.append(sr)

        merged, sg, sr = merged_s, sg_s, sr_s
        hs = _proj_out(merged, hs, w_o, lnw, lnb, alpha, _pick_tile(hs.shape[0], 256))
        gla_s.append(sg)
        ret_s.append(sr)

    cat = lambda parts: parts[0] if len(parts) == 1 else jnp.concatenate(parts, axis=0)
    return (hp.reshape(batch, seq, d_model), hs.reshape(dec_batch, dec_seq, d_model),
            cat(gla_p), cat(ret_p), cat(gla_s), cat(ret_s))
```

```python
import functools

import jax
import jax.numpy as jnp
from jax import lax
from jax.experimental import pallas as pl
from jax.experimental.pallas import tpu as pltpu

GLA_HEADS = 4
RET_HEADS = 8
HEAD_GROUPS = 4
GLA_TAU = 16.0
LOG2_E = 1.4426950408889634
ROPE_BASE = 10000.0
LN_EPS = 1e-5
HEAD_NORM_EPS = 1e-6
PAST_LEN = 16384

LANES = 128
SUBLANES = 8
VMEM_LIMIT_BYTES = 56 * 1024 * 1024

PROMPT_CHUNK = 64
GLA_SUB = 16
SAMPLE_SEQS = 4

_F32 = jnp.float32
_BF16 = jnp.bfloat16
_NT = (((1,), (1,)), ((), ()))
_TN = (((0,), (0,)), ((), ()))


def _dot(a, b):
    return jnp.dot(a.astype(_BF16), b.astype(_BF16), preferred_element_type=_F32)


def _dot_nt(a, b):
    return lax.dot_general(a.astype(_BF16), b.astype(_BF16), _NT, preferred_element_type=_F32)


def _dot_tn(a, b):
    return lax.dot_general(a.astype(_BF16), b.astype(_BF16), _TN, preferred_element_type=_F32)


def _log_sigmoid(x):
    return jnp.minimum(x, 0.0) - jnp.log(1.0 + jnp.exp(-jnp.abs(x)))


def _silu(x):
    return x * jax.nn.sigmoid(x)


def _iota(shape, axis):
    return lax.broadcasted_iota(jnp.int32, shape, axis)


def _seg_cumsum(x, seg):
    pos = _iota(x.shape, 0) & (seg - 1)
    s = 1
    while s < seg:
        x = x + jnp.where(pos >= s, pltpu.roll(x, s, axis=0), 0.0)
        s *= 2
    return x


def _rows_to_cols(rows):
    n = rows[0].shape[1]
    rid = _iota((LANES, n), 0)
    stack = jnp.zeros((LANES, n), _F32)
    for i, r in enumerate(rows):
        stack = jnp.where(rid == i, r, stack)
    return stack.T


def _rope(x, cos, sin_signed):
    w = x.shape[1]
    even = (_iota(x.shape, 1) & 1) == 0
    swapped = jnp.where(even, pltpu.roll(x, w - 1, axis=1), pltpu.roll(x, 1, axis=1))
    return x * cos + swapped * sin_signed


def _gla_pair_scores(a, q, k, b2, lane0, row_lo_hi):
    n = q.shape[0]
    lanes = _iota((SUBLANES, a.shape[1]), 1)
    tiles = [a[r:r + SUBLANES] for r in range(0, n, SUBLANES)]
    for s in range(n):
        lo, hi = row_lo_hi(s)
        for j, r0 in enumerate(range(0, n, SUBLANES)):
            if r0 + SUBLANES <= lo or r0 >= hi:
                continue
            e = jnp.exp2(b2[r0:r0 + SUBLANES] - b2[s:s + 1, :])
            col = jnp.sum(q[r0:r0 + SUBLANES] * k[s:s + 1, :] * e, axis=-1, keepdims=True)
            take = lanes == lane0 + s
            if lo > r0 or hi < r0 + SUBLANES:
                rows = _iota((SUBLANES, 1), 0) + r0
                take = take & (rows >= lo) & (rows < hi)
            tiles[j] = jnp.where(take, col, tiles[j])
    return jnp.concatenate(tiles, axis=0)


def _rms_heads(o, w):
    return o * lax.rsqrt(jnp.mean(o * o, axis=-1, keepdims=True) + HEAD_NORM_EPS) * w


def _group_norm(o, w, b):
    mu = jnp.mean(o, axis=-1, keepdims=True)
    d = o - mu
    var = jnp.mean(d * d, axis=-1, keepdims=True)
    return d * lax.rsqrt(var + HEAD_NORM_EPS) * w + b


def _prep_x_kernel(x_ref, wcol_ref, wlr_ref, blr_ref, xb_ref, la_ref, *, rank):
    xb = x_ref[...].astype(_BF16)
    xb_ref[...] = xb
    wrow = wcol_ref[0]
    w_lra = jnp.concatenate([wrow, jnp.zeros((LANES - rank, wrow.shape[1]), _F32)], axis=0)
    lr = _dot_nt(xb, w_lra)
    w = wlr_ref[...]
    w_hi = w.astype(_BF16).astype(_F32)
    w_lo = w - w_hi
    lr_hi = lr.astype(_BF16).astype(_F32)
    lr_lo = lr - lr_hi
    lr_cat = lr_hi + pltpu.roll(lr_hi, rank, axis=1) + pltpu.roll(lr_lo, 2 * rank, axis=1)
    w_cat = jnp.concatenate([w_hi[:rank], w_lo[:rank], w_hi[:rank],
                             jnp.zeros((LANES - 3 * rank, w.shape[1]), _F32)], axis=0)
    logit = _dot(lr_cat, w_cat) + blr_ref[...]
    la_ref[...] = _log_sigmoid(logit) * (1.0 / GLA_TAU)


def _prep_x(x2d, w_t, layer, lr_col, rank, w_lr, b_lr, tm):
    m, kdim = x2d.shape
    qk = w_lr.shape[1]
    assert lr_col % rank == 0 and rank % SUBLANES == 0 and 3 * rank <= LANES
    whole = lambda a: pl.BlockSpec(a.shape, lambda i: (0,) * a.ndim)
    return pl.pallas_call(
        functools.partial(_prep_x_kernel, rank=rank),
        grid=(m // tm,),
        in_specs=[pl.BlockSpec((tm, kdim), lambda i: (i, 0)),
                  pl.BlockSpec((1, rank, kdim), lambda i: (layer, lr_col // rank, 0)),
                  whole(w_lr), whole(b_lr)],
        out_specs=[pl.BlockSpec((tm, kdim), lambda i: (i, 0)),
                   pl.BlockSpec((tm, qk), lambda i: (i, 0))],
        out_shape=[jax.ShapeDtypeStruct((m, kdim), _BF16),
                   jax.ShapeDtypeStruct((m, qk), _F32)],
        compiler_params=pltpu.CompilerParams(
            dimension_semantics=("arbitrary",), vmem_limit_bytes=VMEM_LIMIT_BYTES),
        name="prep_x",
    )(x2d, w_t, w_lr, b_lr)


N_SAMPLE_IN = 19


def _proj_in_kernel(*refs, n_plain, shift, silu_tiles, sigmoid_tiles, sample_steps):
    x_ref, wa_ref, we_ref = refs[:3]
    wbf_ref = refs[-1]
    if sample_steps:
        unit_in = refs[3:3 + N_SAMPLE_IN]
        h_ref = refs[3 + N_SAMPLE_IN]
        unit_out = refs[4 + N_SAMPLE_IN:7 + N_SAMPLE_IN]
    else:
        h_ref = refs[3]
    j = pl.program_id(0)
    i = pl.program_id(1)
    tn, kdim = wbf_ref.shape

    @pl.when(i == 0)
    def _():
        @pl.when(j < n_plain)
        def _():
            wbf_ref[...] = wa_ref[0].astype(_BF16)

        @pl.when(j >= n_plain)
        def _():
            wbf_ref[0:tn - shift, :] = wa_ref[0, shift:tn, :].astype(_BF16)
            wbf_ref[tn - shift:tn, :] = we_ref[0].astype(_BF16)

    is_silu = functools.reduce(jnp.logical_or, [j == t for t in silu_tiles])
    is_sigmoid = functools.reduce(jnp.logical_or, [j == t for t in sigmoid_tiles])

    def slab(cols, activated):
        acc = lax.dot_general(x_ref[...], wbf_ref[cols, :], _NT, preferred_element_type=_F32)
        if activated:
            s = 0.5 * jnp.tanh(0.5 * acc) + 0.5
            acc = jnp.where(is_sigmoid, s, acc * s)
        h_ref[:, cols] = acc

    def tile(activated):
        finish_unit = None
        if sample_steps:
            finish_unit = _mixer_sample_unit(*unit_in, *unit_out, steps=sample_steps)
        slab(slice(0, tn // 4), activated)
        if finish_unit is not None:
            finish_unit()
        slab(slice(tn // 4, tn), activated)

    @pl.when(is_silu | is_sigmoid)
    def _():
        tile(True)

    @pl.when(jnp.logical_not(is_silu | is_sigmoid))
    def _():
        tile(False)


PROJ_TILES = 16


def _proj_in(x, w_t, layer, rank, tm, sample=None):
    m, kdim = x.shape
    d_model = kdim
    tn = d_model // 2
    nj = PROJ_TILES
    n_plain = 6
    bf16_rows = 2 * SUBLANES
    assert rank % bf16_rows == 0 and tn % rank == 0 and m % tm == 0
    ni = m // tm
    in_specs = [
        pl.BlockSpec((tm, kdim), lambda j, i: (i, 0)),
        pl.BlockSpec((1, tn, kdim), lambda j, i: (layer, j, 0)),
        pl.BlockSpec((1, rank, kdim), lambda j, i: (layer, (j + 1) * (tn // rank), 0)),
    ]
    out_specs = [pl.BlockSpec((tm, tn), lambda j, i: (i, j))]
    out_shape = [jax.ShapeDtypeStruct((m, nj * tn), _F32)]
    operands = [x, w_t, w_t]
    sample_steps = 0
    if sample is not None:
        unit_operands, sample_steps, n_units, unit_specs = sample
        assert n_units <= nj * ni

        def unit_of(j, i):
            u = jnp.minimum(j * ni + i, n_units - 1)
            return u // HEAD_GROUPS, u % HEAD_GROUPS
        unit_in, unit_out, unit_shape = unit_specs(unit_of)
        assert len(unit_in) == N_SAMPLE_IN
        in_specs += unit_in
        out_specs += unit_out
        out_shape += unit_shape
        operands += unit_operands
    kern = functools.partial(_proj_in_kernel, n_plain=n_plain, shift=rank,
                             silu_tiles=(4, 5, 10, 11), sigmoid_tiles=(12, 13, 14, 15),
                             sample_steps=sample_steps)
    return pl.pallas_call(
        kern,
        grid=(nj, ni),
        in_specs=in_specs,
        out_specs=out_specs,
        out_shape=out_shape,
        scratch_shapes=[pltpu.VMEM((tn, kdim), _BF16)],
        compiler_params=pltpu.CompilerParams(
            dimension_semantics=("arbitrary", "arbitrary"),
            vmem_limit_bytes=VMEM_LIMIT_BYTES),
        name="proj_in_mix" if sample is not None else "proj_in",
    )(*operands)


def _h_cols(d_model, g):
    wq, wv = d_model // 8, d_model // 4
    d = d_model
    offs = dict(qa=(0, wq), ka=(d // 2, wq), va=(d, wv), za=(2 * d, wv),
                qb=(3 * d, wq), kb=(3 * d + d // 2, wq), vb=(4 * d, wv), zb=(5 * d, wv),
                ga=(6 * d, wv), gb=(7 * d, wv))
    return {n: slice(o + g * w, o + (g + 1) * w) for n, (o, w) in offs.items()}


def _mixer_prompt_kernel(h_ref, la_ref, gnw_ref, rnw_ref, rnb_ref,
                         cos_ref, sin_ref, lg_ref, m_ref, sg_ref, sr_ref):
    chunk = h_ref.shape[0]
    d_model = m_ref.shape[1]
    dk_a, dv_a = sg_ref.shape[3:]
    dk_b, dv_b = sr_ref.shape[3:]

    @pl.when(pl.program_id(1) == 0)
    def _():
        sg_ref[...] = jnp.zeros(sg_ref.shape, _F32)
        sr_ref[...] = jnp.zeros(sr_ref.shape, _F32)

    rows = _iota((chunk, 1), 0)
    cos, sin = cos_ref[...], sin_ref[...]
    tpos = (_iota(cos.shape, 0) + 1).astype(_F32)
    tdiff = _iota((chunk, chunk), 0) - _iota((chunk, chunk), 1)
    n_sub = chunk // GLA_SUB
    groups = range(HEAD_GROUPS)
    cols = [_h_cols(d_model, g) for g in groups]
    ksl = [slice(hh * dk_b, (hh + 1) * dk_b) for hh in range(2)]
    vsl = [slice(hh * dv_b, (hh + 1) * dv_b) for hh in range(2)]


    gla, ret = [], []
    for g in groups:
        c = cols[g]
        b = _seg_cumsum(la_ref[:, g * dk_a:(g + 1) * dk_a], chunk) * LOG2_E
        q = h_ref[:, c["qa"]] * (dk_a ** -0.5)
        k = h_ref[:, c["ka"]]
        v = h_ref[:, c["va"]].astype(_BF16)
        s_gla = sg_ref[0, 0, g]
        gla.append(dict(b=b, q=q, k=k, v=v, s=s_gla, o=_dot(q * jnp.exp2(b), s_gla)))

        qb = _rope(h_ref[:, c["qb"]], cos, sin)
        kb = _rope(h_ref[:, c["kb"]], cos, sin) * (dk_b ** -0.5)
        vb = h_ref[:, c["vb"]].astype(_BF16)
        lg = lg_ref[g]
        q_dec = qb * jnp.exp(lg * tpos)
        s_ret = [sr_ref[0, 0, 2 * g + hh] for hh in range(2)]
        ret.append(dict(
            vb=vb, lg=lg, s=s_ret, k_dec=kb * jnp.exp(lg * (chunk - tpos)),
            o=[_dot(q_dec[:, ksl[hh]], s_ret[hh]) for hh in range(2)],
            qk=[_dot_nt(qb[:, ksl[hh]], kb[:, ksl[hh]]) for hh in range(2)]))

    for g in groups:
        q, k, b = gla[g]["q"], gla[g]["k"], gla[g]["b"]
        a_off = [jnp.zeros((GLA_SUB, chunk), _F32)]
        for i in range(1, n_sub):
            lo = i * GLA_SUB
            r = b[lo - 1:lo, :]
            k_pre = k * jnp.exp2(jnp.where(rows < lo, r - b, -jnp.inf))
            a_off.append(_dot_nt(q[lo:lo + GLA_SUB] * jnp.exp2(b[lo:lo + GLA_SUB] - r), k_pre))
        gla[g]["a"] = a_off

    for g in groups:
        q, k, b, v = gla[g]["q"], gla[g]["k"], gla[g]["b"], gla[g]["v"]
        intra = []
        for i in range(n_sub):
            lo = i * GLA_SUB
            sub = slice(lo, lo + GLA_SUB)
            a = _gla_pair_scores(gla[g]["a"][i], q[sub], k[sub], b[sub], lo, lambda s: (s, GLA_SUB))
            intra.append(_dot(a, v))
        gla[g]["o"] = gla[g]["o"] + jnp.concatenate(intra, axis=0)
        for hh in range(2):
            lg1 = ret[g]["lg"][:, hh * dk_b:hh * dk_b + 1]
            decay = jnp.exp(jnp.where(tdiff >= 0, lg1 * tdiff.astype(_F32), -jnp.inf))
            ret[g]["o"][hh] = ret[g]["o"][hh] + _dot(ret[g]["qk"][hh] * decay, ret[g]["vb"][:, vsl[hh]])

    for g in groups:
        k, b, v = gla[g]["k"], gla[g]["b"], gla[g]["v"]
        b_last = b[chunk - 1:chunk, :]
        decay_col = _rows_to_cols([jnp.exp2(b_last)])[:, 0:1]
        sg_ref[0, 0, g] = decay_col * gla[g]["s"] + _dot_tn(k * jnp.exp2(b_last - b), v)
        for hh in range(2):
            lg1 = ret[g]["lg"][:, hh * dk_b:hh * dk_b + 1]
            sr_ref[0, 0, 2 * g + hh] = (jnp.exp(lg1 * chunk) * ret[g]["s"][hh]
                                        + _dot_tn(ret[g]["k_dec"][:, ksl[hh]], ret[g]["vb"][:, vsl[hh]]))

    for g in groups:
        c = cols[g]
        gv = slice(g * dv_a, (g + 1) * dv_a)
        o_a = _rms_heads(gla[g]["o"], gnw_ref[...]) * h_ref[:, c["za"]]
        o_b = []
        for hh in range(2):
            nsl = slice(g * dv_a + hh * dv_b, g * dv_a + (hh + 1) * dv_b)
            o_b.append(_group_norm(ret[g]["o"][hh], rnw_ref[:, nsl], rnb_ref[:, nsl]))
        o_b = jnp.concatenate(o_b, axis=1) * h_ref[:, c["zb"]]
        merged = h_ref[:, c["ga"]] * o_a + h_ref[:, c["gb"]] * o_b
        m_ref[:, gv] = merged.astype(m_ref.dtype)


def _mixer_prompt(h, la, params, rope, lg, batch, seq, d_model):
    wq, wv = d_model // 8, d_model // 4
    dk_a, dv_a = wq, wv
    dk_b, dv_b = wq // 2, wv // 2
    chunk = PROMPT_CHUNK
    nc = seq // chunk
    whole = lambda a: pl.BlockSpec(a.shape, lambda b, c: (0,) * a.ndim)
    in_specs = [pl.BlockSpec((chunk, h.shape[1]), lambda b, c: (b * nc + c, 0)),
                pl.BlockSpec((chunk, la.shape[1]), lambda b, c: (b * nc + c, 0))]
    in_specs += [whole(p) for p in params]
    in_specs += [pl.BlockSpec((chunk, wq), lambda b, c: (c, 0)),
                 pl.BlockSpec((chunk, wq), lambda b, c: (c, 0)),
                 whole(lg)]
    out_specs = [
        pl.BlockSpec((chunk, d_model), lambda b, c: (b * nc + c, 0)),
        pl.BlockSpec((1, 1, GLA_HEADS, dk_a, dv_a), lambda b, c: (0, b, 0, 0, 0)),
        pl.BlockSpec((1, 1, RET_HEADS, dk_b, dv_b), lambda b, c: (0, b, 0, 0, 0)),
    ]
    out_shape = [
        jax.ShapeDtypeStruct((batch * seq, d_model), _BF16),
        jax.ShapeDtypeStruct((1, batch, GLA_HEADS, dk_a, dv_a), _F32),
        jax.ShapeDtypeStruct((1, batch, RET_HEADS, dk_b, dv_b), _F32),
    ]
    return pl.pallas_call(
        _mixer_prompt_kernel,
        grid=(batch, nc),
        in_specs=in_specs,
        out_specs=out_specs,
        out_shape=out_shape,
        compiler_params=pltpu.CompilerParams(
            dimension_semantics=("arbitrary", "arbitrary"),
            vmem_limit_bytes=VMEM_LIMIT_BYTES),
        name="mixer_prompt",
    )(h, la, *params, *rope, lg)


def _mixer_sample_unit(qa_ref, ka_ref, va_ref, za_ref, qb_ref, kb_ref, vb_ref, zb_ref,
                       ga_ref, gb_ref, la_ref, gnw_ref, rnw_ref, rnb_ref,
                       cos_ref, sin_ref, lg_ref, sg0_ref, sr0_ref, m_ref, sg_ref, sr_ref,
                       *, steps):
    rows_n, dk_a = qa_ref.shape
    nseq = rows_n // steps
    dk_b = qb_ref.shape[1] // 2
    dv_b = vb_ref.shape[1] // 2
    rows = _iota((rows_n, 1), 0)
    rows_p = _iota((LANES, 1), 0)
    pad = LANES - rows_n
    ksl = [slice(hh * dk_b, (hh + 1) * dk_b) for hh in range(2)]
    vsl = [slice(hh * dv_b, (hh + 1) * dv_b) for hh in range(2)]

    def pad_rows(x):
        return jnp.concatenate([x, jnp.zeros((pad, x.shape[1]), x.dtype)], axis=0)

    def seq_rows(x, n, r=rows):
        return jnp.where((r >= n * steps) & (r < (n + 1) * steps), x, 0.0)

    b = _seg_cumsum(la_ref[...], steps) * LOG2_E
    q = qa_ref[...] * (dk_a ** -0.5)
    k = ka_ref[...]
    v = pad_rows(va_ref[...]).astype(_BF16)
    q_dec = q * jnp.exp2(b)
    s_gla = [sg0_ref[0, n, 0] for n in range(nseq)]
    o_a = _dot(seq_rows(q_dec, 0), s_gla[0])
    for n in range(1, nseq):
        o_a = o_a + _dot(seq_rows(q_dec, n), s_gla[n])

    cos, sin = cos_ref[...], sin_ref[...]
    qb = _rope(qb_ref[...], cos, sin)
    kb = _rope(kb_ref[...], cos, sin) * (dk_b ** -0.5)
    vb = pad_rows(vb_ref[...]).astype(_BF16)
    lg = lg_ref[0]
    lg1 = [lg[:, hh * dk_b:hh * dk_b + 1] for hh in range(2)]
    tpos = ((_iota(qb.shape, 0) & (steps - 1)) + 1).astype(_F32)
    qb_dec = qb * jnp.exp(lg * tpos)
    kb_p = pad_rows(kb)
    s_ret = [[sr0_ref[0, n, hh] for n in range(nseq)] for hh in range(2)]
    o_r, qk = [], []
    for hh in range(2):
        o_h = _dot(seq_rows(qb_dec[:, ksl[hh]], 0), s_ret[hh][0])
        for n in range(1, nseq):
            o_h = o_h + _dot(seq_rows(qb_dec[:, ksl[hh]], n), s_ret[hh][n])
        o_r.append(o_h)
        qk.append(_dot_nt(qb[:, ksl[hh]], kb_p[:, ksl[hh]]))

    a = jnp.zeros((rows_n, LANES), _F32)
    a = _gla_pair_scores(a, q, k, b, 0, lambda s: (s, (s // steps + 1) * steps))
    o_a = o_a + _dot(a, v)
    r_i = _iota((rows_n, LANES), 0)
    c_i = _iota((rows_n, LANES), 1)
    same_seq = (r_i >> (steps.bit_length() - 1)) == (c_i >> (steps.bit_length() - 1))
    tdiff = r_i - c_i
    for hh in range(2):
        decay = jnp.exp(jnp.where((tdiff >= 0) & same_seq, lg1[hh] * tdiff.astype(_F32), -jnp.inf))
        o_r[hh] = o_r[hh] + _dot(qk[hh] * decay, vb[:, vsl[hh]])

    def finish():
        b_last = [b[(n + 1) * steps - 1:(n + 1) * steps, :] for n in range(nseq)]
        b_end = jnp.zeros_like(b)
        for n in range(nseq):
            b_end = b_end + seq_rows(jnp.broadcast_to(b_last[n], b.shape), n)
        k_dec = pad_rows(k * jnp.exp2(b_end - b))
        decay_cols = _rows_to_cols([jnp.exp2(r) for r in b_last])
        for n in range(nseq):
            sg_ref[0, n, 0] = (decay_cols[:, n:n + 1] * s_gla[n]
                               + _dot_tn(seq_rows(k_dec, n, rows_p), v))
        kb_dec = pad_rows(kb * jnp.exp(lg * (steps - tpos)))
        for hh in range(2):
            for n in range(nseq):
                sr_ref[0, n, hh] = (jnp.exp(lg1[hh] * steps) * s_ret[hh][n]
                                    + _dot_tn(seq_rows(kb_dec[:, ksl[hh]], n, rows_p), vb[:, vsl[hh]]))
        o_an = _rms_heads(o_a, gnw_ref[...]) * za_ref[...]
        o_bn = jnp.concatenate([_group_norm(o_r[hh], rnw_ref[:, vsl[hh]], rnb_ref[:, vsl[hh]])
                                for hh in range(2)], axis=1) * zb_ref[...]
        merged = ga_ref[...] * o_an + gb_ref[...] * o_bn
        m_ref[...] = merged.astype(m_ref.dtype)

    return finish


def _sample_units(h, la, params, rope, lg, state_gla, state_ret, layer, nseq_total, steps, d_model):
    wq, wv = d_model // 8, d_model // 4
    dk_a, dv_a = wq, wv
    dk_b, dv_b = wq // 2, wv // 2
    nseq = SAMPLE_SEQS
    rows_n = nseq * steps
    assert steps & (steps - 1) == 0 and nseq_total % nseq == 0
    d = d_model

    def specs(unit_of):
        def at(fn):
            return lambda *ids: fn(*unit_of(*ids))

        def h_spec(width, col_off):
            base = col_off // width
            return pl.BlockSpec((rows_n, width), at(lambda p, g: (p, base + g)))
        in_specs = [h_spec(wq, 0), h_spec(wq, d // 2), h_spec(wv, d), h_spec(wv, 2 * d),
                    h_spec(wq, 3 * d), h_spec(wq, 3 * d + d // 2), h_spec(wv, 4 * d),
                    h_spec(wv, 5 * d), h_spec(wv, 6 * d), h_spec(wv, 7 * d)]
        in_specs += [
            pl.BlockSpec((rows_n, wq), at(lambda p, g: (p, g))),
            pl.BlockSpec((1, wv), at(lambda p, g: (0, 0))),
            pl.BlockSpec((1, wv), at(lambda p, g: (0, g))),
            pl.BlockSpec((1, wv), at(lambda p, g: (0, g))),
            pl.BlockSpec((rows_n, wq), at(lambda p, g: (0, 0))),
            pl.BlockSpec((rows_n, wq), at(lambda p, g: (0, 0))),
            pl.BlockSpec((1, 1, wq), at(lambda p, g: (g, 0, 0))),
            pl.BlockSpec((1, nseq, 1, dk_a, dv_a), at(lambda p, g: (layer, p, g, 0, 0))),
            pl.BlockSpec((1, nseq, 2, dk_b, dv_b), at(lambda p, g: (layer, p, g, 0, 0))),
        ]
        out_specs = [
            pl.BlockSpec((rows_n, wv), at(lambda p, g: (p, g))),
            pl.BlockSpec((1, nseq, 1, dk_a, dv_a), at(lambda p, g: (0, p, g, 0, 0))),
            pl.BlockSpec((1, nseq, 2, dk_b, dv_b), at(lambda p, g: (0, p, g, 0, 0))),
        ]
        out_shape = [
            jax.ShapeDtypeStruct((nseq_total * steps, d_model), _BF16),
            jax.ShapeDtypeStruct((1, nseq_total, GLA_HEADS, dk_a, dv_a), _F32),
            jax.ShapeDtypeStruct((1, nseq_total, RET_HEADS, dk_b, dv_b), _F32),
        ]
        return in_specs, out_specs, out_shape

    operands = [h] * 10 + [la, *params, *rope, lg, state_gla, state_ret]
    n_units = (nseq_total // nseq) * HEAD_GROUPS
    return operands, steps, n_units, specs


def _out_proj_ln(m, x, w, ln_w, ln_b, alpha):
    r = alpha * x + jnp.dot(m, w, preferred_element_type=_F32)
    mu = jnp.mean(r, axis=-1, keepdims=True)
    d = r - mu
    var = jnp.mean(d * d, axis=-1, keepdims=True)
    return d * lax.rsqrt(var + LN_EPS) * ln_w + ln_b


def _proj_out_kernel(m_ref, x_ref, w_ref, lnw_ref, lnb_ref, y_ref, *, alpha):
    y_ref[...] = _out_proj_ln(m_ref[...], x_ref[...], w_ref[...], lnw_ref[...], lnb_ref[...], alpha)


def _proj_out(merged, x2d, w_out, ln_w, ln_b, alpha, tm):
    m, d = x2d.shape
    e = merged.shape[1]
    return pl.pallas_call(
        functools.partial(_proj_out_kernel, alpha=alpha),
        grid=(m // tm,),
        in_specs=[
            pl.BlockSpec((tm, e), lambda i: (i, 0)),
            pl.BlockSpec((tm, d), lambda i: (i, 0)),
            pl.BlockSpec((e, d), lambda i: (0, 0)),
            pl.BlockSpec((1, d), lambda i: (0, 0)),
            pl.BlockSpec((1, d), lambda i: (0, 0)),
        ],
        out_specs=pl.BlockSpec((tm, d), lambda i: (i, 0)),
        out_shape=jax.ShapeDtypeStruct((m, d), _F32),
        compiler_params=pltpu.CompilerParams(
            dimension_semantics=("arbitrary",),
            vmem_limit_bytes=VMEM_LIMIT_BYTES),
        name="proj_out",
    )(merged, x2d, w_out, ln_w, ln_b)


def _rope_tables(pos, dk):
    inv = 1.0 / (ROPE_BASE ** jnp.linspace(0.0, 1.0, dk // 2, dtype=_F32))
    ang = pos.astype(_F32)[:, None] * inv[None, :]
    cos = jnp.repeat(jnp.cos(ang), 2, axis=1)
    sin = jnp.stack([-jnp.sin(ang), jnp.sin(ang)], axis=-1).reshape(ang.shape[0], dk)
    return jnp.tile(cos, (1, 2)), jnp.tile(sin, (1, 2))


def _pick_tile(n, pref):
    t = min(n, pref)
    while n % t:
        t //= 2
    return t


def kernel(x_prompt, x_sample, state_gla, state_ret, w_in, w_lr, b_lr, gla_norm_w,
           ret_norm_w, ret_norm_b, w_out, ln_w, ln_b):
    depth, d_model, _ = w_in.shape
    batch, seq, _ = x_prompt.shape
    dec_batch, dec_seq, _ = x_sample.shape
    rank = w_lr.shape[1]
    dk_b = d_model // 2 // RET_HEADS
    assert state_gla.shape[2] == GLA_HEADS and state_ret.shape[2] == RET_HEADS
    assert rank <= LANES and seq % PROMPT_CHUNK == 0
    alpha = (2.0 * depth) ** 0.25

    lg_heads = jnp.log(1.0 - 2.0 ** (-5.0 - jnp.arange(RET_HEADS, dtype=_F32)))
    lg = jnp.repeat(lg_heads, dk_b).reshape(HEAD_GROUPS, 1, 2 * dk_b)
    rope_p = _rope_tables(jnp.arange(seq, dtype=jnp.int32), dk_b)
    pos_s = PAST_LEN + jnp.arange(dec_seq, dtype=jnp.int32)
    rope_s = _rope_tables(jnp.tile(pos_s, SAMPLE_SEQS), dk_b)

    hp = x_prompt.reshape(batch * seq, d_model)
    hs = x_sample.reshape(dec_batch * dec_seq, d_model)
    w_t = jnp.swapaxes(w_in, 1, 2)
    gla_p, ret_p, gla_s, ret_s = [], [], [], []
    for l in range(depth):
        lr_lo = 3 * d_model
        w_lr_p = jnp.pad(w_lr[l], ((0, LANES - rank), (0, 0)))
        w_o = w_out[l].astype(_BF16)
        params = (gla_norm_w[l][None, :], ret_norm_w[l][None, :], ret_norm_b[l][None, :])
        lnw, lnb = ln_w[l][None, :], ln_b[l][None, :]

        xp_b, la_p = _prep_x(hp, w_t, l, lr_lo, rank, w_lr_p, b_lr[l][None, :],
                             _pick_tile(hp.shape[0], 512))
        xs_b, la_s = _prep_x(hs, w_t, l, lr_lo, rank, w_lr_p, b_lr[l][None, :],
                             _pick_tile(hs.shape[0], 512))
        (h_s,) = _proj_in(xs_b, w_t, l, rank, _pick_tile(xs_b.shape[0], 1024))
        units = _sample_units(h_s, la_s, params, rope_s, lg, state_gla, state_ret, l,
                              dec_batch, dec_seq, d_model)
        h_p, merged_s, sg_s, sr_s = _proj_in(xp_b, w_t, l, rank, _pick_tile(xp_b.shape[0], 1024),
                                             sample=units)

        merged, sg, sr = _mixer_prompt(h_p, la_p, params, rope_p, lg, batch, seq, d_model)
        hp = _proj_out(merged, hp, w_o, lnw, lnb, alpha, _pick_tile(hp.shape[0], 512))
        gla_p.append(sg)
        ret_p.append(sr)

        merged, sg, sr = merged_s, sg_s, sr_s
        hs = _proj_out(merged, hs, w_o, lnw, lnb, alpha, _pick_tile(hs.shape[0], 256))
        gla_s.append(sg)
        ret_s.append(sr)

    cat = lambda parts: parts[0] if len(parts) == 1 else jnp.concatenate(parts, axis=0)
    return (hp.reshape(batch, seq, d_model), hs.reshape(dec_batch, dec_seq, d_model),
            cat(gla_p), cat(ret_p), cat(gla_s), cat(ret_s))
```

```python
import functools

import jax
import jax.numpy as jnp
from jax import lax
from jax.experimental import pallas as pl
from jax.experimental.pallas import tpu as pltpu

GLA_HEADS = 4
RET_HEADS = 8
HEAD_GROUPS = 4
GLA_TAU = 16.0
LOG2_E = 1.4426950408889634
ROPE_BASE = 10000.0
LN_EPS = 1e-5
HEAD_NORM_EPS = 1e-6
PAST_LEN = 16384

LANES = 128
SUBLANES = 8
VMEM_LIMIT_BYTES = 56 * 1024 * 1024

PROMPT_CHUNK = 64
GLA_SUB = 16
SAMPLE_SEQS = 4

_F32 = jnp.float32
_BF16 = jnp.bfloat16
_NT = (((1,), (1,)), ((), ()))
_TN = (((0,), (0,)), ((), ()))


def _dot(a, b):
    return jnp.dot(a.astype(_BF16), b.astype(_BF16), preferred_element_type=_F32)


def _dot_nt(a, b):
    return lax.dot_general(a.astype(_BF16), b.astype(_BF16), _NT, preferred_element_type=_F32)


def _dot_tn(a, b):
    return lax.dot_general(a.astype(_BF16), b.astype(_BF16), _TN, preferred_element_type=_F32)


def _log_sigmoid(x):
    return jnp.minimum(x, 0.0) - jnp.log(1.0 + jnp.exp(-jnp.abs(x)))


def _silu(x):
    return x * jax.nn.sigmoid(x)


def _iota(shape, axis):
    return lax.broadcasted_iota(jnp.int32, shape, axis)


def _seg_cumsum(x, seg):
    pos = _iota(x.shape, 0) & (seg - 1)
    s = 1
    while s < seg:
        x = x + jnp.where(pos >= s, pltpu.roll(x, s, axis=0), 0.0)
        s *= 2
    return x


def _rows_to_cols(rows):
    n = rows[0].shape[1]
    rid = _iota((LANES, n), 0)
    stack = jnp.zeros((LANES, n), _F32)
    for i, r in enumerate(rows):
        stack = jnp.where(rid == i, r, stack)
    return stack.T


def _rope(x, cos, sin_signed):
    w = x.shape[1]
    even = (_iota(x.shape, 1) & 1) == 0
    swapped = jnp.where(even, pltpu.roll(x, w - 1, axis=1), pltpu.roll(x, 1, axis=1))
    return x * cos + swapped * sin_signed


def _gla_pair_scores(a, q, k, b2, lane0, row_lo_hi):
    n = q.shape[0]
    lanes = _iota((SUBLANES, a.shape[1]), 1)
    tiles = [a[r:r + SUBLANES] for r in range(0, n, SUBLANES)]
    for s in range(n):
        lo, hi = row_lo_hi(s)
        for j, r0 in enumerate(range(0, n, SUBLANES)):
            if r0 + SUBLANES <= lo or r0 >= hi:
                continue
            e = jnp.exp2(b2[r0:r0 + SUBLANES] - b2[s:s + 1, :])
            col = jnp.sum(q[r0:r0 + SUBLANES] * k[s:s + 1, :] * e, axis=-1, keepdims=True)
            take = lanes == lane0 + s
            if lo > r0 or hi < r0 + SUBLANES:
                rows = _iota((SUBLANES, 1), 0) + r0
                take = take & (rows >= lo) & (rows < hi)
            tiles[j] = jnp.where(take, col, tiles[j])
    return jnp.concatenate(tiles, axis=0)


def _rms_heads(o, w):
    return o * lax.rsqrt(jnp.mean(o * o, axis=-1, keepdims=True) + HEAD_NORM_EPS) * w


def _group_norm(o, w, b):
    mu = jnp.mean(o, axis=-1, keepdims=True)
    d = o - mu
    var = jnp.mean(d * d, axis=-1, keepdims=True)
    return d * lax.rsqrt(var + HEAD_NORM_EPS) * w + b


def _prep_x_kernel(x_ref, wcol_ref, wlr_ref, blr_ref, xb_ref, la_ref, *, rank):
    xb = x_ref[...].astype(_BF16)
    xb_ref[...] = xb
    wrow = wcol_ref[0]
    w_lra = jnp.concatenate([wrow, jnp.zeros((LANES - rank, wrow.shape[1]), _F32)], axis=0)
    lr = _dot_nt(xb, w_lra)
    w = wlr_ref[...]
    w_hi = w.astype(_BF16).astype(_F32)
    w_lo = w - w_hi
    lr_hi = lr.astype(_BF16).astype(_F32)
    lr_lo = lr - lr_hi
    lr_cat = lr_hi + pltpu.roll(lr_hi, rank, axis=1) + pltpu.roll(lr_lo, 2 * rank, axis=1)
    w_cat = jnp.concatenate([w_hi[:rank], w_lo[:rank], w_hi[:rank],
                             jnp.zeros((LANES - 3 * rank, w.shape[1]), _F32)], axis=0)
    logit = _dot(lr_cat, w_cat) + blr_ref[...]
    la_ref[...] = _log_sigmoid(logit) * (1.0 / GLA_TAU)


def _prep_x(x2d, w_t, layer, lr_col, rank, w_lr, b_lr, tm):
    m, kdim = x2d.shape
    qk = w_lr.shape[1]
    assert lr_col % rank == 0 and rank % SUBLANES == 0 and 3 * rank <= LANES
    whole = lambda a: pl.BlockSpec(a.shape, lambda i: (0,) * a.ndim)
    return pl.pallas_call(
        functools.partial(_prep_x_kernel, rank=rank),
        grid=(m // tm,),
        in_specs=[pl.BlockSpec((tm, kdim), lambda i: (i, 0)),
                  pl.BlockSpec((1, rank, kdim), lambda i: (layer, lr_col // rank, 0)),
                  whole(w_lr), whole(b_lr)],
        out_specs=[pl.BlockSpec((tm, kdim), lambda i: (i, 0)),
                   pl.BlockSpec((tm, qk), lambda i: (i, 0))],
        out_shape=[jax.ShapeDtypeStruct((m, kdim), _BF16),
                   jax.ShapeDtypeStruct((m, qk), _F32)],
        compiler_params=pltpu.CompilerParams(
            dimension_semantics=("arbitrary",), vmem_limit_bytes=VMEM_LIMIT_BYTES),
        name="prep_x",
    )(x2d, w_t, w_lr, b_lr)


N_SAMPLE_IN = 19


def _proj_in_kernel(*refs, n_plain, shift, silu_tiles, sigmoid_tiles, sample_steps):
    x_ref, wa_ref, we_ref = refs[:3]
    wbf_ref = refs[-1]
    if sample_steps:
        unit_in = refs[3:3 + N_SAMPLE_IN]
        h_ref = refs[3 + N_SAMPLE_IN]
        unit_out = refs[4 + N_SAMPLE_IN:7 + N_SAMPLE_IN]
    else:
        h_ref = refs[3]
    j = pl.program_id(0)
    i = pl.program_id(1)
    tn, kdim = wbf_ref.shape

    @pl.when(i == 0)
    def _():
        @pl.when(j < n_plain)
        def _():
            wbf_ref[...] = wa_ref[0].astype(_BF16)

        @pl.when(j >= n_plain)
        def _():
            wbf_ref[0:tn - shift, :] = wa_ref[0, shift:tn, :].astype(_BF16)
            wbf_ref[tn - shift:tn, :] = we_ref[0].astype(_BF16)

    is_silu = functools.reduce(jnp.logical_or, [j == t for t in silu_tiles])
    is_sigmoid = functools.reduce(jnp.logical_or, [j == t for t in sigmoid_tiles])

    def slab(cols, activated):
        acc = lax.dot_general(x_ref[...], wbf_ref[cols, :], _NT, preferred_element_type=_F32)
        if activated:
            s = 0.5 * jnp.tanh(0.5 * acc) + 0.5
            acc = jnp.where(is_sigmoid, s, acc * s)
        h_ref[:, cols] = acc

    def tile(activated):
        finish_unit = None
        if sample_steps:
            finish_unit = _mixer_sample_unit(*unit_in, *unit_out, steps=sample_steps)
        slab(slice(0, tn // 4), activated)
        if finish_unit is not None:
            finish_unit()
        slab(slice(tn // 4, tn), activated)

    @pl.when(is_silu | is_sigmoid)
    def _():
        tile(True)

    @pl.when(jnp.logical_not(is_silu | is_sigmoid))
    def _():
        tile(False)


PROJ_TILES = 16


def _proj_in(x, w_t, layer, rank, tm, sample=None):
    m, kdim = x.shape
    d_model = kdim
    tn = d_model // 2
    nj = PROJ_TILES
    n_plain = 6
    bf16_rows = 2 * SUBLANES
    assert rank % bf16_rows == 0 and tn % rank == 0 and m % tm == 0
    ni = m // tm
    in_specs = [
        pl.BlockSpec((tm, kdim), lambda j, i: (i, 0)),
        pl.BlockSpec((1, tn, kdim), lambda j, i: (layer, j, 0)),
        pl.BlockSpec((1, rank, kdim), lambda j, i: (layer, (j + 1) * (tn // rank), 0)),
    ]
    out_specs = [pl.BlockSpec((tm, tn), lambda j, i: (i, j))]
    out_shape = [jax.ShapeDtypeStruct((m, nj * tn), _F32)]
    operands = [x, w_t, w_t]
    sample_steps = 0
    if sample is not None:
        unit_operands, sample_steps, n_units, unit_specs = sample
        assert n_units <= nj * ni

        def unit_of(j, i):
            u = jnp.minimum(j * ni + i, n_units - 1)
            return u // HEAD_GROUPS, u % HEAD_GROUPS
        unit_in, unit_out, unit_shape = unit_specs(unit_of)
        assert len(unit_in) == N_SAMPLE_IN
        in_specs += unit_in
        out_specs += unit_out
        out_shape += unit_shape
        operands += unit_operands
    kern = functools.partial(_proj_in_kernel, n_plain=n_plain, shift=rank,
                             silu_tiles=(4, 5, 10, 11), sigmoid_tiles=(12, 13, 14, 15),
                             sample_steps=sample_steps)
    return pl.pallas_call(
        kern,
        grid=(nj, ni),
        in_specs=in_specs,
        out_specs=out_specs,
        out_shape=out_shape,
        scratch_shapes=[pltpu.VMEM((tn, kdim), _BF16)],
        compiler_params=pltpu.CompilerParams(
            dimension_semantics=("arbitrary", "arbitrary"),
            vmem_limit_bytes=VMEM_LIMIT_BYTES),
        name="proj_in_mix" if sample is not None else "proj_in",
    )(*operands)


def _h_cols(d_model, g):
    wq, wv = d_model // 8, d_model // 4
    d = d_model
    offs = dict(qa=(0, wq), ka=(d // 2, wq), va=(d, wv), za=(2 * d, wv),
                qb=(3 * d, wq), kb=(3 * d + d // 2, wq), vb=(4 * d, wv), zb=(5 * d, wv),
                ga=(6 * d, wv), gb=(7 * d, wv))
    return {n: slice(o + g * w, o + (g + 1) * w) for n, (o, w) in offs.items()}


def _mixer_prompt_kernel(h_ref, la_ref, gnw_ref, rnw_ref, rnb_ref,
                         cos_ref, sin_ref, lg_ref, wout_ref, m_ref, sg_ref, sr_ref, wout_bf_ref):
    wout_bf_ref[...] = wout_ref[0].astype(_BF16)
    chunk = h_ref.shape[0]
    d_model = m_ref.shape[1]
    dk_a, dv_a = sg_ref.shape[3:]
    dk_b, dv_b = sr_ref.shape[3:]

    @pl.when(pl.program_id(1) == 0)
    def _():
        sg_ref[...] = jnp.zeros(sg_ref.shape, _F32)
        sr_ref[...] = jnp.zeros(sr_ref.shape, _F32)

    rows = _iota((chunk, 1), 0)
    cos, sin = cos_ref[...], sin_ref[...]
    tpos = (_iota(cos.shape, 0) + 1).astype(_F32)
    tdiff = _iota((chunk, chunk), 0) - _iota((chunk, chunk), 1)
    n_sub = chunk // GLA_SUB
    groups = range(HEAD_GROUPS)
    cols = [_h_cols(d_model, g) for g in groups]
    ksl = [slice(hh * dk_b, (hh + 1) * dk_b) for hh in range(2)]
    vsl = [slice(hh * dv_b, (hh + 1) * dv_b) for hh in range(2)]


    gla, ret = [], []
    for g in groups:
        c = cols[g]
        b = _seg_cumsum(la_ref[:, g * dk_a:(g + 1) * dk_a], chunk) * LOG2_E
        q = h_ref[:, c["qa"]] * (dk_a ** -0.5)
        k = h_ref[:, c["ka"]]
        v = h_ref[:, c["va"]].astype(_BF16)
        s_gla = sg_ref[0, 0, g]
        gla.append(dict(b=b, q=q, k=k, v=v, s=s_gla, o=_dot(q * jnp.exp2(b), s_gla)))

        qb = _rope(h_ref[:, c["qb"]], cos, sin)
        kb = _rope(h_ref[:, c["kb"]], cos, sin) * (dk_b ** -0.5)
        vb = h_ref[:, c["vb"]].astype(_BF16)
        lg = lg_ref[g]
        q_dec = qb * jnp.exp(lg * tpos)
        s_ret = [sr_ref[0, 0, 2 * g + hh] for hh in range(2)]
        ret.append(dict(
            vb=vb, lg=lg, s=s_ret, k_dec=kb * jnp.exp(lg * (chunk - tpos)),
            o=[_dot(q_dec[:, ksl[hh]], s_ret[hh]) for hh in range(2)],
            qk=[_dot_nt(qb[:, ksl[hh]], kb[:, ksl[hh]]) for hh in range(2)]))

    for g in groups:
        q, k, b = gla[g]["q"], gla[g]["k"], gla[g]["b"]
        a_off = [jnp.zeros((GLA_SUB, chunk), _F32)]
        for i in range(1, n_sub):
            lo = i * GLA_SUB
            r = b[lo - 1:lo, :]
            k_pre = k * jnp.exp2(jnp.where(rows < lo, r - b, -jnp.inf))
            a_off.append(_dot_nt(q[lo:lo + GLA_SUB] * jnp.exp2(b[lo:lo + GLA_SUB] - r), k_pre))
        gla[g]["a"] = a_off

    for g in groups:
        q, k, b, v = gla[g]["q"], gla[g]["k"], gla[g]["b"], gla[g]["v"]
        intra = []
        for i in range(n_sub):
            lo = i * GLA_SUB
            sub = slice(lo, lo + GLA_SUB)
            a = _gla_pair_scores(gla[g]["a"][i], q[sub], k[sub], b[sub], lo, lambda s: (s, GLA_SUB))
            intra.append(_dot(a, v))
        gla[g]["o"] = gla[g]["o"] + jnp.concatenate(intra, axis=0)
        for hh in range(2):
            lg1 = ret[g]["lg"][:, hh * dk_b:hh * dk_b + 1]
            decay = jnp.exp(jnp.where(tdiff >= 0, lg1 * tdiff.astype(_F32), -jnp.inf))
            ret[g]["o"][hh] = ret[g]["o"][hh] + _dot(ret[g]["qk"][hh] * decay, ret[g]["vb"][:, vsl[hh]])

    for g in groups:
        k, b, v = gla[g]["k"], gla[g]["b"], gla[g]["v"]
        b_last = b[chunk - 1:chunk, :]
        decay_col = _rows_to_cols([jnp.exp2(b_last)])[:, 0:1]
        sg_ref[0, 0, g] = decay_col * gla[g]["s"] + _dot_tn(k * jnp.exp2(b_last - b), v)
        for hh in range(2):
            lg1 = ret[g]["lg"][:, hh * dk_b:hh * dk_b + 1]
            sr_ref[0, 0, 2 * g + hh] = (jnp.exp(lg1 * chunk) * ret[g]["s"][hh]
                                        + _dot_tn(ret[g]["k_dec"][:, ksl[hh]], ret[g]["vb"][:, vsl[hh]]))

    for g in groups:
        c = cols[g]
        gv = slice(g * dv_a, (g + 1) * dv_a)
        o_a = _rms_heads(gla[g]["o"], gnw_ref[...]) * h_ref[:, c["za"]]
        o_b = []
        for hh in range(2):
            nsl = slice(g * dv_a + hh * dv_b, g * dv_a + (hh + 1) * dv_b)
            o_b.append(_group_norm(ret[g]["o"][hh], rnw_ref[:, nsl], rnb_ref[:, nsl]))
        o_b = jnp.concatenate(o_b, axis=1) * h_ref[:, c["zb"]]
        merged = h_ref[:, c["ga"]] * o_a + h_ref[:, c["gb"]] * o_b
        m_ref[:, gv] = merged.astype(m_ref.dtype)


def _mixer_prompt(h, la, params, rope, lg, w_out, layer, batch, seq, d_model):
    wq, wv = d_model // 8, d_model // 4
    dk_a, dv_a = wq, wv
    dk_b, dv_b = wq // 2, wv // 2
    chunk = PROMPT_CHUNK
    nc = seq // chunk
    e_rows = w_out.shape[1]
    bf16_rows = 2 * SUBLANES
    assert e_rows % (batch * nc) == 0 and (e_rows // (batch * nc)) % bf16_rows == 0
    w_rows = e_rows // (batch * nc)
    whole = lambda a: pl.BlockSpec(a.shape, lambda b, c: (0,) * a.ndim)
    in_specs = [pl.BlockSpec((chunk, h.shape[1]), lambda b, c: (b * nc + c, 0)),
                pl.BlockSpec((chunk, la.shape[1]), lambda b, c: (b * nc + c, 0))]
    in_specs += [whole(p) for p in params]
    in_specs += [pl.BlockSpec((chunk, wq), lambda b, c: (c, 0)),
                 pl.BlockSpec((chunk, wq), lambda b, c: (c, 0)),
                 whole(lg),
                 pl.BlockSpec((1, w_rows, w_out.shape[2]), lambda b, c: (layer, b * nc + c, 0))]
    out_specs = [
        pl.BlockSpec((chunk, d_model), lambda b, c: (b * nc + c, 0)),
        pl.BlockSpec((1, 1, GLA_HEADS, dk_a, dv_a), lambda b, c: (0, b, 0, 0, 0)),
        pl.BlockSpec((1, 1, RET_HEADS, dk_b, dv_b), lambda b, c: (0, b, 0, 0, 0)),
        pl.BlockSpec((w_rows, w_out.shape[2]), lambda b, c: (b * nc + c, 0)),
    ]
    out_shape = [
        jax.ShapeDtypeStruct((batch * seq, d_model), _BF16),
        jax.ShapeDtypeStruct((1, batch, GLA_HEADS, dk_a, dv_a), _F32),
        jax.ShapeDtypeStruct((1, batch, RET_HEADS, dk_b, dv_b), _F32),
        jax.ShapeDtypeStruct(w_out.shape[1:], _BF16),
    ]
    return pl.pallas_call(
        _mixer_prompt_kernel,
        grid=(batch, nc),
        in_specs=in_specs,
        out_specs=out_specs,
        out_shape=out_shape,
        compiler_params=pltpu.CompilerParams(
            dimension_semantics=("arbitrary", "arbitrary"),
            vmem_limit_bytes=VMEM_LIMIT_BYTES),
        name="mixer_prompt",
    )(h, la, *params, *rope, lg, w_out)


def _mixer_sample_unit(qa_ref, ka_ref, va_ref, za_ref, qb_ref, kb_ref, vb_ref, zb_ref,
                       ga_ref, gb_ref, la_ref, gnw_ref, rnw_ref, rnb_ref,
                       cos_ref, sin_ref, lg_ref, sg0_ref, sr0_ref, m_ref, sg_ref, sr_ref,
                       *, steps):
    rows_n, dk_a = qa_ref.shape
    nseq = rows_n // steps
    dk_b = qb_ref.shape[1] // 2
    dv_b = vb_ref.shape[1] // 2
    rows = _iota((rows_n, 1), 0)
    rows_p = _iota((LANES, 1), 0)
    pad = LANES - rows_n
    ksl = [slice(hh * dk_b, (hh + 1) * dk_b) for hh in range(2)]
    vsl = [slice(hh * dv_b, (hh + 1) * dv_b) for hh in range(2)]

    def pad_rows(x):
        return jnp.concatenate([x, jnp.zeros((pad, x.shape[1]), x.dtype)], axis=0)

    def seq_rows(x, n, r=rows):
        return jnp.where((r >= n * steps) & (r < (n + 1) * steps), x, 0.0)

    b = _seg_cumsum(la_ref[...], steps) * LOG2_E
    q = qa_ref[...] * (dk_a ** -0.5)
    k = ka_ref[...]
    v = pad_rows(va_ref[...]).astype(_BF16)
    q_dec = q * jnp.exp2(b)
    s_gla = [sg0_ref[0, n, 0] for n in range(nseq)]
    o_a = _dot(seq_rows(q_dec, 0), s_gla[0])
    for n in range(1, nseq):
        o_a = o_a + _dot(seq_rows(q_dec, n), s_gla[n])

    cos, sin = cos_ref[...], sin_ref[...]
    qb = _rope(qb_ref[...], cos, sin)
    kb = _rope(kb_ref[...], cos, sin) * (dk_b ** -0.5)
    vb = pad_rows(vb_ref[...]).astype(_BF16)
    lg = lg_ref[0]
    lg1 = [lg[:, hh * dk_b:hh * dk_b + 1] for hh in range(2)]
    tpos = ((_iota(qb.shape, 0) & (steps - 1)) + 1).astype(_F32)
    qb_dec = qb * jnp.exp(lg * tpos)
    kb_p = pad_rows(kb)
    s_ret = [[sr0_ref[0, n, hh] for n in range(nseq)] for hh in range(2)]
    o_r, qk = [], []
    for hh in range(2):
        o_h = _dot(seq_rows(qb_dec[:, ksl[hh]], 0), s_ret[hh][0])
        for n in range(1, nseq):
            o_h = o_h + _dot(seq_rows(qb_dec[:, ksl[hh]], n), s_ret[hh][n])
        o_r.append(o_h)
        qk.append(_dot_nt(qb[:, ksl[hh]], kb_p[:, ksl[hh]]))

    a = jnp.zeros((rows_n, LANES), _F32)
    a = _gla_pair_scores(a, q, k, b, 0, lambda s: (s, (s // steps + 1) * steps))
    o_a = o_a + _dot(a, v)
    r_i = _iota((rows_n, LANES), 0)
    c_i = _iota((rows_n, LANES), 1)
    same_seq = (r_i >> (steps.bit_length() - 1)) == (c_i >> (steps.bit_length() - 1))
    tdiff = r_i - c_i
    for hh in range(2):
        decay = jnp.exp(jnp.where((tdiff >= 0) & same_seq, lg1[hh] * tdiff.astype(_F32), -jnp.inf))
        o_r[hh] = o_r[hh] + _dot(qk[hh] * decay, vb[:, vsl[hh]])

    def finish():
        b_last = [b[(n + 1) * steps - 1:(n + 1) * steps, :] for n in range(nseq)]
        b_end = jnp.zeros_like(b)
        for n in range(nseq):
            b_end = b_end + seq_rows(jnp.broadcast_to(b_last[n], b.shape), n)
        k_dec = pad_rows(k * jnp.exp2(b_end - b))
        decay_cols = _rows_to_cols([jnp.exp2(r) for r in b_last])
        for n in range(nseq):
            sg_ref[0, n, 0] = (decay_cols[:, n:n + 1] * s_gla[n]
                               + _dot_tn(seq_rows(k_dec, n, rows_p), v))
        kb_dec = pad_rows(kb * jnp.exp(lg * (steps - tpos)))
        for hh in range(2):
            for n in range(nseq):
                sr_ref[0, n, hh] = (jnp.exp(lg1[hh] * steps) * s_ret[hh][n]
                                    + _dot_tn(seq_rows(kb_dec[:, ksl[hh]], n, rows_p), vb[:, vsl[hh]]))
        o_an = _rms_heads(o_a, gnw_ref[...]) * za_ref[...]
        o_bn = jnp.concatenate([_group_norm(o_r[hh], rnw_ref[:, vsl[hh]], rnb_ref[:, vsl[hh]])
                                for hh in range(2)], axis=1) * zb_ref[...]
        merged = ga_ref[...] * o_an + gb_ref[...] * o_bn
        m_ref[...] = merged.astype(m_ref.dtype)

    return finish


def _sample_units(h, la, params, rope, lg, state_gla, state_ret, layer, nseq_total, steps, d_model):
    wq, wv = d_model // 8, d_model // 4
    dk_a, dv_a = wq, wv
    dk_b, dv_b = wq // 2, wv // 2
    nseq = SAMPLE_SEQS
    rows_n = nseq * steps
    assert steps & (steps - 1) == 0 and nseq_total % nseq == 0
    d = d_model

    def specs(unit_of):
        def at(fn):
            return lambda *ids: fn(*unit_of(*ids))

        def h_spec(width, col_off):
            base = col_off // width
            return pl.BlockSpec((rows_n, width), at(lambda p, g: (p, base + g)))
        in_specs = [h_spec(wq, 0), h_spec(wq, d // 2), h_spec(wv, d), h_spec(wv, 2 * d),
                    h_spec(wq, 3 * d), h_spec(wq, 3 * d + d // 2), h_spec(wv, 4 * d),
                    h_spec(wv, 5 * d), h_spec(wv, 6 * d), h_spec(wv, 7 * d)]
        in_specs += [
            pl.BlockSpec((rows_n, wq), at(lambda p, g: (p, g))),
            pl.BlockSpec((1, wv), at(lambda p, g: (0, 0))),
            pl.BlockSpec((1, wv), at(lambda p, g: (0, g))),
            pl.BlockSpec((1, wv), at(lambda p, g: (0, g))),
            pl.BlockSpec((rows_n, wq), at(lambda p, g: (0, 0))),
            pl.BlockSpec((rows_n, wq), at(lambda p, g: (0, 0))),
            pl.BlockSpec((1, 1, wq), at(lambda p, g: (g, 0, 0))),
            pl.BlockSpec((1, nseq, 1, dk_a, dv_a), at(lambda p, g: (layer, p, g, 0, 0))),
            pl.BlockSpec((1, nseq, 2, dk_b, dv_b), at(lambda p, g: (layer, p, g, 0, 0))),
        ]
        out_specs = [
            pl.BlockSpec((rows_n, wv), at(lambda p, g: (p, g))),
            pl.BlockSpec((1, nseq, 1, dk_a, dv_a), at(lambda p, g: (0, p, g, 0, 0))),
            pl.BlockSpec((1, nseq, 2, dk_b, dv_b), at(lambda p, g: (0, p, g, 0, 0))),
        ]
        out_shape = [
            jax.ShapeDtypeStruct((nseq_total * steps, d_model), _BF16),
            jax.ShapeDtypeStruct((1, nseq_total, GLA_HEADS, dk_a, dv_a), _F32),
            jax.ShapeDtypeStruct((1, nseq_total, RET_HEADS, dk_b, dv_b), _F32),
        ]
        return in_specs, out_specs, out_shape

    operands = [h] * 10 + [la, *params, *rope, lg, state_gla, state_ret]
    n_units = (nseq_total // nseq) * HEAD_GROUPS
    return operands, steps, n_units, specs


PROJ_OUT_SLABS = 4

def _out_proj_ln(m, x, w, ln_w, ln_b, alpha):
    r = alpha * x + jnp.dot(m, w, preferred_element_type=_F32)
    mu = jnp.mean(r, axis=-1, keepdims=True)
    d = r - mu
    var = jnp.mean(d * d, axis=-1, keepdims=True)
    return d * lax.rsqrt(var + LN_EPS) * ln_w + ln_b


def _proj_out_kernel(m_ref, x_ref, w_ref, lnw_ref, lnb_ref, y_ref, *, alpha):
    tm = m_ref.shape[0]
    slab = min(tm, max(tm // PROJ_OUT_SLABS, LANES))
    for r in range(0, tm, slab):
        y_ref[r:r + slab, :] = _out_proj_ln(m_ref[r:r + slab, :], x_ref[r:r + slab, :], w_ref[...],
                                            lnw_ref[...], lnb_ref[...], alpha)


def _proj_out(merged, x2d, w_out, ln_w, ln_b, alpha, tm):
    m, d = x2d.shape
    e = merged.shape[1]
    return pl.pallas_call(
        functools.partial(_proj_out_kernel, alpha=alpha),
        grid=(m // tm,),
        in_specs=[
            pl.BlockSpec((tm, e), lambda i: (i, 0)),
            pl.BlockSpec((tm, d), lambda i: (i, 0)),
            pl.BlockSpec((e, d), lambda i: (0, 0)),
            pl.BlockSpec((1, d), lambda i: (0, 0)),
            pl.BlockSpec((1, d), lambda i: (0, 0)),
        ],
        out_specs=pl.BlockSpec((tm, d), lambda i: (i, 0)),
        out_shape=jax.ShapeDtypeStruct((m, d), _F32),
        compiler_params=pltpu.CompilerParams(
            dimension_semantics=("arbitrary",),
            vmem_limit_bytes=VMEM_LIMIT_BYTES),
        name="proj_out",
    )(merged, x2d, w_out, ln_w, ln_b)


def _rope_tables(pos, dk):
    inv = 1.0 / (ROPE_BASE ** jnp.linspace(0.0, 1.0, dk // 2, dtype=_F32))
    ang = pos.astype(_F32)[:, None] * inv[None, :]
    cos = jnp.repeat(jnp.cos(ang), 2, axis=1)
    sin = jnp.stack([-jnp.sin(ang), jnp.sin(ang)], axis=-1).reshape(ang.shape[0], dk)
    return jnp.tile(cos, (1, 2)), jnp.tile(sin, (1, 2))


def _pick_tile(n, pref):
    t = min(n, pref)
    while n % t:
        t //= 2
    return t


def kernel(x_prompt, x_sample, state_gla, state_ret, w_in, w_lr, b_lr, gla_norm_w,
           ret_norm_w, ret_norm_b, w_out, ln_w, ln_b):
    depth, d_model, _ = w_in.shape
    batch, seq, _ = x_prompt.shape
    dec_batch, dec_seq, _ = x_sample.shape
    rank = w_lr.shape[1]
    dk_b = d_model // 2 // RET_HEADS
    assert state_gla.shape[2] == GLA_HEADS and state_ret.shape[2] == RET_HEADS
    assert rank <= LANES and seq % PROMPT_CHUNK == 0
    alpha = (2.0 * depth) ** 0.25

    lg_heads = jnp.log(1.0 - 2.0 ** (-5.0 - jnp.arange(RET_HEADS, dtype=_F32)))
    lg = jnp.repeat(lg_heads, dk_b).reshape(HEAD_GROUPS, 1, 2 * dk_b)
    rope_p = _rope_tables(jnp.arange(seq, dtype=jnp.int32), dk_b)
    pos_s = PAST_LEN + jnp.arange(dec_seq, dtype=jnp.int32)
    rope_s = _rope_tables(jnp.tile(pos_s, SAMPLE_SEQS), dk_b)

    hp = x_prompt.reshape(batch * seq, d_model)
    hs = x_sample.reshape(dec_batch * dec_seq, d_model)
    w_t = jnp.swapaxes(w_in, 1, 2)
    gla_p, ret_p, gla_s, ret_s = [], [], [], []
    for l in range(depth):
        lr_lo = 3 * d_model
        w_lr_p = jnp.pad(w_lr[l], ((0, LANES - rank), (0, 0)))
        params = (gla_norm_w[l][None, :], ret_norm_w[l][None, :], ret_norm_b[l][None, :])
        lnw, lnb = ln_w[l][None, :], ln_b[l][None, :]

        xp_b, la_p = _prep_x(hp, w_t, l, lr_lo, rank, w_lr_p, b_lr[l][None, :],
                             _pick_tile(hp.shape[0], 512))
        xs_b, la_s = _prep_x(hs, w_t, l, lr_lo, rank, w_lr_p, b_lr[l][None, :],
                             _pick_tile(hs.shape[0], 512))
        (h_s,) = _proj_in(xs_b, w_t, l, rank, _pick_tile(xs_b.shape[0], 1024))
        units = _sample_units(h_s, la_s, params, rope_s, lg, state_gla, state_ret, l,
                              dec_batch, dec_seq, d_model)
        h_p, merged_s, sg_s, sr_s = _proj_in(xp_b, w_t, l, rank, _pick_tile(xp_b.shape[0], 1024),
                                             sample=units)

        merged, sg, sr, w_o = _mixer_prompt(h_p, la_p, params, rope_p, lg, w_out, l,
                                            batch, seq, d_model)
        hp = _proj_out(merged, hp, w_o, lnw, lnb, alpha, _pick_tile(hp.shape[0], 512))
        gla_p.append(sg)
        ret_p.append(sr)

        merged, sg, sr = merged_s, sg_s, sr_s
        hs = _proj_out(merged, hs, w_o, lnw, lnb, alpha, _pick_tile(hs.shape[0], 256))
        gla_s.append(sg)
        ret_s.append(sr)

    cat = lambda parts: parts[0] if len(parts) == 1 else jnp.concatenate(parts, axis=0)
    return (hp.reshape(batch, seq, d_model), hs.reshape(dec_batch, dec_seq, d_model),
            cat(gla_p), cat(ret_p), cat(gla_s), cat(ret_s))
```

```python
import functools

import jax
import jax.numpy as jnp
from jax import lax
from jax.experimental import pallas as pl
from jax.experimental.pallas import tpu as pltpu

GLA_HEADS = 4
RET_HEADS = 8
HEAD_GROUPS = 4
GLA_TAU = 16.0
LOG2_E = 1.4426950408889634
ROPE_BASE = 10000.0
LN_EPS = 1e-5
HEAD_NORM_EPS = 1e-6
PAST_LEN = 16384

LANES = 128
SUBLANES = 8
VMEM_LIMIT_BYTES = 56 * 1024 * 1024

PROMPT_CHUNK = 64
GLA_SUB = 16
SAMPLE_SEQS = 4

_F32 = jnp.float32
_BF16 = jnp.bfloat16
_NT = (((1,), (1,)), ((), ()))
_TN = (((0,), (0,)), ((), ()))


def _dot(a, b):
    return jnp.dot(a.astype(_BF16), b.astype(_BF16), preferred_element_type=_F32)


def _dot_nt(a, b):
    return lax.dot_general(a.astype(_BF16), b.astype(_BF16), _NT, preferred_element_type=_F32)


def _dot_tn(a, b):
    return lax.dot_general(a.astype(_BF16), b.astype(_BF16), _TN, preferred_element_type=_F32)


def _log_sigmoid(x):
    return jnp.minimum(x, 0.0) - jnp.log(1.0 + jnp.exp(-jnp.abs(x)))


def _silu(x):
    return x * jax.nn.sigmoid(x)


def _iota(shape, axis):
    return lax.broadcasted_iota(jnp.int32, shape, axis)


def _seg_cumsum(x, seg):
    pos = _iota(x.shape, 0) & (seg - 1)
    s = 1
    while s < seg:
        x = x + jnp.where(pos >= s, pltpu.roll(x, s, axis=0), 0.0)
        s *= 2
    return x


def _cumsum_rows_mxu(x):
    n = x.shape[0]
    hi = x.astype(_BF16)
    r1 = x - hi.astype(_F32)
    mid = r1.astype(_BF16)
    lo = (r1 - mid.astype(_F32)).astype(_BF16)
    tri3 = (_iota((n, 3 * n), 0) >= (_iota((n, 3 * n), 1) & (n - 1))).astype(_BF16)
    return jnp.dot(tri3, jnp.concatenate([hi, mid, lo], axis=0), preferred_element_type=_F32)


def _rows_to_cols(rows):
    n = rows[0].shape[1]
    rid = _iota((LANES, n), 0)
    stack = jnp.zeros((LANES, n), _F32)
    for i, r in enumerate(rows):
        stack = jnp.where(rid == i, r, stack)
    return stack.T


def _rope(x, cos, sin_signed):
    w = x.shape[1]
    even = (_iota(x.shape, 1) & 1) == 0
    swapped = jnp.where(even, pltpu.roll(x, w - 1, axis=1), pltpu.roll(x, 1, axis=1))
    return x * cos + swapped * sin_signed


def _gla_pair_scores(a, q, k, b2, lane0, row_lo_hi):
    n = q.shape[0]
    lanes = _iota((SUBLANES, a.shape[1]), 1)
    tiles = [a[r:r + SUBLANES] for r in range(0, n, SUBLANES)]
    for s in range(n):
        lo, hi = row_lo_hi(s)
        for j, r0 in enumerate(range(0, n, SUBLANES)):
            if r0 + SUBLANES <= lo or r0 >= hi:
                continue
            e = jnp.exp2(b2[r0:r0 + SUBLANES] - b2[s:s + 1, :])
            col = jnp.sum(q[r0:r0 + SUBLANES] * k[s:s + 1, :] * e, axis=-1, keepdims=True)
            take = lanes == lane0 + s
            if lo > r0 or hi < r0 + SUBLANES:
                rows = _iota((SUBLANES, 1), 0) + r0
                take = take & (rows >= lo) & (rows < hi)
            tiles[j] = jnp.where(take, col, tiles[j])
    return jnp.concatenate(tiles, axis=0)


def _rms_heads(o, w):
    return o * lax.rsqrt(jnp.mean(o * o, axis=-1, keepdims=True) + HEAD_NORM_EPS) * w


def _group_norm(o, w, b):
    mu = jnp.mean(o, axis=-1, keepdims=True)
    d = o - mu
    var = jnp.mean(d * d, axis=-1, keepdims=True)
    return d * lax.rsqrt(var + HEAD_NORM_EPS) * w + b


def _prep_x_kernel(x_ref, wcol_ref, wlr_ref, blr_ref, xb_ref, la_ref, *, rank):
    xb = x_ref[...].astype(_BF16)
    xb_ref[...] = xb
    wrow = wcol_ref[0]
    w_lra = jnp.concatenate([wrow, jnp.zeros((LANES - rank, wrow.shape[1]), _F32)], axis=0)
    lr = _dot_nt(xb, w_lra)
    w = wlr_ref[...]
    w_hi = w.astype(_BF16).astype(_F32)
    w_lo = w - w_hi
    lr_hi = lr.astype(_BF16).astype(_F32)
    lr_lo = lr - lr_hi
    lr_cat = lr_hi + pltpu.roll(lr_hi, rank, axis=1) + pltpu.roll(lr_lo, 2 * rank, axis=1)
    w_cat = jnp.concatenate([w_hi[:rank], w_lo[:rank], w_hi[:rank],
                             jnp.zeros((LANES - 3 * rank, w.shape[1]), _F32)], axis=0)
    logit = _dot(lr_cat, w_cat) + blr_ref[...]
    la_ref[...] = _log_sigmoid(logit) * (1.0 / GLA_TAU)


def _prep_x(x2d, w_t, layer, lr_col, rank, w_lr, b_lr, tm):
    m, kdim = x2d.shape
    qk = w_lr.shape[1]
    assert lr_col % rank == 0 and rank % SUBLANES == 0 and 3 * rank <= LANES
    whole = lambda a: pl.BlockSpec(a.shape, lambda i: (0,) * a.ndim)
    return pl.pallas_call(
        functools.partial(_prep_x_kernel, rank=rank),
        grid=(m // tm,),
        in_specs=[pl.BlockSpec((tm, kdim), lambda i: (i, 0)),
                  pl.BlockSpec((1, rank, kdim), lambda i: (layer, lr_col // rank, 0)),
                  whole(w_lr), whole(b_lr)],
        out_specs=[pl.BlockSpec((tm, kdim), lambda i: (i, 0)),
                   pl.BlockSpec((tm, qk), lambda i: (i, 0))],
        out_shape=[jax.ShapeDtypeStruct((m, kdim), _BF16),
                   jax.ShapeDtypeStruct((m, qk), _F32)],
        compiler_params=pltpu.CompilerParams(
            dimension_semantics=("arbitrary",), vmem_limit_bytes=VMEM_LIMIT_BYTES),
        name="prep_x",
    )(x2d, w_t, w_lr, b_lr)


N_SAMPLE_IN = 19


def _proj_in_kernel(*refs, n_plain, shift, silu_tiles, sigmoid_tiles, sample_steps):
    x_ref, wa_ref, we_ref = refs[:3]
    wbf_ref = refs[-1]
    if sample_steps:
        unit_in = refs[3:3 + N_SAMPLE_IN]
        h_ref = refs[3 + N_SAMPLE_IN]
        unit_out = refs[4 + N_SAMPLE_IN:7 + N_SAMPLE_IN]
    else:
        h_ref = refs[3]
    j = pl.program_id(0)
    i = pl.program_id(1)
    tn, kdim = wbf_ref.shape

    @pl.when(i == 0)
    def _():
        @pl.when(j < n_plain)
        def _():
            wbf_ref[...] = wa_ref[0].astype(_BF16)

        @pl.when(j >= n_plain)
        def _():
            wbf_ref[0:tn - shift, :] = wa_ref[0, shift:tn, :].astype(_BF16)
            wbf_ref[tn - shift:tn, :] = we_ref[0].astype(_BF16)

    is_silu = functools.reduce(jnp.logical_or, [j == t for t in silu_tiles])
    is_sigmoid = functools.reduce(jnp.logical_or, [j == t for t in sigmoid_tiles])

    def slab(cols, activated):
        acc = lax.dot_general(x_ref[...], wbf_ref[cols, :], _NT, preferred_element_type=_F32)
        if activated:
            s = 0.5 * jnp.tanh(0.5 * acc) + 0.5
            acc = jnp.where(is_sigmoid, s, acc * s)
        h_ref[:, cols] = acc

    def tile(activated):
        finish_unit = None
        if sample_steps:
            finish_unit = _mixer_sample_unit(*unit_in, *unit_out, steps=sample_steps)
        slab(slice(0, tn // 4), activated)
        if finish_unit is not None:
            finish_unit()
        slab(slice(tn // 4, tn), activated)

    @pl.when(is_silu | is_sigmoid)
    def _():
        tile(True)

    @pl.when(jnp.logical_not(is_silu | is_sigmoid))
    def _():
        tile(False)


PROJ_TILES = 16


def _proj_in(x, w_t, layer, rank, tm, sample=None):
    m, kdim = x.shape
    d_model = kdim
    tn = d_model // 2
    nj = PROJ_TILES
    n_plain = 6
    bf16_rows = 2 * SUBLANES
    assert rank % bf16_rows == 0 and tn % rank == 0 and m % tm == 0
    ni = m // tm

    def row_block(j, i):
        return jnp.where(j % 2 == 0, i, ni - 1 - i)
    in_specs = [
        pl.BlockSpec((tm, kdim), lambda j, i: (row_block(j, i), 0)),
        pl.BlockSpec((1, tn, kdim), lambda j, i: (layer, j, 0)),
        pl.BlockSpec((1, rank, kdim), lambda j, i: (layer, (j + 1) * (tn // rank), 0)),
    ]
    out_specs = [pl.BlockSpec((tm, tn), lambda j, i: (row_block(j, i), j))]
    out_shape = [jax.ShapeDtypeStruct((m, nj * tn), _F32)]
    operands = [x, w_t, w_t]
    sample_steps = 0
    if sample is not None:
        unit_operands, sample_steps, n_units, unit_specs = sample
        assert n_units <= nj * ni

        def unit_of(j, i):
            u = jnp.minimum(j * ni + i, n_units - 1)
            return u // HEAD_GROUPS, u % HEAD_GROUPS
        unit_in, unit_out, unit_shape = unit_specs(unit_of)
        assert len(unit_in) == N_SAMPLE_IN
        in_specs += unit_in
        out_specs += unit_out
        out_shape += unit_shape
        operands += unit_operands
    kern = functools.partial(_proj_in_kernel, n_plain=n_plain, shift=rank,
                             silu_tiles=(4, 5, 10, 11), sigmoid_tiles=(12, 13, 14, 15),
                             sample_steps=sample_steps)
    return pl.pallas_call(
        kern,
        grid=(nj, ni),
        in_specs=in_specs,
        out_specs=out_specs,
        out_shape=out_shape,
        scratch_shapes=[pltpu.VMEM((tn, kdim), _BF16)],
        compiler_params=pltpu.CompilerParams(
            dimension_semantics=("arbitrary", "arbitrary"),
            vmem_limit_bytes=VMEM_LIMIT_BYTES),
        name="proj_in_mix" if sample is not None else "proj_in",
    )(*operands)


def _h_cols(d_model, g):
    wq, wv = d_model // 8, d_model // 4
    d = d_model
    offs = dict(qa=(0, wq), ka=(d // 2, wq), va=(d, wv), za=(2 * d, wv),
                qb=(3 * d, wq), kb=(3 * d + d // 2, wq), vb=(4 * d, wv), zb=(5 * d, wv),
                ga=(6 * d, wv), gb=(7 * d, wv))
    return {n: slice(o + g * w, o + (g + 1) * w) for n, (o, w) in offs.items()}


def _mixer_prompt_kernel(h_ref, la_ref, gnw_ref, rnw_ref, rnb_ref,
                         cos_ref, sin_ref, lg_ref, wout_ref, m_ref, sg_ref, sr_ref, wout_bf_ref,
                         qdec_ref, kdec_ref, dmat_ref):
    wout_bf_ref[...] = wout_ref[0].astype(_BF16)
    chunk = h_ref.shape[0]
    d_model = m_ref.shape[1]
    dk_a, dv_a = sg_ref.shape[3:]
    dk_b, dv_b = sr_ref.shape[3:]

    @pl.when(pl.program_id(1) == 0)
    def _():
        sg_ref[...] = jnp.zeros(sg_ref.shape, _F32)
        sr_ref[...] = jnp.zeros(sr_ref.shape, _F32)

    @pl.when((pl.program_id(0) == 0) & (pl.program_id(1) == 0))
    def _():
        tpos = (_iota((chunk, 2 * dk_b), 0) + 1).astype(_F32)
        tdiff = _iota((chunk, chunk), 0) - _iota((chunk, chunk), 1)
        for g in range(HEAD_GROUPS):
            lg = lg_ref[g]
            qdec_ref[g] = jnp.exp(lg * tpos)
            kdec_ref[g] = jnp.exp(lg * (chunk - tpos))
            for hh in range(2):
                lg1 = lg[:, hh * dk_b:hh * dk_b + 1]
                dmat_ref[2 * g + hh] = jnp.exp(
                    jnp.where(tdiff >= 0, lg1 * tdiff.astype(_F32), -jnp.inf))

    rows = _iota((chunk, 1), 0)
    cos, sin = cos_ref[...], sin_ref[...]
    n_sub = chunk // GLA_SUB
    groups = range(HEAD_GROUPS)
    cols = [_h_cols(d_model, g) for g in groups]
    ksl = [slice(hh * dk_b, (hh + 1) * dk_b) for hh in range(2)]
    vsl = [slice(hh * dv_b, (hh + 1) * dv_b) for hh in range(2)]


    gla, ret = [], []
    for g in groups:
        c = cols[g]
        b = _cumsum_rows_mxu(la_ref[:, g * dk_a:(g + 1) * dk_a]) * LOG2_E
        q = h_ref[:, c["qa"]] * (dk_a ** -0.5)
        k = h_ref[:, c["ka"]]
        v = h_ref[:, c["va"]].astype(_BF16)
        s_gla = sg_ref[0, 0, g]
        gla.append(dict(b=b, q=q, k=k, v=v, s=s_gla, o=_dot(q * jnp.exp2(b), s_gla)))

        qb = _rope(h_ref[:, c["qb"]], cos, sin)
        kb = _rope(h_ref[:, c["kb"]], cos, sin) * (dk_b ** -0.5)
        vb = h_ref[:, c["vb"]].astype(_BF16)
        lg = lg_ref[g]
        q_dec = qb * qdec_ref[g]
        s_ret = [sr_ref[0, 0, 2 * g + hh] for hh in range(2)]
        ret.append(dict(
            vb=vb, lg=lg, s=s_ret, k_dec=kb * kdec_ref[g],
            o=[_dot(q_dec[:, ksl[hh]], s_ret[hh]) for hh in range(2)],
            qk=[_dot_nt(qb[:, ksl[hh]], kb[:, ksl[hh]]) for hh in range(2)]))

    for g in groups:
        q, k, b = gla[g]["q"], gla[g]["k"], gla[g]["b"]
        a_off = [jnp.zeros((GLA_SUB, chunk), _F32)]
        for i in range(1, n_sub):
            lo = i * GLA_SUB
            r = b[lo - 1:lo, :]
            k_pre = k * jnp.exp2(jnp.where(rows < lo, r - b, -jnp.inf))
            a_off.append(_dot_nt(q[lo:lo + GLA_SUB] * jnp.exp2(b[lo:lo + GLA_SUB] - r), k_pre))
        gla[g]["a"] = a_off

    for g in groups:
        q, k, b, v = gla[g]["q"], gla[g]["k"], gla[g]["b"], gla[g]["v"]
        intra = []
        for i in range(n_sub):
            lo = i * GLA_SUB
            sub = slice(lo, lo + GLA_SUB)
            a = _gla_pair_scores(gla[g]["a"][i], q[sub], k[sub], b[sub], lo, lambda s: (s, GLA_SUB))
            intra.append(_dot(a, v))
        gla[g]["o"] = gla[g]["o"] + jnp.concatenate(intra, axis=0)
        for hh in range(2):
            scores = ret[g]["qk"][hh] * dmat_ref[2 * g + hh]
            ret[g]["o"][hh] = ret[g]["o"][hh] + _dot(scores, ret[g]["vb"][:, vsl[hh]])

    for g in groups:
        k, b, v = gla[g]["k"], gla[g]["b"], gla[g]["v"]
        b_last = b[chunk - 1:chunk, :]
        decay_col = _rows_to_cols([jnp.exp2(b_last)])[:, 0:1]
        sg_ref[0, 0, g] = decay_col * gla[g]["s"] + _dot_tn(k * jnp.exp2(b_last - b), v)
        for hh in range(2):
            lg1 = ret[g]["lg"][:, hh * dk_b:hh * dk_b + 1]
            sr_ref[0, 0, 2 * g + hh] = (jnp.exp(lg1 * chunk) * ret[g]["s"][hh]
                                        + _dot_tn(ret[g]["k_dec"][:, ksl[hh]], ret[g]["vb"][:, vsl[hh]]))

    for g in groups:
        c = cols[g]
        gv = slice(g * dv_a, (g + 1) * dv_a)
        o_a = _rms_heads(gla[g]["o"], gnw_ref[...]) * h_ref[:, c["za"]]
        o_b = []
        for hh in range(2):
            nsl = slice(g * dv_a + hh * dv_b, g * dv_a + (hh + 1) * dv_b)
            o_b.append(_group_norm(ret[g]["o"][hh], rnw_ref[:, nsl], rnb_ref[:, nsl]))
        o_b = jnp.concatenate(o_b, axis=1) * h_ref[:, c["zb"]]
        merged = h_ref[:, c["ga"]] * o_a + h_ref[:, c["gb"]] * o_b
        m_ref[:, gv] = merged.astype(m_ref.dtype)


def _mixer_prompt(h, la, params, rope, lg, w_out, layer, batch, seq, d_model):
    wq, wv = d_model // 8, d_model // 4
    dk_a, dv_a = wq, wv
    dk_b, dv_b = wq // 2, wv // 2
    chunk = PROMPT_CHUNK
    nc = seq // chunk
    e_rows = w_out.shape[1]
    bf16_rows = 2 * SUBLANES
    assert e_rows % (batch * nc) == 0 and (e_rows // (batch * nc)) % bf16_rows == 0
    w_rows = e_rows // (batch * nc)
    whole = lambda a: pl.BlockSpec(a.shape, lambda b, c: (0,) * a.ndim)
    in_specs = [pl.BlockSpec((chunk, h.shape[1]), lambda b, c: (b * nc + c, 0)),
                pl.BlockSpec((chunk, la.shape[1]), lambda b, c: (b * nc + c, 0))]
    in_specs += [whole(p) for p in params]
    in_specs += [pl.BlockSpec((chunk, wq), lambda b, c: (c, 0)),
                 pl.BlockSpec((chunk, wq), lambda b, c: (c, 0)),
                 whole(lg),
                 pl.BlockSpec((1, w_rows, w_out.shape[2]), lambda b, c: (layer, b * nc + c, 0))]
    out_specs = [
        pl.BlockSpec((chunk, d_model), lambda b, c: (b * nc + c, 0)),
        pl.BlockSpec((1, 1, GLA_HEADS, dk_a, dv_a), lambda b, c: (0, b, 0, 0, 0)),
        pl.BlockSpec((1, 1, RET_HEADS, dk_b, dv_b), lambda b, c: (0, b, 0, 0, 0)),
        pl.BlockSpec((w_rows, w_out.shape[2]), lambda b, c: (b * nc + c, 0)),
    ]
    out_shape = [
        jax.ShapeDtypeStruct((batch * seq, d_model), _BF16),
        jax.ShapeDtypeStruct((1, batch, GLA_HEADS, dk_a, dv_a), _F32),
        jax.ShapeDtypeStruct((1, batch, RET_HEADS, dk_b, dv_b), _F32),
        jax.ShapeDtypeStruct(w_out.shape[1:], _BF16),
    ]
    return pl.pallas_call(
        _mixer_prompt_kernel,
        grid=(batch, nc),
        in_specs=in_specs,
        out_specs=out_specs,
        out_shape=out_shape,
        scratch_shapes=[pltpu.VMEM((HEAD_GROUPS, chunk, wq), _F32),
                        pltpu.VMEM((HEAD_GROUPS, chunk, wq), _F32),
                        pltpu.VMEM((RET_HEADS, chunk, chunk), _F32)],
        compiler_params=pltpu.CompilerParams(
            dimension_semantics=("arbitrary", "arbitrary"),
            vmem_limit_bytes=VMEM_LIMIT_BYTES),
        name="mixer_prompt",
    )(h, la, *params, *rope, lg, w_out)


def _mixer_sample_unit(qa_ref, ka_ref, va_ref, za_ref, qb_ref, kb_ref, vb_ref, zb_ref,
                       ga_ref, gb_ref, la_ref, gnw_ref, rnw_ref, rnb_ref,
                       cos_ref, sin_ref, lg_ref, sg0_ref, sr0_ref, m_ref, sg_ref, sr_ref,
                       *, steps):
    rows_n, dk_a = qa_ref.shape
    nseq = rows_n // steps
    dk_b = qb_ref.shape[1] // 2
    dv_b = vb_ref.shape[1] // 2
    rows = _iota((rows_n, 1), 0)
    rows_p = _iota((LANES, 1), 0)
    pad = LANES - rows_n
    ksl = [slice(hh * dk_b, (hh + 1) * dk_b) for hh in range(2)]
    vsl = [slice(hh * dv_b, (hh + 1) * dv_b) for hh in range(2)]

    def pad_rows(x):
        return jnp.concatenate([x, jnp.zeros((pad, x.shape[1]), x.dtype)], axis=0)

    def seq_rows(x, n, r=rows):
        return jnp.where((r >= n * steps) & (r < (n + 1) * steps), x, 0.0)

    b = _seg_cumsum(la_ref[...], steps) * LOG2_E
    q = qa_ref[...] * (dk_a ** -0.5)
    k = ka_ref[...]
    v = pad_rows(va_ref[...]).astype(_BF16)
    q_dec = q * jnp.exp2(b)
    s_gla = [sg0_ref[0, n, 0] for n in range(nseq)]
    o_a = _dot(seq_rows(q_dec, 0), s_gla[0])
    for n in range(1, nseq):
        o_a = o_a + _dot(seq_rows(q_dec, n), s_gla[n])

    cos, sin = cos_ref[...], sin_ref[...]
    qb = _rope(qb_ref[...], cos, sin)
    kb = _rope(kb_ref[...], cos, sin) * (dk_b ** -0.5)
    vb = pad_rows(vb_ref[...]).astype(_BF16)
    lg = lg_ref[0]
    lg1 = [lg[:, hh * dk_b:hh * dk_b + 1] for hh in range(2)]
    tpos = ((_iota(qb.shape, 0) & (steps - 1)) + 1).astype(_F32)
    qb_dec = qb * jnp.exp(lg * tpos)
    kb_p = pad_rows(kb)
    s_ret = [[sr0_ref[0, n, hh] for n in range(nseq)] for hh in range(2)]
    o_r, qk = [], []
    for hh in range(2):
        o_h = _dot(seq_rows(qb_dec[:, ksl[hh]], 0), s_ret[hh][0])
        for n in range(1, nseq):
            o_h = o_h + _dot(seq_rows(qb_dec[:, ksl[hh]], n), s_ret[hh][n])
        o_r.append(o_h)
        qk.append(_dot_nt(qb[:, ksl[hh]], kb_p[:, ksl[hh]]))

    a = jnp.zeros((rows_n, LANES), _F32)
    a = _gla_pair_scores(a, q, k, b, 0, lambda s: (s, (s // steps + 1) * steps))
    o_a = o_a + _dot(a, v)
    r_i = _iota((rows_n, LANES), 0)
    c_i = _iota((rows_n, LANES), 1)
    same_seq = (r_i >> (steps.bit_length() - 1)) == (c_i >> (steps.bit_length() - 1))
    tdiff = r_i - c_i
    for hh in range(2):
        decay = jnp.exp(jnp.where((tdiff >= 0) & same_seq, lg1[hh] * tdiff.astype(_F32), -jnp.inf))
        o_r[hh] = o_r[hh] + _dot(qk[hh] * decay, vb[:, vsl[hh]])

    def finish():
        b_last = [b[(n + 1) * steps - 1:(n + 1) * steps, :] for n in range(nseq)]
        b_end = jnp.zeros_like(b)
        for n in range(nseq):
            b_end = b_end + seq_rows(jnp.broadcast_to(b_last[n], b.shape), n)
        k_dec = k * jnp.exp2(b_end - b)
        decay_cols = _rows_to_cols([jnp.exp2(r) for r in b_last])
        for n in range(nseq):
            sg_ref[0, n, 0] = (decay_cols[:, n:n + 1] * s_gla[n]
                               + _dot_tn(seq_rows(k_dec, n), v[:rows_n]))
        kb_dec = kb * jnp.exp(lg * (steps - tpos))
        for hh in range(2):
            for n in range(nseq):
                sr_ref[0, n, hh] = (jnp.exp(lg1[hh] * steps) * s_ret[hh][n]
                                    + _dot_tn(seq_rows(kb_dec[:, ksl[hh]], n), vb[:rows_n, vsl[hh]]))
        o_an = _rms_heads(o_a, gnw_ref[...]) * za_ref[...]
        o_bn = jnp.concatenate([_group_norm(o_r[hh], rnw_ref[:, vsl[hh]], rnb_ref[:, vsl[hh]])
                                for hh in range(2)], axis=1) * zb_ref[...]
        merged = ga_ref[...] * o_an + gb_ref[...] * o_bn
        m_ref[...] = merged.astype(m_ref.dtype)

    return finish


def _sample_units(h, la, params, rope, lg, state_gla, state_ret, layer, nseq_total, steps, d_model):
    wq, wv = d_model // 8, d_model // 4
    dk_a, dv_a = wq, wv
    dk_b, dv_b = wq // 2, wv // 2
    nseq = SAMPLE_SEQS
    rows_n = nseq * steps
    assert steps & (steps - 1) == 0 and nseq_total % nseq == 0
    d = d_model

    def specs(unit_of):
        def at(fn):
            return lambda *ids: fn(*unit_of(*ids))

        def h_spec(width, col_off):
            base = col_off // width
            return pl.BlockSpec((rows_n, width), at(lambda p, g: (p, base + g)))
        in_specs = [h_spec(wq, 0), h_spec(wq, d // 2), h_spec(wv, d), h_spec(wv, 2 * d),
                    h_spec(wq, 3 * d), h_spec(wq, 3 * d + d // 2), h_spec(wv, 4 * d),
                    h_spec(wv, 5 * d), h_spec(wv, 6 * d), h_spec(wv, 7 * d)]
        in_specs += [
            pl.BlockSpec((rows_n, wq), at(lambda p, g: (p, g))),
            pl.BlockSpec((1, wv), at(lambda p, g: (0, 0))),
            pl.BlockSpec((1, wv), at(lambda p, g: (0, g))),
            pl.BlockSpec((1, wv), at(lambda p, g: (0, g))),
            pl.BlockSpec((rows_n, wq), at(lambda p, g: (0, 0))),
            pl.BlockSpec((rows_n, wq), at(lambda p, g: (0, 0))),
            pl.BlockSpec((1, 1, wq), at(lambda p, g: (g, 0, 0))),
            pl.BlockSpec((1, nseq, 1, dk_a, dv_a), at(lambda p, g: (layer, p, g, 0, 0))),
            pl.BlockSpec((1, nseq, 2, dk_b, dv_b), at(lambda p, g: (layer, p, g, 0, 0))),
        ]
        out_specs = [
            pl.BlockSpec((rows_n, wv), at(lambda p, g: (p, g))),
            pl.BlockSpec((1, nseq, 1, dk_a, dv_a), at(lambda p, g: (0, p, g, 0, 0))),
            pl.BlockSpec((1, nseq, 2, dk_b, dv_b), at(lambda p, g: (0, p, g, 0, 0))),
        ]
        out_shape = [
            jax.ShapeDtypeStruct((nseq_total * steps, d_model), _BF16),
            jax.ShapeDtypeStruct((1, nseq_total, GLA_HEADS, dk_a, dv_a), _F32),
            jax.ShapeDtypeStruct((1, nseq_total, RET_HEADS, dk_b, dv_b), _F32),
        ]
        return in_specs, out_specs, out_shape

    operands = [h] * 10 + [la, *params, *rope, lg, state_gla, state_ret]
    n_units = (nseq_total // nseq) * HEAD_GROUPS
    return operands, steps, n_units, specs


PROJ_OUT_SLABS = 4

def _out_proj_ln(m, x, w, ln_w, ln_b, alpha):
    r = alpha * x + jnp.dot(m, w, preferred_element_type=_F32)
    mu = jnp.mean(r, axis=-1, keepdims=True)
    d = r - mu
    var = jnp.mean(d * d, axis=-1, keepdims=True)
    return d * lax.rsqrt(var + LN_EPS) * ln_w + ln_b


def _proj_out_kernel(m_ref, x_ref, w_ref, lnw_ref, lnb_ref, y_ref, *, alpha):
    tm = m_ref.shape[0]
    slab = min(tm, max(tm // PROJ_OUT_SLABS, LANES))
    for r in range(0, tm, slab):
        y_ref[r:r + slab, :] = _out_proj_ln(m_ref[r:r + slab, :], x_ref[r:r + slab, :], w_ref[...],
                                            lnw_ref[...], lnb_ref[...], alpha)


def _proj_out(merged, x2d, w_out, ln_w, ln_b, alpha, tm):
    m, d = x2d.shape
    e = merged.shape[1]
    return pl.pallas_call(
        functools.partial(_proj_out_kernel, alpha=alpha),
        grid=(m // tm,),
        in_specs=[
            pl.BlockSpec((tm, e), lambda i: (i, 0)),
            pl.BlockSpec((tm, d), lambda i: (i, 0)),
            pl.BlockSpec((e, d), lambda i: (0, 0)),
            pl.BlockSpec((1, d), lambda i: (0, 0)),
            pl.BlockSpec((1, d), lambda i: (0, 0)),
        ],
        out_specs=pl.BlockSpec((tm, d), lambda i: (i, 0)),
        out_shape=jax.ShapeDtypeStruct((m, d), _F32),
        compiler_params=pltpu.CompilerParams(
            dimension_semantics=("arbitrary",),
            vmem_limit_bytes=VMEM_LIMIT_BYTES),
        name="proj_out",
    )(merged, x2d, w_out, ln_w, ln_b)


def _rope_tables(pos, dk):
    inv = 1.0 / (ROPE_BASE ** jnp.linspace(0.0, 1.0, dk // 2, dtype=_F32))
    ang = pos.astype(_F32)[:, None] * inv[None, :]
    cos = jnp.repeat(jnp.cos(ang), 2, axis=1)
    sin = jnp.stack([-jnp.sin(ang), jnp.sin(ang)], axis=-1).reshape(ang.shape[0], dk)
    return jnp.tile(cos, (1, 2)), jnp.tile(sin, (1, 2))


def _pick_tile(n, pref):
    t = min(n, pref)
    while n % t:
        t //= 2
    return t


def kernel(x_prompt, x_sample, state_gla, state_ret, w_in, w_lr, b_lr, gla_norm_w,
           ret_norm_w, ret_norm_b, w_out, ln_w, ln_b):
    depth, d_model, _ = w_in.shape
    batch, seq, _ = x_prompt.shape
    dec_batch, dec_seq, _ = x_sample.shape
    rank = w_lr.shape[1]
    dk_b = d_model // 2 // RET_HEADS
    assert state_gla.shape[2] == GLA_HEADS and state_ret.shape[2] == RET_HEADS
    assert rank <= LANES and seq % PROMPT_CHUNK == 0
    alpha = (2.0 * depth) ** 0.25

    lg_heads = jnp.log(1.0 - 2.0 ** (-5.0 - jnp.arange(RET_HEADS, dtype=_F32)))
    lg = jnp.repeat(lg_heads, dk_b).reshape(HEAD_GROUPS, 1, 2 * dk_b)
    rope_p = _rope_tables(jnp.arange(seq, dtype=jnp.int32), dk_b)
    pos_s = PAST_LEN + jnp.arange(dec_seq, dtype=jnp.int32)
    rope_s = _rope_tables(jnp.tile(pos_s, SAMPLE_SEQS), dk_b)

    hp = x_prompt.reshape(batch * seq, d_model)
    hs = x_sample.reshape(dec_batch * dec_seq, d_model)
    w_t = jnp.swapaxes(w_in, 1, 2)
    gla_p, ret_p, gla_s, ret_s = [], [], [], []
    for l in range(depth):
        lr_lo = 3 * d_model
        w_lr_p = jnp.pad(w_lr[l], ((0, LANES - rank), (0, 0)))
        params = (gla_norm_w[l][None, :], ret_norm_w[l][None, :], ret_norm_b[l][None, :])
        lnw, lnb = ln_w[l][None, :], ln_b[l][None, :]

        xp_b, la_p = _prep_x(hp, w_t, l, lr_lo, rank, w_lr_p, b_lr[l][None, :],
                             _pick_tile(hp.shape[0], 512))
        xs_b, la_s = _prep_x(hs, w_t, l, lr_lo, rank, w_lr_p, b_lr[l][None, :],
                             _pick_tile(hs.shape[0], 512))
        (h_s,) = _proj_in(xs_b, w_t, l, rank, _pick_tile(xs_b.shape[0], 1024))
        units = _sample_units(h_s, la_s, params, rope_s, lg, state_gla, state_ret, l,
                              dec_batch, dec_seq, d_model)
        h_p, merged_s, sg_s, sr_s = _proj_in(xp_b, w_t, l, rank, _pick_tile(xp_b.shape[0], 1024),
                                             sample=units)

        merged, sg, sr, w_o = _mixer_prompt(h_p, la_p, params, rope_p, lg, w_out, l,
                                            batch, seq, d_model)
        hp = _proj_out(merged, hp, w_o, lnw, lnb, alpha, _pick_tile(hp.shape[0], 512))
        gla_p.append(sg)
        ret_p.append(sr)

        merged, sg, sr = merged_s, sg_s, sr_s
        hs = _proj_out(merged, hs, w_o, lnw, lnb, alpha, _pick_tile(hs.shape[0], 256))
        gla_s.append(sg)
        ret_s.append(sr)

    cat = lambda parts: parts[0] if len(parts) == 1 else jnp.concatenate(parts, axis=0)
    return (hp.reshape(batch, seq, d_model), hs.reshape(dec_batch, dec_seq, d_model),
            cat(gla_p), cat(ret_p), cat(gla_s), cat(ret_s))
```

```python
import functools

import jax
import jax.numpy as jnp
from jax import lax
from jax.experimental import pallas as pl
from jax.experimental.pallas import tpu as pltpu

GLA_HEADS = 4
RET_HEADS = 8
HEAD_GROUPS = 4
GLA_TAU = 16.0
LOG2_E = 1.4426950408889634
ROPE_BASE = 10000.0
LN_EPS = 1e-5
HEAD_NORM_EPS = 1e-6
PAST_LEN = 16384

LANES = 128
SUBLANES = 8
VMEM_LIMIT_BYTES = 56 * 1024 * 1024

PROMPT_CHUNK = 64
GLA_SUB = 16
SAMPLE_SEQS = 4

_F32 = jnp.float32
_BF16 = jnp.bfloat16
_NT = (((1,), (1,)), ((), ()))
_TN = (((0,), (0,)), ((), ()))


def _dot(a, b):
    return jnp.dot(a.astype(_BF16), b.astype(_BF16), preferred_element_type=_F32)


def _dot_nt(a, b):
    return lax.dot_general(a.astype(_BF16), b.astype(_BF16), _NT, preferred_element_type=_F32)


def _dot_tn(a, b):
    return lax.dot_general(a.astype(_BF16), b.astype(_BF16), _TN, preferred_element_type=_F32)


def _log_sigmoid(x):
    return jnp.minimum(x, 0.0) - jnp.log(1.0 + jnp.exp(-jnp.abs(x)))


def _silu(x):
    return x * jax.nn.sigmoid(x)


def _iota(shape, axis):
    return lax.broadcasted_iota(jnp.int32, shape, axis)


def _seg_cumsum(x, seg):
    pos = _iota(x.shape, 0) & (seg - 1)
    s = 1
    while s < seg:
        x = x + jnp.where(pos >= s, pltpu.roll(x, s, axis=0), 0.0)
        s *= 2
    return x


def _cumsum_rows_mxu(x):
    n = x.shape[0]
    hi = x.astype(_BF16)
    r1 = x - hi.astype(_F32)
    mid = r1.astype(_BF16)
    lo = (r1 - mid.astype(_F32)).astype(_BF16)
    tri3 = (_iota((n, 3 * n), 0) >= (_iota((n, 3 * n), 1) & (n - 1))).astype(_BF16)
    return jnp.dot(tri3, jnp.concatenate([hi, mid, lo], axis=0), preferred_element_type=_F32)


def _rows_to_cols(rows):
    n = rows[0].shape[1]
    rid = _iota((LANES, n), 0)
    stack = jnp.zeros((LANES, n), _F32)
    for i, r in enumerate(rows):
        stack = jnp.where(rid == i, r, stack)
    return stack.T


def _rope(x, cos, sin_signed):
    w = x.shape[1]
    even = (_iota(x.shape, 1) & 1) == 0
    swapped = jnp.where(even, pltpu.roll(x, w - 1, axis=1), pltpu.roll(x, 1, axis=1))
    return x * cos + swapped * sin_signed


def _gla_pair_scores(a, q, k, b2, lane0, row_lo_hi):
    n = q.shape[0]
    lanes = _iota((SUBLANES, a.shape[1]), 1)
    tiles = [a[r:r + SUBLANES] for r in range(0, n, SUBLANES)]
    for s in range(n):
        lo, hi = row_lo_hi(s)
        for j, r0 in enumerate(range(0, n, SUBLANES)):
            if r0 + SUBLANES <= lo or r0 >= hi:
                continue
            e = jnp.exp2(b2[r0:r0 + SUBLANES] - b2[s:s + 1, :])
            col = jnp.sum(q[r0:r0 + SUBLANES] * k[s:s + 1, :] * e, axis=-1, keepdims=True)
            take = lanes == lane0 + s
            if lo > r0 or hi < r0 + SUBLANES:
                rows = _iota((SUBLANES, 1), 0) + r0
                take = take & (rows >= lo) & (rows < hi)
            tiles[j] = jnp.where(take, col, tiles[j])
    return jnp.concatenate(tiles, axis=0)


def _rms_heads(o, w):
    return o * lax.rsqrt(jnp.mean(o * o, axis=-1, keepdims=True) + HEAD_NORM_EPS) * w


def _group_norm(o, w, b):
    mu = jnp.mean(o, axis=-1, keepdims=True)
    d = o - mu
    var = jnp.mean(d * d, axis=-1, keepdims=True)
    return d * lax.rsqrt(var + HEAD_NORM_EPS) * w + b


def _prep_x_kernel(x_ref, wcol_ref, wlr_ref, blr_ref, xb_ref, la_ref, *, rank):
    x = x_ref[...]
    if x.ndim == 3:
        x = x.reshape(x.shape[0] * x.shape[1], x.shape[2])
    xb = x.astype(_BF16)
    xb_ref[...] = xb
    wrow = wcol_ref[0]
    w_lra = jnp.concatenate([wrow, jnp.zeros((LANES - rank, wrow.shape[1]), _F32)], axis=0)
    lr = _dot_nt(xb, w_lra)
    w = wlr_ref[...]
    w_hi = w.astype(_BF16).astype(_F32)
    w_lo = w - w_hi
    lr_hi = lr.astype(_BF16).astype(_F32)
    lr_lo = lr - lr_hi
    lr_cat = lr_hi + pltpu.roll(lr_hi, rank, axis=1) + pltpu.roll(lr_lo, 2 * rank, axis=1)
    w_cat = jnp.concatenate([w_hi[:rank], w_lo[:rank], w_hi[:rank],
                             jnp.zeros((LANES - 3 * rank, w.shape[1]), _F32)], axis=0)
    logit = _dot(lr_cat, w_cat) + blr_ref[...]
    la_ref[...] = _log_sigmoid(logit) * (1.0 / GLA_TAU)


def _prep_x(x, w_t, layer, lr_col, rank, w_lr, b_lr, tm):
    kdim = x.shape[-1]
    m = x.size // kdim
    qk = w_lr.shape[1]
    assert lr_col % rank == 0 and rank % SUBLANES == 0 and 3 * rank <= LANES
    whole = lambda a: pl.BlockSpec(a.shape, lambda i: (0,) * a.ndim)
    if x.ndim == 3:
        x_spec = pl.BlockSpec((tm // x.shape[1], x.shape[1], kdim), lambda i: (i, 0, 0))
    else:
        x_spec = pl.BlockSpec((tm, kdim), lambda i: (i, 0))
    return pl.pallas_call(
        functools.partial(_prep_x_kernel, rank=rank),
        grid=(m // tm,),
        in_specs=[x_spec,
                  pl.BlockSpec((1, rank, kdim), lambda i: (layer, lr_col // rank, 0)),
                  whole(w_lr), whole(b_lr)],
        out_specs=[pl.BlockSpec((tm, kdim), lambda i: (i, 0)),
                   pl.BlockSpec((tm, qk), lambda i: (i, 0))],
        out_shape=[jax.ShapeDtypeStruct((m, kdim), _BF16),
                   jax.ShapeDtypeStruct((m, qk), _F32)],
        compiler_params=pltpu.CompilerParams(
            dimension_semantics=("arbitrary",), vmem_limit_bytes=VMEM_LIMIT_BYTES),
        name="prep_x",
    )(x, w_t, w_lr, b_lr)


N_SAMPLE_IN = 19


def _proj_in_kernel(*refs, n_plain, shift, silu_tiles, sigmoid_tiles, sample_steps):
    x_ref, wa_ref, we_ref = refs[:3]
    wbf_ref = refs[-1]
    if sample_steps:
        unit_in = refs[3:3 + N_SAMPLE_IN]
        h_ref = refs[3 + N_SAMPLE_IN]
        unit_out = refs[4 + N_SAMPLE_IN:7 + N_SAMPLE_IN]
    else:
        h_ref = refs[3]
    j = pl.program_id(0)
    i = pl.program_id(1)
    tn, kdim = wbf_ref.shape

    @pl.when(i == 0)
    def _():
        @pl.when(j < n_plain)
        def _():
            wbf_ref[...] = wa_ref[0].astype(_BF16)

        @pl.when(j >= n_plain)
        def _():
            wbf_ref[0:tn - shift, :] = wa_ref[0, shift:tn, :].astype(_BF16)
            wbf_ref[tn - shift:tn, :] = we_ref[0].astype(_BF16)

    is_silu = functools.reduce(jnp.logical_or, [j == t for t in silu_tiles])
    is_sigmoid = functools.reduce(jnp.logical_or, [j == t for t in sigmoid_tiles])

    def slab(cols, activated):
        acc = lax.dot_general(x_ref[...], wbf_ref[cols, :], _NT, preferred_element_type=_F32)
        if activated:
            s = 0.5 * jnp.tanh(0.5 * acc) + 0.5
            acc = jnp.where(is_sigmoid, s, acc * s)
        h_ref[:, cols] = acc

    def tile(activated):
        finish_unit = None
        if sample_steps:
            finish_unit = _mixer_sample_unit(*unit_in, *unit_out, steps=sample_steps)
        slab(slice(0, tn // 4), activated)
        if finish_unit is not None:
            finish_unit()
        slab(slice(tn // 4, tn), activated)

    @pl.when(is_silu | is_sigmoid)
    def _():
        tile(True)

    @pl.when(jnp.logical_not(is_silu | is_sigmoid))
    def _():
        tile(False)


PROJ_TILES = 16


def _proj_in(x, w_t, layer, rank, tm, sample=None):
    m, kdim = x.shape
    d_model = kdim
    tn = d_model // 2
    nj = PROJ_TILES
    n_plain = 6
    bf16_rows = 2 * SUBLANES
    assert rank % bf16_rows == 0 and tn % rank == 0 and m % tm == 0
    ni = m // tm

    def row_block(j, i):
        return jnp.where(j % 2 == 0, i, ni - 1 - i)
    in_specs = [
        pl.BlockSpec((tm, kdim), lambda j, i: (row_block(j, i), 0)),
        pl.BlockSpec((1, tn, kdim), lambda j, i: (layer, j, 0)),
        pl.BlockSpec((1, rank, kdim), lambda j, i: (layer, (j + 1) * (tn // rank), 0)),
    ]
    out_specs = [pl.BlockSpec((tm, tn), lambda j, i: (row_block(j, i), j))]
    out_shape = [jax.ShapeDtypeStruct((m, nj * tn), _F32)]
    operands = [x, w_t, w_t]
    sample_steps = 0
    if sample is not None:
        unit_operands, sample_steps, n_units, unit_specs = sample
        assert n_units <= nj * ni

        def unit_of(j, i):
            u = jnp.minimum(j * ni + i, n_units - 1)
            return u // HEAD_GROUPS, u % HEAD_GROUPS
        unit_in, unit_out, unit_shape = unit_specs(unit_of)
        assert len(unit_in) == N_SAMPLE_IN
        in_specs += unit_in
        out_specs += unit_out
        out_shape += unit_shape
        operands += unit_operands
    kern = functools.partial(_proj_in_kernel, n_plain=n_plain, shift=rank,
                             silu_tiles=(4, 5, 10, 11), sigmoid_tiles=(12, 13, 14, 15),
                             sample_steps=sample_steps)
    return pl.pallas_call(
        kern,
        grid=(nj, ni),
        in_specs=in_specs,
        out_specs=out_specs,
        out_shape=out_shape,
        scratch_shapes=[pltpu.VMEM((tn, kdim), _BF16)],
        compiler_params=pltpu.CompilerParams(
            dimension_semantics=("arbitrary", "arbitrary"),
            vmem_limit_bytes=VMEM_LIMIT_BYTES),
        name="proj_in_mix" if sample is not None else "proj_in",
    )(*operands)


def _h_cols(d_model, g):
    wq, wv = d_model // 8, d_model // 4
    d = d_model
    offs = dict(qa=(0, wq), ka=(d // 2, wq), va=(d, wv), za=(2 * d, wv),
                qb=(3 * d, wq), kb=(3 * d + d // 2, wq), vb=(4 * d, wv), zb=(5 * d, wv),
                ga=(6 * d, wv), gb=(7 * d, wv))
    return {n: slice(o + g * w, o + (g + 1) * w) for n, (o, w) in offs.items()}


def _mixer_prompt_kernel(h_ref, la_ref, gnw_ref, rnw_ref, rnb_ref,
                         cos_ref, sin_ref, lg_ref, wout_ref, m_ref, sg_ref, sr_ref, wout_bf_ref,
                         qdec_ref, kdec_ref, dmat_ref):
    wout_bf_ref[...] = wout_ref[0].astype(_BF16)
    chunk = h_ref.shape[0]
    d_model = m_ref.shape[1]
    dk_a, dv_a = sg_ref.shape[3:]
    dk_b, dv_b = sr_ref.shape[3:]

    @pl.when(pl.program_id(1) == 0)
    def _():
        sg_ref[...] = jnp.zeros(sg_ref.shape, _F32)
        sr_ref[...] = jnp.zeros(sr_ref.shape, _F32)

    @pl.when((pl.program_id(0) == 0) & (pl.program_id(1) == 0))
    def _():
        tpos = (_iota((chunk, 2 * dk_b), 0) + 1).astype(_F32)
        tdiff = _iota((chunk, chunk), 0) - _iota((chunk, chunk), 1)
        for g in range(HEAD_GROUPS):
            lg = lg_ref[g]
            qdec_ref[g] = jnp.exp(lg * tpos)
            kdec_ref[g] = jnp.exp(lg * (chunk - tpos))
            for hh in range(2):
                lg1 = lg[:, hh * dk_b:hh * dk_b + 1]
                dmat_ref[2 * g + hh] = jnp.exp(
                    jnp.where(tdiff >= 0, lg1 * tdiff.astype(_F32), -jnp.inf))

    rows = _iota((chunk, 1), 0)
    cos, sin = cos_ref[...], sin_ref[...]
    n_sub = chunk // GLA_SUB
    groups = range(HEAD_GROUPS)
    cols = [_h_cols(d_model, g) for g in groups]
    ksl = [slice(hh * dk_b, (hh + 1) * dk_b) for hh in range(2)]
    vsl = [slice(hh * dv_b, (hh + 1) * dv_b) for hh in range(2)]


    gla, ret = [], []
    for g in groups:
        c = cols[g]
        b = _cumsum_rows_mxu(la_ref[:, g * dk_a:(g + 1) * dk_a]) * LOG2_E
        q = h_ref[:, c["qa"]] * (dk_a ** -0.5)
        k = h_ref[:, c["ka"]]
        v = h_ref[:, c["va"]].astype(_BF16)
        s_gla = sg_ref[0, 0, g]
        gla.append(dict(b=b, q=q, k=k, v=v, s=s_gla, o=_dot(q * jnp.exp2(b), s_gla)))

        qb = _rope(h_ref[:, c["qb"]], cos, sin)
        kb = _rope(h_ref[:, c["kb"]], cos, sin) * (dk_b ** -0.5)
        vb = h_ref[:, c["vb"]].astype(_BF16)
        lg = lg_ref[g]
        q_dec = qb * qdec_ref[g]
        s_ret = [sr_ref[0, 0, 2 * g + hh] for hh in range(2)]
        ret.append(dict(
            vb=vb, lg=lg, s=s_ret, k_dec=kb * kdec_ref[g],
            o=[_dot(q_dec[:, ksl[hh]], s_ret[hh]) for hh in range(2)],
            qk=[_dot_nt(qb[:, ksl[hh]], kb[:, ksl[hh]]) for hh in range(2)]))

    for g in groups:
        q, k, b = gla[g]["q"], gla[g]["k"], gla[g]["b"]
        a_off = [jnp.zeros((GLA_SUB, chunk), _F32)]
        for i in range(1, n_sub):
            lo = i * GLA_SUB
            r = b[lo - 1:lo, :]
            k_pre = k * jnp.exp2(jnp.where(rows < lo, r - b, -jnp.inf))
            a_off.append(_dot_nt(q[lo:lo + GLA_SUB] * jnp.exp2(b[lo:lo + GLA_SUB] - r), k_pre))
        gla[g]["a"] = a_off

    for g in groups:
        q, k, b, v = gla[g]["q"], gla[g]["k"], gla[g]["b"], gla[g]["v"]
        intra = []
        for i in range(n_sub):
            lo = i * GLA_SUB
            sub = slice(lo, lo + GLA_SUB)
            a = _gla_pair_scores(gla[g]["a"][i], q[sub], k[sub], b[sub], lo, lambda s: (s, GLA_SUB))
            intra.append(_dot(a, v))
        gla[g]["o"] = gla[g]["o"] + jnp.concatenate(intra, axis=0)
        for hh in range(2):
            scores = ret[g]["qk"][hh] * dmat_ref[2 * g + hh]
            ret[g]["o"][hh] = ret[g]["o"][hh] + _dot(scores, ret[g]["vb"][:, vsl[hh]])

    for g in groups:
        k, b, v = gla[g]["k"], gla[g]["b"], gla[g]["v"]
        b_last = b[chunk - 1:chunk, :]
        decay_col = _rows_to_cols([jnp.exp2(b_last)])[:, 0:1]
        sg_ref[0, 0, g] = decay_col * gla[g]["s"] + _dot_tn(k * jnp.exp2(b_last - b), v)
        for hh in range(2):
            lg1 = ret[g]["lg"][:, hh * dk_b:hh * dk_b + 1]
            sr_ref[0, 0, 2 * g + hh] = (jnp.exp(lg1 * chunk) * ret[g]["s"][hh]
                                        + _dot_tn(ret[g]["k_dec"][:, ksl[hh]], ret[g]["vb"][:, vsl[hh]]))

    for g in groups:
        c = cols[g]
        gv = slice(g * dv_a, (g + 1) * dv_a)
        o_a = _rms_heads(gla[g]["o"], gnw_ref[...]) * h_ref[:, c["za"]]
        o_b = []
        for hh in range(2):
            nsl = slice(g * dv_a + hh * dv_b, g * dv_a + (hh + 1) * dv_b)
            o_b.append(_group_norm(ret[g]["o"][hh], rnw_ref[:, nsl], rnb_ref[:, nsl]))
        o_b = jnp.concatenate(o_b, axis=1) * h_ref[:, c["zb"]]
        merged = h_ref[:, c["ga"]] * o_a + h_ref[:, c["gb"]] * o_b
        m_ref[:, gv] = merged.astype(m_ref.dtype)


def _mixer_prompt(h, la, params, rope, lg, w_out, layer, batch, seq, d_model):
    wq, wv = d_model // 8, d_model // 4
    dk_a, dv_a = wq, wv
    dk_b, dv_b = wq // 2, wv // 2
    chunk = PROMPT_CHUNK
    nc = seq // chunk
    e_rows = w_out.shape[1]
    bf16_rows = 2 * SUBLANES
    assert e_rows % (batch * nc) == 0 and (e_rows // (batch * nc)) % bf16_rows == 0
    w_rows = e_rows // (batch * nc)
    whole = lambda a: pl.BlockSpec(a.shape, lambda b, c: (0,) * a.ndim)
    in_specs = [pl.BlockSpec((chunk, h.shape[1]), lambda b, c: (b * nc + c, 0)),
                pl.BlockSpec((chunk, la.shape[1]), lambda b, c: (b * nc + c, 0))]
    in_specs += [whole(p) for p in params]
    in_specs += [pl.BlockSpec((chunk, wq), lambda b, c: (c, 0)),
                 pl.BlockSpec((chunk, wq), lambda b, c: (c, 0)),
                 whole(lg),
                 pl.BlockSpec((1, w_rows, w_out.shape[2]), lambda b, c: (layer, b * nc + c, 0))]
    out_specs = [
        pl.BlockSpec((chunk, d_model), lambda b, c: (b * nc + c, 0)),
        pl.BlockSpec((1, 1, GLA_HEADS, dk_a, dv_a), lambda b, c: (0, b, 0, 0, 0)),
        pl.BlockSpec((1, 1, RET_HEADS, dk_b, dv_b), lambda b, c: (0, b, 0, 0, 0)),
        pl.BlockSpec((w_rows, w_out.shape[2]), lambda b, c: (b * nc + c, 0)),
    ]
    out_shape = [
        jax.ShapeDtypeStruct((batch * seq, d_model), _BF16),
        jax.ShapeDtypeStruct((1, batch, GLA_HEADS, dk_a, dv_a), _F32),
        jax.ShapeDtypeStruct((1, batch, RET_HEADS, dk_b, dv_b), _F32),
        jax.ShapeDtypeStruct(w_out.shape[1:], _BF16),
    ]
    return pl.pallas_call(
        _mixer_prompt_kernel,
        grid=(batch, nc),
        in_specs=in_specs,
        out_specs=out_specs,
        out_shape=out_shape,
        scratch_shapes=[pltpu.VMEM((HEAD_GROUPS, chunk, wq), _F32),
                        pltpu.VMEM((HEAD_GROUPS, chunk, wq), _F32),
                        pltpu.VMEM((RET_HEADS, chunk, chunk), _F32)],
        compiler_params=pltpu.CompilerParams(
            dimension_semantics=("arbitrary", "arbitrary"),
            vmem_limit_bytes=VMEM_LIMIT_BYTES),
        name="mixer_prompt",
    )(h, la, *params, *rope, lg, w_out)


def _mixer_sample_unit(qa_ref, ka_ref, va_ref, za_ref, qb_ref, kb_ref, vb_ref, zb_ref,
                       ga_ref, gb_ref, la_ref, gnw_ref, rnw_ref, rnb_ref,
                       cos_ref, sin_ref, lg_ref, sg0_ref, sr0_ref, m_ref, sg_ref, sr_ref,
                       *, steps):
    rows_n, dk_a = qa_ref.shape
    nseq = rows_n // steps
    dk_b = qb_ref.shape[1] // 2
    dv_b = vb_ref.shape[1] // 2
    rows = _iota((rows_n, 1), 0)
    rows_p = _iota((LANES, 1), 0)
    pad = LANES - rows_n
    ksl = [slice(hh * dk_b, (hh + 1) * dk_b) for hh in range(2)]
    vsl = [slice(hh * dv_b, (hh + 1) * dv_b) for hh in range(2)]

    def pad_rows(x):
        return jnp.concatenate([x, jnp.zeros((pad, x.shape[1]), x.dtype)], axis=0)

    def seq_rows(x, n, r=rows):
        return jnp.where((r >= n * steps) & (r < (n + 1) * steps), x, 0.0)

    b = _seg_cumsum(la_ref[...], steps) * LOG2_E
    q = qa_ref[...] * (dk_a ** -0.5)
    k = ka_ref[...]
    v = pad_rows(va_ref[...]).astype(_BF16)
    q_dec = q * jnp.exp2(b)
    s_gla = [sg0_ref[0, n, 0] for n in range(nseq)]
    o_a = _dot(seq_rows(q_dec, 0), s_gla[0])
    for n in range(1, nseq):
        o_a = o_a + _dot(seq_rows(q_dec, n), s_gla[n])

    cos, sin = cos_ref[...], sin_ref[...]
    qb = _rope(qb_ref[...], cos, sin)
    kb = _rope(kb_ref[...], cos, sin) * (dk_b ** -0.5)
    vb = pad_rows(vb_ref[...]).astype(_BF16)
    lg = lg_ref[0]
    lg1 = [lg[:, hh * dk_b:hh * dk_b + 1] for hh in range(2)]
    tpos = ((_iota(qb.shape, 0) & (steps - 1)) + 1).astype(_F32)
    qb_dec = qb * jnp.exp(lg * tpos)
    kb_p = pad_rows(kb)
    s_ret = [[sr0_ref[0, n, hh] for n in range(nseq)] for hh in range(2)]
    o_r, qk = [], []
    for hh in range(2):
        o_h = _dot(seq_rows(qb_dec[:, ksl[hh]], 0), s_ret[hh][0])
        for n in range(1, nseq):
            o_h = o_h + _dot(seq_rows(qb_dec[:, ksl[hh]], n), s_ret[hh][n])
        o_r.append(o_h)
        qk.append(_dot_nt(qb[:, ksl[hh]], kb_p[:, ksl[hh]]))

    a = jnp.zeros((rows_n, LANES), _F32)
    a = _gla_pair_scores(a, q, k, b, 0, lambda s: (s, (s // steps + 1) * steps))
    o_a = o_a + _dot(a, v)
    r_i = _iota((rows_n, LANES), 0)
    c_i = _iota((rows_n, LANES), 1)
    same_seq = (r_i >> (steps.bit_length() - 1)) == (c_i >> (steps.bit_length() - 1))
    tdiff = r_i - c_i
    for hh in range(2):
        decay = jnp.exp(jnp.where((tdiff >= 0) & same_seq, lg1[hh] * tdiff.astype(_F32), -jnp.inf))
        o_r[hh] = o_r[hh] + _dot(qk[hh] * decay, vb[:, vsl[hh]])

    def finish():
        b_last = [b[(n + 1) * steps - 1:(n + 1) * steps, :] for n in range(nseq)]
        b_end = jnp.zeros_like(b)
        for n in range(nseq):
            b_end = b_end + seq_rows(jnp.broadcast_to(b_last[n], b.shape), n)
        k_dec = k * jnp.exp2(b_end - b)
        decay_cols = _rows_to_cols([jnp.exp2(r) for r in b_last])
        for n in range(nseq):
            sg_ref[0, n, 0] = (decay_cols[:, n:n + 1] * s_gla[n]
                               + _dot_tn(seq_rows(k_dec, n), v[:rows_n]))
        kb_dec = kb * jnp.exp(lg * (steps - tpos))
        for hh in range(2):
            for n in range(nseq):
                sr_ref[0, n, hh] = (jnp.exp(lg1[hh] * steps) * s_ret[hh][n]
                                    + _dot_tn(seq_rows(kb_dec[:, ksl[hh]], n), vb[:rows_n, vsl[hh]]))
        o_an = _rms_heads(o_a, gnw_ref[...]) * za_ref[...]
        o_bn = jnp.concatenate([_group_norm(o_r[hh], rnw_ref[:, vsl[hh]], rnb_ref[:, vsl[hh]])
                                for hh in range(2)], axis=1) * zb_ref[...]
        merged = ga_ref[...] * o_an + gb_ref[...] * o_bn
        m_ref[...] = merged.astype(m_ref.dtype)

    return finish


def _sample_units(h, la, params, rope, lg, state_gla, state_ret, layer, nseq_total, steps, d_model):
    wq, wv = d_model // 8, d_model // 4
    dk_a, dv_a = wq, wv
    dk_b, dv_b = wq // 2, wv // 2
    nseq = SAMPLE_SEQS
    rows_n = nseq * steps
    assert steps & (steps - 1) == 0 and nseq_total % nseq == 0
    d = d_model

    def specs(unit_of):
        def at(fn):
            return lambda *ids: fn(*unit_of(*ids))

        def h_spec(width, col_off):
            base = col_off // width
            return pl.BlockSpec((rows_n, width), at(lambda p, g: (p, base + g)))
        in_specs = [h_spec(wq, 0), h_spec(wq, d // 2), h_spec(wv, d), h_spec(wv, 2 * d),
                    h_spec(wq, 3 * d), h_spec(wq, 3 * d + d // 2), h_spec(wv, 4 * d),
                    h_spec(wv, 5 * d), h_spec(wv, 6 * d), h_spec(wv, 7 * d)]
        in_specs += [
            pl.BlockSpec((rows_n, wq), at(lambda p, g: (p, g))),
            pl.BlockSpec((1, wv), at(lambda p, g: (0, 0))),
            pl.BlockSpec((1, wv), at(lambda p, g: (0, g))),
            pl.BlockSpec((1, wv), at(lambda p, g: (0, g))),
            pl.BlockSpec((rows_n, wq), at(lambda p, g: (0, 0))),
            pl.BlockSpec((rows_n, wq), at(lambda p, g: (0, 0))),
            pl.BlockSpec((1, 1, wq), at(lambda p, g: (g, 0, 0))),
            pl.BlockSpec((1, nseq, 1, dk_a, dv_a), at(lambda p, g: (layer, p, g, 0, 0))),
            pl.BlockSpec((1, nseq, 2, dk_b, dv_b), at(lambda p, g: (layer, p, g, 0, 0))),
        ]
        out_specs = [
            pl.BlockSpec((rows_n, wv), at(lambda p, g: (p, g))),
            pl.BlockSpec((1, nseq, 1, dk_a, dv_a), at(lambda p, g: (0, p, g, 0, 0))),
            pl.BlockSpec((1, nseq, 2, dk_b, dv_b), at(lambda p, g: (0, p, g, 0, 0))),
        ]
        out_shape = [
            jax.ShapeDtypeStruct((nseq_total * steps, d_model), _BF16),
            jax.ShapeDtypeStruct((1, nseq_total, GLA_HEADS, dk_a, dv_a), _F32),
            jax.ShapeDtypeStruct((1, nseq_total, RET_HEADS, dk_b, dv_b), _F32),
        ]
        return in_specs, out_specs, out_shape

    operands = [h] * 10 + [la, *params, *rope, lg, state_gla, state_ret]
    n_units = (nseq_total // nseq) * HEAD_GROUPS
    return operands, steps, n_units, specs


PROJ_OUT_SLABS = 4

def _out_proj_ln(m, x, w, ln_w, ln_b, alpha):
    r = alpha * x + jnp.dot(m, w, preferred_element_type=_F32)
    mu = jnp.mean(r, axis=-1, keepdims=True)
    d = r - mu
    var = jnp.mean(d * d, axis=-1, keepdims=True)
    return d * lax.rsqrt(var + LN_EPS) * ln_w + ln_b


def _proj_out_kernel(m_ref, x_ref, w_ref, lnw_ref, lnb_ref, y_ref, *, alpha):
    tm = m_ref.shape[0]
    slab = min(tm, max(tm // PROJ_OUT_SLABS, LANES))
    for r in range(0, tm, slab):
        if x_ref.ndim == 3:
            steps = x_ref.shape[1]
            seqs = slice(r // steps, (r + slab) // steps)
            x = x_ref[seqs].reshape(slab, x_ref.shape[2])
        else:
            x = x_ref[r:r + slab, :]
        y = _out_proj_ln(m_ref[r:r + slab, :], x, w_ref[...], lnw_ref[...], lnb_ref[...], alpha)
        if x_ref.ndim == 3:
            y_ref[seqs] = y.reshape(slab // steps, steps, y.shape[1])
        else:
            y_ref[r:r + slab, :] = y


def _proj_out(merged, x, w_out, ln_w, ln_b, alpha, tm):
    d = x.shape[-1]
    m = x.size // d
    e = merged.shape[1]
    if x.ndim == 3:
        xy_spec = pl.BlockSpec((tm // x.shape[1], x.shape[1], d), lambda i: (i, 0, 0))
    else:
        xy_spec = pl.BlockSpec((tm, d), lambda i: (i, 0))
    return pl.pallas_call(
        functools.partial(_proj_out_kernel, alpha=alpha),
        grid=(m // tm,),
        in_specs=[
            pl.BlockSpec((tm, e), lambda i: (i, 0)),
            xy_spec,
            pl.BlockSpec((e, d), lambda i: (0, 0)),
            pl.BlockSpec((1, d), lambda i: (0, 0)),
            pl.BlockSpec((1, d), lambda i: (0, 0)),
        ],
        out_specs=xy_spec,
        out_shape=jax.ShapeDtypeStruct(x.shape, _F32),
        compiler_params=pltpu.CompilerParams(
            dimension_semantics=("arbitrary",),
            vmem_limit_bytes=VMEM_LIMIT_BYTES),
        name="proj_out",
    )(merged, x, w_out, ln_w, ln_b)


def _rope_tables(pos, dk):
    inv = 1.0 / (ROPE_BASE ** jnp.linspace(0.0, 1.0, dk // 2, dtype=_F32))
    ang = pos.astype(_F32)[:, None] * inv[None, :]
    cos = jnp.repeat(jnp.cos(ang), 2, axis=1)
    sin = jnp.stack([-jnp.sin(ang), jnp.sin(ang)], axis=-1).reshape(ang.shape[0], dk)
    return jnp.tile(cos, (1, 2)), jnp.tile(sin, (1, 2))


def _pick_tile(n, pref):
    t = min(n, pref)
    while n % t:
        t //= 2
    return t


def kernel(x_prompt, x_sample, state_gla, state_ret, w_in, w_lr, b_lr, gla_norm_w,
           ret_norm_w, ret_norm_b, w_out, ln_w, ln_b):
    depth, d_model, _ = w_in.shape
    batch, seq, _ = x_prompt.shape
    dec_batch, dec_seq, _ = x_sample.shape
    rank = w_lr.shape[1]
    dk_b = d_model // 2 // RET_HEADS
    assert state_gla.shape[2] == GLA_HEADS and state_ret.shape[2] == RET_HEADS
    assert rank <= LANES and seq % PROMPT_CHUNK == 0
    alpha = (2.0 * depth) ** 0.25

    lg_heads = jnp.log(1.0 - 2.0 ** (-5.0 - jnp.arange(RET_HEADS, dtype=_F32)))
    lg = jnp.repeat(lg_heads, dk_b).reshape(HEAD_GROUPS, 1, 2 * dk_b)
    rope_p = _rope_tables(jnp.arange(seq, dtype=jnp.int32), dk_b)
    pos_s = PAST_LEN + jnp.arange(dec_seq, dtype=jnp.int32)
    rope_s = _rope_tables(jnp.tile(pos_s, SAMPLE_SEQS), dk_b)

    hp = x_prompt.reshape(batch * seq, d_model)
    hs = x_sample
    n_rows_s = dec_batch * dec_seq
    w_t = jnp.swapaxes(w_in, 1, 2)
    gla_p, ret_p, gla_s, ret_s = [], [], [], []
    for l in range(depth):
        lr_lo = 3 * d_model
        w_lr_p = jnp.pad(w_lr[l], ((0, LANES - rank), (0, 0)))
        params = (gla_norm_w[l][None, :], ret_norm_w[l][None, :], ret_norm_b[l][None, :])
        lnw, lnb = ln_w[l][None, :], ln_b[l][None, :]

        xp_b, la_p = _prep_x(hp, w_t, l, lr_lo, rank, w_lr_p, b_lr[l][None, :],
                             _pick_tile(hp.shape[0], 512))
        xs_b, la_s = _prep_x(hs, w_t, l, lr_lo, rank, w_lr_p, b_lr[l][None, :],
                             _pick_tile(n_rows_s, 512))
        (h_s,) = _proj_in(xs_b, w_t, l, rank, _pick_tile(xs_b.shape[0], 1024))
        units = _sample_units(h_s, la_s, params, rope_s, lg, state_gla, state_ret, l,
                              dec_batch, dec_seq, d_model)
        h_p, merged_s, sg_s, sr_s = _proj_in(xp_b, w_t, l, rank, _pick_tile(xp_b.shape[0], 1024),
                                             sample=units)

        merged, sg, sr, w_o = _mixer_prompt(h_p, la_p, params, rope_p, lg, w_out, l,
                                            batch, seq, d_model)
        hp = _proj_out(merged, hp, w_o, lnw, lnb, alpha, _pick_tile(hp.shape[0], 512))
        gla_p.append(sg)
        ret_p.append(sr)

        merged, sg, sr = merged_s, sg_s, sr_s
        hs = _proj_out(merged, hs, w_o, lnw, lnb, alpha, _pick_tile(n_rows_s, 256))
        gla_s.append(sg)
        ret_s.append(sr)

    cat = lambda parts: parts[0] if len(parts) == 1 else jnp.concatenate(parts, axis=0)
    return (hp.reshape(batch, seq, d_model), hs,
            cat(gla_p), cat(ret_p), cat(gla_s), cat(ret_s))
```

```python
import functools

import jax
import jax.numpy as jnp
from jax import lax
from jax.experimental import pallas as pl
from jax.experimental.pallas import tpu as pltpu

GLA_HEADS = 4
RET_HEADS = 8
HEAD_GROUPS = 4
GLA_TAU = 16.0
LOG2_E = 1.4426950408889634
ROPE_BASE = 10000.0
LN_EPS = 1e-5
HEAD_NORM_EPS = 1e-6
PAST_LEN = 16384

LANES = 128
SUBLANES = 8
VMEM_LIMIT_BYTES = 56 * 1024 * 1024

PROMPT_CHUNK = 128
GLA_SUB = 16
SAMPLE_SEQS = 4

_F32 = jnp.float32
_BF16 = jnp.bfloat16
_NT = (((1,), (1,)), ((), ()))
_TN = (((0,), (0,)), ((), ()))


def _dot(a, b):
    return jnp.dot(a.astype(_BF16), b.astype(_BF16), preferred_element_type=_F32)


def _dot_nt(a, b):
    return lax.dot_general(a.astype(_BF16), b.astype(_BF16), _NT, preferred_element_type=_F32)


def _dot_tn(a, b):
    return lax.dot_general(a.astype(_BF16), b.astype(_BF16), _TN, preferred_element_type=_F32)


def _log_sigmoid(x):
    return jnp.minimum(x, 0.0) - jnp.log(1.0 + jnp.exp(-jnp.abs(x)))


def _silu(x):
    return x * jax.nn.sigmoid(x)


def _iota(shape, axis):
    return lax.broadcasted_iota(jnp.int32, shape, axis)


def _seg_cumsum(x, seg):
    pos = _iota(x.shape, 0) & (seg - 1)
    s = 1
    while s < seg:
        x = x + jnp.where(pos >= s, pltpu.roll(x, s, axis=0), 0.0)
        s *= 2
    return x


def _cumsum_rows_mxu(x):
    n = x.shape[0]
    hi = x.astype(_BF16)
    r1 = x - hi.astype(_F32)
    mid = r1.astype(_BF16)
    lo = (r1 - mid.astype(_F32)).astype(_BF16)
    tri3 = (_iota((n, 3 * n), 0) >= (_iota((n, 3 * n), 1) & (n - 1))).astype(_BF16)
    return jnp.dot(tri3, jnp.concatenate([hi, mid, lo], axis=0), preferred_element_type=_F32)


def _rows_to_cols(rows):
    n = rows[0].shape[1]
    rid = _iota((LANES, n), 0)
    stack = jnp.zeros((LANES, n), _F32)
    for i, r in enumerate(rows):
        stack = jnp.where(rid == i, r, stack)
    return stack.T


def _rope(x, cos, sin_signed):
    w = x.shape[1]
    even = (_iota(x.shape, 1) & 1) == 0
    swapped = jnp.where(even, pltpu.roll(x, w - 1, axis=1), pltpu.roll(x, 1, axis=1))
    return x * cos + swapped * sin_signed


def _gla_pair_scores(a, q, k, b2, lane0, row_lo_hi):
    n = q.shape[0]
    lanes = _iota((SUBLANES, a.shape[1]), 1)
    tiles = [a[r:r + SUBLANES] for r in range(0, n, SUBLANES)]
    for s in range(n):
        lo, hi = row_lo_hi(s)
        for j, r0 in enumerate(range(0, n, SUBLANES)):
            if r0 + SUBLANES <= lo or r0 >= hi:
                continue
            e = jnp.exp2(b2[r0:r0 + SUBLANES] - b2[s:s + 1, :])
            col = jnp.sum(q[r0:r0 + SUBLANES] * k[s:s + 1, :] * e, axis=-1, keepdims=True)
            take = lanes == lane0 + s
            if lo > r0 or hi < r0 + SUBLANES:
                rows = _iota((SUBLANES, 1), 0) + r0
                take = take & (rows >= lo) & (rows < hi)
            tiles[j] = jnp.where(take, col, tiles[j])
    return jnp.concatenate(tiles, axis=0)


def _rms_heads(o, w):
    return o * lax.rsqrt(jnp.mean(o * o, axis=-1, keepdims=True) + HEAD_NORM_EPS) * w


def _group_norm(o, w, b):
    mu = jnp.mean(o, axis=-1, keepdims=True)
    d = o - mu
    var = jnp.mean(d * d, axis=-1, keepdims=True)
    return d * lax.rsqrt(var + HEAD_NORM_EPS) * w + b


def _prep_x_kernel(x_ref, wcol_ref, wlr_ref, blr_ref, xb_ref, la_ref, *, rank):
    x = x_ref[...]
    if x.ndim == 3:
        x = x.reshape(x.shape[0] * x.shape[1], x.shape[2])
    xb = x.astype(_BF16)
    xb_ref[...] = xb
    wrow = wcol_ref[0]
    w_lra = jnp.concatenate([wrow, jnp.zeros((LANES - rank, wrow.shape[1]), _F32)], axis=0)
    lr = _dot_nt(xb, w_lra)
    w = wlr_ref[...]
    w_hi = w.astype(_BF16).astype(_F32)
    w_lo = w - w_hi
    lr_hi = lr.astype(_BF16).astype(_F32)
    lr_lo = lr - lr_hi
    lr_cat = lr_hi + pltpu.roll(lr_hi, rank, axis=1) + pltpu.roll(lr_lo, 2 * rank, axis=1)
    w_cat = jnp.concatenate([w_hi[:rank], w_lo[:rank], w_hi[:rank],
                             jnp.zeros((LANES - 3 * rank, w.shape[1]), _F32)], axis=0)
    logit = _dot(lr_cat, w_cat) + blr_ref[...]
    la_ref[...] = _log_sigmoid(logit) * (1.0 / GLA_TAU)


def _prep_x(x, w_t, layer, lr_col, rank, w_lr, b_lr, tm):
    kdim = x.shape[-1]
    m = x.size // kdim
    qk = w_lr.shape[1]
    assert lr_col % rank == 0 and rank % SUBLANES == 0 and 3 * rank <= LANES
    whole = lambda a: pl.BlockSpec(a.shape, lambda i: (0,) * a.ndim)
    if x.ndim == 3:
        x_spec = pl.BlockSpec((tm // x.shape[1], x.shape[1], kdim), lambda i: (i, 0, 0))
    else:
        x_spec = pl.BlockSpec((tm, kdim), lambda i: (i, 0))
    return pl.pallas_call(
        functools.partial(_prep_x_kernel, rank=rank),
        grid=(m // tm,),
        in_specs=[x_spec,
                  pl.BlockSpec((1, rank, kdim), lambda i: (layer, lr_col // rank, 0)),
                  whole(w_lr), whole(b_lr)],
        out_specs=[pl.BlockSpec((tm, kdim), lambda i: (i, 0)),
                   pl.BlockSpec((tm, qk), lambda i: (i, 0))],
        out_shape=[jax.ShapeDtypeStruct((m, kdim), _BF16),
                   jax.ShapeDtypeStruct((m, qk), _F32)],
        compiler_params=pltpu.CompilerParams(
            dimension_semantics=("arbitrary",), vmem_limit_bytes=VMEM_LIMIT_BYTES),
        name="prep_x",
    )(x, w_t, w_lr, b_lr)


N_SAMPLE_IN = 19


def _proj_in_kernel(*refs, n_plain, shift, silu_tiles, sigmoid_tiles, sample_steps):
    x_ref, wa_ref, we_ref = refs[:3]
    wbf_ref = refs[-1]
    if sample_steps:
        unit_in = refs[3:3 + N_SAMPLE_IN]
        h_ref = refs[3 + N_SAMPLE_IN]
        unit_out = refs[4 + N_SAMPLE_IN:7 + N_SAMPLE_IN]
    else:
        h_ref = refs[3]
    j = pl.program_id(0)
    i = pl.program_id(1)
    tn, kdim = wbf_ref.shape

    @pl.when(i == 0)
    def _():
        @pl.when(j < n_plain)
        def _():
            wbf_ref[...] = wa_ref[0].astype(_BF16)

        @pl.when(j >= n_plain)
        def _():
            wbf_ref[0:tn - shift, :] = wa_ref[0, shift:tn, :].astype(_BF16)
            wbf_ref[tn - shift:tn, :] = we_ref[0].astype(_BF16)

    is_silu = functools.reduce(jnp.logical_or, [j == t for t in silu_tiles])
    is_sigmoid = functools.reduce(jnp.logical_or, [j == t for t in sigmoid_tiles])

    def slab(cols, activated):
        acc = lax.dot_general(x_ref[...], wbf_ref[cols, :], _NT, preferred_element_type=_F32)
        if activated:
            s = 0.5 * jnp.tanh(0.5 * acc) + 0.5
            acc = jnp.where(is_sigmoid, s, acc * s)
        h_ref[:, cols] = acc

    def tile(activated):
        finish_unit = None
        if sample_steps:
            finish_unit = _mixer_sample_unit(*unit_in, *unit_out, steps=sample_steps)
        slab(slice(0, tn // 4), activated)
        if finish_unit is not None:
            finish_unit()
        slab(slice(tn // 4, tn), activated)

    @pl.when(is_silu | is_sigmoid)
    def _():
        tile(True)

    @pl.when(jnp.logical_not(is_silu | is_sigmoid))
    def _():
        tile(False)


PROJ_TILES = 16


def _proj_in(x, w_t, layer, rank, tm, sample=None):
    m, kdim = x.shape
    d_model = kdim
    tn = d_model // 2
    nj = PROJ_TILES
    n_plain = 6
    bf16_rows = 2 * SUBLANES
    assert rank % bf16_rows == 0 and tn % rank == 0 and m % tm == 0
    ni = m // tm

    def row_block(j, i):
        return jnp.where(j % 2 == 0, i, ni - 1 - i)
    in_specs = [
        pl.BlockSpec((tm, kdim), lambda j, i: (row_block(j, i), 0)),
        pl.BlockSpec((1, tn, kdim), lambda j, i: (layer, j, 0)),
        pl.BlockSpec((1, rank, kdim), lambda j, i: (layer, (j + 1) * (tn // rank), 0)),
    ]
    out_specs = [pl.BlockSpec((tm, tn), lambda j, i: (row_block(j, i), j))]
    out_shape = [jax.ShapeDtypeStruct((m, nj * tn), _F32)]
    operands = [x, w_t, w_t]
    sample_steps = 0
    if sample is not None:
        unit_operands, sample_steps, n_units, unit_specs = sample
        assert n_units <= nj * ni

        def unit_of(j, i):
            u = jnp.minimum(j * ni + i, n_units - 1)
            return u // HEAD_GROUPS, u % HEAD_GROUPS
        unit_in, unit_out, unit_shape = unit_specs(unit_of)
        assert len(unit_in) == N_SAMPLE_IN
        in_specs += unit_in
        out_specs += unit_out
        out_shape += unit_shape
        operands += unit_operands
    kern = functools.partial(_proj_in_kernel, n_plain=n_plain, shift=rank,
                             silu_tiles=(4, 5, 10, 11), sigmoid_tiles=(12, 13, 14, 15),
                             sample_steps=sample_steps)
    return pl.pallas_call(
        kern,
        grid=(nj, ni),
        in_specs=in_specs,
        out_specs=out_specs,
        out_shape=out_shape,
        scratch_shapes=[pltpu.VMEM((tn, kdim), _BF16)],
        compiler_params=pltpu.CompilerParams(
            dimension_semantics=("arbitrary", "arbitrary"),
            vmem_limit_bytes=VMEM_LIMIT_BYTES),
        name="proj_in_mix" if sample is not None else "proj_in",
    )(*operands)


def _h_cols(d_model, g):
    wq, wv = d_model // 8, d_model // 4
    d = d_model
    offs = dict(qa=(0, wq), ka=(d // 2, wq), va=(d, wv), za=(2 * d, wv),
                qb=(3 * d, wq), kb=(3 * d + d // 2, wq), vb=(4 * d, wv), zb=(5 * d, wv),
                ga=(6 * d, wv), gb=(7 * d, wv))
    return {n: slice(o + g * w, o + (g + 1) * w) for n, (o, w) in offs.items()}


def _mixer_prompt_kernel(h_ref, la_ref, gnw_ref, rnw_ref, rnb_ref,
                         cos_ref, sin_ref, lg_ref, wout_ref, m_ref, sg_ref, sr_ref, wout_bf_ref,
                         qdec_ref, kdec_ref, dmat_ref):
    wout_bf_ref[...] = wout_ref[0].astype(_BF16)
    chunk = h_ref.shape[0]
    d_model = m_ref.shape[1]
    dk_a, dv_a = sg_ref.shape[3:]
    dk_b, dv_b = sr_ref.shape[3:]

    @pl.when(pl.program_id(1) == 0)
    def _():
        sg_ref[...] = jnp.zeros(sg_ref.shape, _F32)
        sr_ref[...] = jnp.zeros(sr_ref.shape, _F32)

    @pl.when((pl.program_id(0) == 0) & (pl.program_id(1) == 0))
    def _():
        tpos = (_iota((chunk, 2 * dk_b), 0) + 1).astype(_F32)
        tdiff = _iota((chunk, chunk), 0) - _iota((chunk, chunk), 1)
        for g in range(HEAD_GROUPS):
            lg = lg_ref[g]
            qdec_ref[g] = jnp.exp(lg * tpos)
            kdec_ref[g] = jnp.exp(lg * (chunk - tpos))
            for hh in range(2):
                lg1 = lg[:, hh * dk_b:hh * dk_b + 1]
                dmat_ref[2 * g + hh] = jnp.exp(
                    jnp.where(tdiff >= 0, lg1 * tdiff.astype(_F32), -jnp.inf))

    rows = _iota((chunk, 1), 0)
    cos, sin = cos_ref[...], sin_ref[...]
    n_sub = chunk // GLA_SUB
    groups = range(HEAD_GROUPS)
    cols = [_h_cols(d_model, g) for g in groups]
    ksl = [slice(hh * dk_b, (hh + 1) * dk_b) for hh in range(2)]
    vsl = [slice(hh * dv_b, (hh + 1) * dv_b) for hh in range(2)]


    gla, ret = [], []
    for g in groups:
        c = cols[g]
        b = _cumsum_rows_mxu(la_ref[:, g * dk_a:(g + 1) * dk_a]) * LOG2_E
        q = h_ref[:, c["qa"]] * (dk_a ** -0.5)
        k = h_ref[:, c["ka"]]
        v = h_ref[:, c["va"]].astype(_BF16)
        s_gla = sg_ref[0, 0, g]
        gla.append(dict(b=b, q=q, k=k, v=v, s=s_gla, o=_dot(q * jnp.exp2(b), s_gla)))

        qb = _rope(h_ref[:, c["qb"]], cos, sin)
        kb = _rope(h_ref[:, c["kb"]], cos, sin) * (dk_b ** -0.5)
        vb = h_ref[:, c["vb"]].astype(_BF16)
        lg = lg_ref[g]
        q_dec = qb * qdec_ref[g]
        s_ret = [sr_ref[0, 0, 2 * g + hh] for hh in range(2)]
        ret.append(dict(
            vb=vb, lg=lg, s=s_ret, k_dec=kb * kdec_ref[g],
            o=[_dot(q_dec[:, ksl[hh]], s_ret[hh]) for hh in range(2)],
            qk=[_dot_nt(qb[:, ksl[hh]], kb[:, ksl[hh]]) for hh in range(2)]))

    for g in groups:
        q, k, b = gla[g]["q"], gla[g]["k"], gla[g]["b"]
        a_off = [jnp.zeros((GLA_SUB, chunk), _F32)]
        for i in range(1, n_sub):
            lo = i * GLA_SUB
            r = b[lo - 1:lo, :]
            k_pre = k * jnp.exp2(jnp.where(rows < lo, r - b, -jnp.inf))
            a_off.append(_dot_nt(q[lo:lo + GLA_SUB] * jnp.exp2(b[lo:lo + GLA_SUB] - r), k_pre))
        gla[g]["a"] = a_off

    for g in groups:
        q, k, b, v = gla[g]["q"], gla[g]["k"], gla[g]["b"], gla[g]["v"]
        intra = []
        for i in range(n_sub):
            lo = i * GLA_SUB
            sub = slice(lo, lo + GLA_SUB)
            a = _gla_pair_scores(gla[g]["a"][i], q[sub], k[sub], b[sub], lo, lambda s: (s, GLA_SUB))
            intra.append(_dot(a, v))
        gla[g]["o"] = gla[g]["o"] + jnp.concatenate(intra, axis=0)
        for hh in range(2):
            scores = ret[g]["qk"][hh] * dmat_ref[2 * g + hh]
            ret[g]["o"][hh] = ret[g]["o"][hh] + _dot(scores, ret[g]["vb"][:, vsl[hh]])

    for g in groups:
        k, b, v = gla[g]["k"], gla[g]["b"], gla[g]["v"]
        b_last = b[chunk - 1:chunk, :]
        decay_col = _rows_to_cols([jnp.exp2(b_last)])[:, 0:1]
        sg_ref[0, 0, g] = decay_col * gla[g]["s"] + _dot_tn(k * jnp.exp2(b_last - b), v)
        for hh in range(2):
            lg1 = ret[g]["lg"][:, hh * dk_b:hh * dk_b + 1]
            sr_ref[0, 0, 2 * g + hh] = (jnp.exp(lg1 * chunk) * ret[g]["s"][hh]
                                        + _dot_tn(ret[g]["k_dec"][:, ksl[hh]], ret[g]["vb"][:, vsl[hh]]))

    for g in groups:
        c = cols[g]
        gv = slice(g * dv_a, (g + 1) * dv_a)
        o_a = _rms_heads(gla[g]["o"], gnw_ref[...]) * h_ref[:, c["za"]]
        o_b = []
        for hh in range(2):
            nsl = slice(g * dv_a + hh * dv_b, g * dv_a + (hh + 1) * dv_b)
            o_b.append(_group_norm(ret[g]["o"][hh], rnw_ref[:, nsl], rnb_ref[:, nsl]))
        o_b = jnp.concatenate(o_b, axis=1) * h_ref[:, c["zb"]]
        merged = h_ref[:, c["ga"]] * o_a + h_ref[:, c["gb"]] * o_b
        m_ref[:, gv] = merged.astype(m_ref.dtype)


def _mixer_prompt(h, la, params, rope, lg, w_out, layer, batch, seq, d_model):
    wq, wv = d_model // 8, d_model // 4
    dk_a, dv_a = wq, wv
    dk_b, dv_b = wq // 2, wv // 2
    chunk = PROMPT_CHUNK
    nc = seq // chunk
    e_rows = w_out.shape[1]
    bf16_rows = 2 * SUBLANES
    assert e_rows % (batch * nc) == 0 and (e_rows // (batch * nc)) % bf16_rows == 0
    w_rows = e_rows // (batch * nc)
    whole = lambda a: pl.BlockSpec(a.shape, lambda b, c: (0,) * a.ndim)
    in_specs = [pl.BlockSpec((chunk, h.shape[1]), lambda b, c: (b * nc + c, 0)),
                pl.BlockSpec((chunk, la.shape[1]), lambda b, c: (b * nc + c, 0))]
    in_specs += [whole(p) for p in params]
    in_specs += [pl.BlockSpec((chunk, wq), lambda b, c: (c, 0)),
                 pl.BlockSpec((chunk, wq), lambda b, c: (c, 0)),
                 whole(lg),
                 pl.BlockSpec((1, w_rows, w_out.shape[2]), lambda b, c: (layer, b * nc + c, 0))]
    out_specs = [
        pl.BlockSpec((chunk, d_model), lambda b, c: (b * nc + c, 0)),
        pl.BlockSpec((1, 1, GLA_HEADS, dk_a, dv_a), lambda b, c: (0, b, 0, 0, 0)),
        pl.BlockSpec((1, 1, RET_HEADS, dk_b, dv_b), lambda b, c: (0, b, 0, 0, 0)),
        pl.BlockSpec((w_rows, w_out.shape[2]), lambda b, c: (b * nc + c, 0)),
    ]
    out_shape = [
        jax.ShapeDtypeStruct((batch * seq, d_model), _BF16),
        jax.ShapeDtypeStruct((1, batch, GLA_HEADS, dk_a, dv_a), _F32),
        jax.ShapeDtypeStruct((1, batch, RET_HEADS, dk_b, dv_b), _F32),
        jax.ShapeDtypeStruct(w_out.shape[1:], _BF16),
    ]
    return pl.pallas_call(
        _mixer_prompt_kernel,
        grid=(batch, nc),
        in_specs=in_specs,
        out_specs=out_specs,
        out_shape=out_shape,
        scratch_shapes=[pltpu.VMEM((HEAD_GROUPS, chunk, wq), _F32),
                        pltpu.VMEM((HEAD_GROUPS, chunk, wq), _F32),
                        pltpu.VMEM((RET_HEADS, chunk, chunk), _F32)],
        compiler_params=pltpu.CompilerParams(
            dimension_semantics=("arbitrary", "arbitrary"),
            vmem_limit_bytes=VMEM_LIMIT_BYTES),
        name="mixer_prompt",
    )(h, la, *params, *rope, lg, w_out)


def _mixer_sample_unit(qa_ref, ka_ref, va_ref, za_ref, qb_ref, kb_ref, vb_ref, zb_ref,
                       ga_ref, gb_ref, la_ref, gnw_ref, rnw_ref, rnb_ref,
                       cos_ref, sin_ref, lg_ref, sg0_ref, sr0_ref, m_ref, sg_ref, sr_ref,
                       *, steps):
    rows_n, dk_a = qa_ref.shape
    nseq = rows_n // steps
    dk_b = qb_ref.shape[1] // 2
    dv_b = vb_ref.shape[1] // 2
    rows = _iota((rows_n, 1), 0)
    rows_p = _iota((LANES, 1), 0)
    pad = LANES - rows_n
    ksl = [slice(hh * dk_b, (hh + 1) * dk_b) for hh in range(2)]
    vsl = [slice(hh * dv_b, (hh + 1) * dv_b) for hh in range(2)]

    def pad_rows(x):
        return jnp.concatenate([x, jnp.zeros((pad, x.shape[1]), x.dtype)], axis=0)

    def seq_rows(x, n, r=rows):
        return jnp.where((r >= n * steps) & (r < (n + 1) * steps), x, 0.0)

    b = _seg_cumsum(la_ref[...], steps) * LOG2_E
    q = qa_ref[...] * (dk_a ** -0.5)
    k = ka_ref[...]
    v = pad_rows(va_ref[...]).astype(_BF16)
    q_dec = q * jnp.exp2(b)
    s_gla = [sg0_ref[0, n, 0] for n in range(nseq)]
    o_a = _dot(seq_rows(q_dec, 0), s_gla[0])
    for n in range(1, nseq):
        o_a = o_a + _dot(seq_rows(q_dec, n), s_gla[n])

    cos, sin = cos_ref[...], sin_ref[...]
    qb = _rope(qb_ref[...], cos, sin)
    kb = _rope(kb_ref[...], cos, sin) * (dk_b ** -0.5)
    vb = pad_rows(vb_ref[...]).astype(_BF16)
    lg = lg_ref[0]
    lg1 = [lg[:, hh * dk_b:hh * dk_b + 1] for hh in range(2)]
    tpos = ((_iota(qb.shape, 0) & (steps - 1)) + 1).astype(_F32)
    qb_dec = qb * jnp.exp(lg * tpos)
    kb_p = pad_rows(kb)
    s_ret = [[sr0_ref[0, n, hh] for n in range(nseq)] for hh in range(2)]
    o_r, qk = [], []
    for hh in range(2):
        o_h = _dot(seq_rows(qb_dec[:, ksl[hh]], 0), s_ret[hh][0])
        for n in range(1, nseq):
            o_h = o_h + _dot(seq_rows(qb_dec[:, ksl[hh]], n), s_ret[hh][n])
        o_r.append(o_h)
        qk.append(_dot_nt(qb[:, ksl[hh]], kb_p[:, ksl[hh]]))

    a = jnp.zeros((rows_n, LANES), _F32)
    a = _gla_pair_scores(a, q, k, b, 0, lambda s: (s, (s // steps + 1) * steps))
    o_a = o_a + _dot(a, v)
    r_i = _iota((rows_n, LANES), 0)
    c_i = _iota((rows_n, LANES), 1)
    same_seq = (r_i >> (steps.bit_length() - 1)) == (c_i >> (steps.bit_length() - 1))
    tdiff = r_i - c_i
    for hh in range(2):
        decay = jnp.exp(jnp.where((tdiff >= 0) & same_seq, lg1[hh] * tdiff.astype(_F32), -jnp.inf))
        o_r[hh] = o_r[hh] + _dot(qk[hh] * decay, vb[:, vsl[hh]])

    def finish():
        b_last = [b[(n + 1) * steps - 1:(n + 1) * steps, :] for n in range(nseq)]
        b_end = jnp.zeros_like(b)
        for n in range(nseq):
            b_end = b_end + seq_rows(jnp.broadcast_to(b_last[n], b.shape), n)
        k_dec = k * jnp.exp2(b_end - b)
        decay_cols = _rows_to_cols([jnp.exp2(r) for r in b_last])
        for n in range(nseq):
            sg_ref[0, n, 0] = (decay_cols[:, n:n + 1] * s_gla[n]
                               + _dot_tn(seq_rows(k_dec, n), v[:rows_n]))
        kb_dec = kb * jnp.exp(lg * (steps - tpos))
        for hh in range(2):
            for n in range(nseq):
                sr_ref[0, n, hh] = (jnp.exp(lg1[hh] * steps) * s_ret[hh][n]
                                    + _dot_tn(seq_rows(kb_dec[:, ksl[hh]], n), vb[:rows_n, vsl[hh]]))
        o_an = _rms_heads(o_a, gnw_ref[...]) * za_ref[...]
        o_bn = jnp.concatenate([_group_norm(o_r[hh], rnw_ref[:, vsl[hh]], rnb_ref[:, vsl[hh]])
                                for hh in range(2)], axis=1) * zb_ref[...]
        merged = ga_ref[...] * o_an + gb_ref[...] * o_bn
        m_ref[...] = merged.astype(m_ref.dtype)

    return finish


def _sample_units(h, la, params, rope, lg, state_gla, state_ret, layer, nseq_total, steps, d_model):
    wq, wv = d_model // 8, d_model // 4
    dk_a, dv_a = wq, wv
    dk_b, dv_b = wq // 2, wv // 2
    nseq = SAMPLE_SEQS
    rows_n = nseq * steps
    assert steps & (steps - 1) == 0 and nseq_total % nseq == 0
    d = d_model

    def specs(unit_of):
        def at(fn):
            return lambda *ids: fn(*unit_of(*ids))

        def h_spec(width, col_off):
            base = col_off // width
            return pl.BlockSpec((rows_n, width), at(lambda p, g: (p, base + g)))
        in_specs = [h_spec(wq, 0), h_spec(wq, d // 2), h_spec(wv, d), h_spec(wv, 2 * d),
                    h_spec(wq, 3 * d), h_spec(wq, 3 * d + d // 2), h_spec(wv, 4 * d),
                    h_spec(wv, 5 * d), h_spec(wv, 6 * d), h_spec(wv, 7 * d)]
        in_specs += [
            pl.BlockSpec((rows_n, wq), at(lambda p, g: (p, g))),
            pl.BlockSpec((1, wv), at(lambda p, g: (0, 0))),
            pl.BlockSpec((1, wv), at(lambda p, g: (0, g))),
            pl.BlockSpec((1, wv), at(lambda p, g: (0, g))),
            pl.BlockSpec((rows_n, wq), at(lambda p, g: (0, 0))),
            pl.BlockSpec((rows_n, wq), at(lambda p, g: (0, 0))),
            pl.BlockSpec((1, 1, wq), at(lambda p, g: (g, 0, 0))),
            pl.BlockSpec((1, nseq, 1, dk_a, dv_a), at(lambda p, g: (layer, p, g, 0, 0))),
            pl.BlockSpec((1, nseq, 2, dk_b, dv_b), at(lambda p, g: (layer, p, g, 0, 0))),
        ]
        out_specs = [
            pl.BlockSpec((rows_n, wv), at(lambda p, g: (p, g))),
            pl.BlockSpec((1, nseq, 1, dk_a, dv_a), at(lambda p, g: (0, p, g, 0, 0))),
            pl.BlockSpec((1, nseq, 2, dk_b, dv_b), at(lambda p, g: (0, p, g, 0, 0))),
        ]
        out_shape = [
            jax.ShapeDtypeStruct((nseq_total * steps, d_model), _BF16),
            jax.ShapeDtypeStruct((1, nseq_total, GLA_HEADS, dk_a, dv_a), _F32),
            jax.ShapeDtypeStruct((1, nseq_total, RET_HEADS, dk_b, dv_b), _F32),
        ]
        return in_specs, out_specs, out_shape

    operands = [h] * 10 + [la, *params, *rope, lg, state_gla, state_ret]
    n_units = (nseq_total // nseq) * HEAD_GROUPS
    return operands, steps, n_units, specs


PROJ_OUT_SLABS = 4

def _out_proj_ln(m, x, w, ln_w, ln_b, alpha):
    r = alpha * x + jnp.dot(m, w, preferred_element_type=_F32)
    mu = jnp.mean(r, axis=-1, keepdims=True)
    d = r - mu
    var = jnp.mean(d * d, axis=-1, keepdims=True)
    return d * lax.rsqrt(var + LN_EPS) * ln_w + ln_b


def _proj_out_kernel(m_ref, x_ref, w_ref, lnw_ref, lnb_ref, y_ref, *, alpha):
    tm = m_ref.shape[0]
    slab = min(tm, max(tm // PROJ_OUT_SLABS, LANES))
    for r in range(0, tm, slab):
        if x_ref.ndim == 3:
            steps = x_ref.shape[1]
            seqs = slice(r // steps, (r + slab) // steps)
            x = x_ref[seqs].reshape(slab, x_ref.shape[2])
        else:
            x = x_ref[r:r + slab, :]
        y = _out_proj_ln(m_ref[r:r + slab, :], x, w_ref[...], lnw_ref[...], lnb_ref[...], alpha)
        if x_ref.ndim == 3:
            y_ref[seqs] = y.reshape(slab // steps, steps, y.shape[1])
        else:
            y_ref[r:r + slab, :] = y


def _proj_out(merged, x, w_out, ln_w, ln_b, alpha, tm):
    d = x.shape[-1]
    m = x.size // d
    e = merged.shape[1]
    if x.ndim == 3:
        xy_spec = pl.BlockSpec((tm // x.shape[1], x.shape[1], d), lambda i: (i, 0, 0))
    else:
        xy_spec = pl.BlockSpec((tm, d), lambda i: (i, 0))
    return pl.pallas_call(
        functools.partial(_proj_out_kernel, alpha=alpha),
        grid=(m // tm,),
        in_specs=[
            pl.BlockSpec((tm, e), lambda i: (i, 0)),
            xy_spec,
            pl.BlockSpec((e, d), lambda i: (0, 0)),
            pl.BlockSpec((1, d), lambda i: (0, 0)),
            pl.BlockSpec((1, d), lambda i: (0, 0)),
        ],
        out_specs=xy_spec,
        out_shape=jax.ShapeDtypeStruct(x.shape, _F32),
        compiler_params=pltpu.CompilerParams(
            dimension_semantics=("arbitrary",),
            vmem_limit_bytes=VMEM_LIMIT_BYTES),
        name="proj_out",
    )(merged, x, w_out, ln_w, ln_b)


def _rope_tables(pos, dk):
    inv = 1.0 / (ROPE_BASE ** jnp.linspace(0.0, 1.0, dk // 2, dtype=_F32))
    ang = pos.astype(_F32)[:, None] * inv[None, :]
    cos = jnp.repeat(jnp.cos(ang), 2, axis=1)
    sin = jnp.stack([-jnp.sin(ang), jnp.sin(ang)], axis=-1).reshape(ang.shape[0], dk)
    return jnp.tile(cos, (1, 2)), jnp.tile(sin, (1, 2))


def _pick_tile(n, pref):
    t = min(n, pref)
    while n % t:
        t //= 2
    return t


def kernel(x_prompt, x_sample, state_gla, state_ret, w_in, w_lr, b_lr, gla_norm_w,
           ret_norm_w, ret_norm_b, w_out, ln_w, ln_b):
    depth, d_model, _ = w_in.shape
    batch, seq, _ = x_prompt.shape
    dec_batch, dec_seq, _ = x_sample.shape
    rank = w_lr.shape[1]
    dk_b = d_model // 2 // RET_HEADS
    assert state_gla.shape[2] == GLA_HEADS and state_ret.shape[2] == RET_HEADS
    assert rank <= LANES and seq % PROMPT_CHUNK == 0
    alpha = (2.0 * depth) ** 0.25

    lg_heads = jnp.log(1.0 - 2.0 ** (-5.0 - jnp.arange(RET_HEADS, dtype=_F32)))
    lg = jnp.repeat(lg_heads, dk_b).reshape(HEAD_GROUPS, 1, 2 * dk_b)
    rope_p = _rope_tables(jnp.arange(seq, dtype=jnp.int32), dk_b)
    pos_s = PAST_LEN + jnp.arange(dec_seq, dtype=jnp.int32)
    rope_s = _rope_tables(jnp.tile(pos_s, SAMPLE_SEQS), dk_b)

    hp = x_prompt.reshape(batch * seq, d_model)
    hs = x_sample
    n_rows_s = dec_batch * dec_seq
    w_t = jnp.swapaxes(w_in, 1, 2)
    gla_p, ret_p, gla_s, ret_s = [], [], [], []
    for l in range(depth):
        lr_lo = 3 * d_model
        w_lr_p = jnp.pad(w_lr[l], ((0, LANES - rank), (0, 0)))
        params = (gla_norm_w[l][None, :], ret_norm_w[l][None, :], ret_norm_b[l][None, :])
        lnw, lnb = ln_w[l][None, :], ln_b[l][None, :]

        xp_b, la_p = _prep_x(hp, w_t, l, lr_lo, rank, w_lr_p, b_lr[l][None, :],
                             _pick_tile(hp.shape[0], 512))
        xs_b, la_s = _prep_x(hs, w_t, l, lr_lo, rank, w_lr_p, b_lr[l][None, :],
                             _pick_tile(n_rows_s, 512))
        (h_s,) = _proj_in(xs_b, w_t, l, rank, _pick_tile(xs_b.shape[0], 1024))
        units = _sample_units(h_s, la_s, params, rope_s, lg, state_gla, state_ret, l,
                              dec_batch, dec_seq, d_model)
        h_p, merged_s, sg_s, sr_s = _proj_in(xp_b, w_t, l, rank, _pick_tile(xp_b.shape[0], 1024),
                                             sample=units)

        merged, sg, sr, w_o = _mixer_prompt(h_p, la_p, params, rope_p, lg, w_out, l,
                                            batch, seq, d_model)
        hp = _proj_out(merged, hp, w_o, lnw, lnb, alpha, _pick_tile(hp.shape[0], 512))
        gla_p.append(sg)
        ret_p.append(sr)

        merged, sg, sr = merged_s, sg_s, sr_s
        hs = _proj_out(merged, hs, w_o, lnw, lnb, alpha, _pick_tile(n_rows_s, 256))
        gla_s.append(sg)
        ret_s.append(sr)

    cat = lambda parts: parts[0] if len(parts) == 1 else jnp.concatenate(parts, axis=0)
    return (hp.reshape(batch, seq, d_model), hs,
            cat(gla_p), cat(ret_p), cat(gla_s), cat(ret_s))
```

```python
import functools

import jax
import jax.numpy as jnp
from jax import lax
from jax.experimental import pallas as pl
from jax.experimental.pallas import tpu as pltpu

GLA_HEADS = 4
RET_HEADS = 8
HEAD_GROUPS = 4
GLA_TAU = 16.0
LOG2_E = 1.4426950408889634
ROPE_BASE = 10000.0
LN_EPS = 1e-5
HEAD_NORM_EPS = 1e-6
PAST_LEN = 16384

LANES = 128
SUBLANES = 8
VMEM_LIMIT_BYTES = 60 * 1024 * 1024

PROMPT_CHUNK = 128
GLA_SUB = 16
SAMPLE_SEQS = 4

_F32 = jnp.float32
_BF16 = jnp.bfloat16
_NT = (((1,), (1,)), ((), ()))
_TN = (((0,), (0,)), ((), ()))


def _dot(a, b):
    return jnp.dot(a.astype(_BF16), b.astype(_BF16), preferred_element_type=_F32)


def _dot_nt(a, b):
    return lax.dot_general(a.astype(_BF16), b.astype(_BF16), _NT, preferred_element_type=_F32)


def _dot_tn(a, b):
    return lax.dot_general(a.astype(_BF16), b.astype(_BF16), _TN, preferred_element_type=_F32)


def _log_sigmoid(x):
    return jnp.minimum(x, 0.0) - jnp.log(1.0 + jnp.exp(-jnp.abs(x)))


def _silu(x):
    return x * jax.nn.sigmoid(x)


def _iota(shape, axis):
    return lax.broadcasted_iota(jnp.int32, shape, axis)


def _seg_cumsum(x, seg):
    pos = _iota(x.shape, 0) & (seg - 1)
    s = 1
    while s < seg:
        x = x + jnp.where(pos >= s, pltpu.roll(x, s, axis=0), 0.0)
        s *= 2
    return x


def _cumsum_rows_mxu(x):
    n = x.shape[0]
    hi = x.astype(_BF16)
    r1 = x - hi.astype(_F32)
    mid = r1.astype(_BF16)
    lo = (r1 - mid.astype(_F32)).astype(_BF16)
    tri3 = (_iota((n, 3 * n), 0) >= (_iota((n, 3 * n), 1) & (n - 1))).astype(_BF16)
    return jnp.dot(tri3, jnp.concatenate([hi, mid, lo], axis=0), preferred_element_type=_F32)


def _rows_to_cols(rows):
    n = rows[0].shape[1]
    rid = _iota((LANES, n), 0)
    stack = jnp.zeros((LANES, n), _F32)
    for i, r in enumerate(rows):
        stack = jnp.where(rid == i, r, stack)
    return stack.T


def _rope(x, cos, sin_signed):
    w = x.shape[1]
    even = (_iota(x.shape, 1) & 1) == 0
    swapped = jnp.where(even, pltpu.roll(x, w - 1, axis=1), pltpu.roll(x, 1, axis=1))
    return x * cos + swapped * sin_signed


def _gla_pair_scores(a, q, k, b2, lane0, row_lo_hi):
    n = q.shape[0]
    lanes = _iota((SUBLANES, a.shape[1]), 1)
    tiles = [a[r:r + SUBLANES] for r in range(0, n, SUBLANES)]
    for s in range(n):
        lo, hi = row_lo_hi(s)
        for j, r0 in enumerate(range(0, n, SUBLANES)):
            if r0 + SUBLANES <= lo or r0 >= hi:
                continue
            e = jnp.exp2(b2[r0:r0 + SUBLANES] - b2[s:s + 1, :])
            col = jnp.sum(q[r0:r0 + SUBLANES] * k[s:s + 1, :] * e, axis=-1, keepdims=True)
            take = lanes == lane0 + s
            if lo > r0 or hi < r0 + SUBLANES:
                rows = _iota((SUBLANES, 1), 0) + r0
                take = take & (rows >= lo) & (rows < hi)
            tiles[j] = jnp.where(take, col, tiles[j])
    return jnp.concatenate(tiles, axis=0)


def _rms_heads(o, w):
    return o * lax.rsqrt(jnp.mean(o * o, axis=-1, keepdims=True) + HEAD_NORM_EPS) * w


def _group_norm(o, w, b):
    mu = jnp.mean(o, axis=-1, keepdims=True)
    d = o - mu
    var = jnp.mean(d * d, axis=-1, keepdims=True)
    return d * lax.rsqrt(var + HEAD_NORM_EPS) * w + b


def _prep_x_kernel(x_ref, wcol_ref, wlr_ref, blr_ref, xb_ref, la_ref, *, rank):
    x = x_ref[...]
    if x.ndim == 3:
        x = x.reshape(x.shape[0] * x.shape[1], x.shape[2])
    xb = x.astype(_BF16)
    xb_ref[...] = xb
    wrow = wcol_ref[0]
    w_lra = jnp.concatenate([wrow, jnp.zeros((LANES - rank, wrow.shape[1]), _F32)], axis=0)
    lr = _dot_nt(xb, w_lra)
    w = wlr_ref[...]
    w_hi = w.astype(_BF16).astype(_F32)
    w_lo = w - w_hi
    lr_hi = lr.astype(_BF16).astype(_F32)
    lr_lo = lr - lr_hi
    lr_cat = lr_hi + pltpu.roll(lr_hi, rank, axis=1) + pltpu.roll(lr_lo, 2 * rank, axis=1)
    w_cat = jnp.concatenate([w_hi[:rank], w_lo[:rank], w_hi[:rank],
                             jnp.zeros((LANES - 3 * rank, w.shape[1]), _F32)], axis=0)
    logit = _dot(lr_cat, w_cat) + blr_ref[...]
    la_ref[...] = _log_sigmoid(logit) * (1.0 / GLA_TAU)


def _prep_x(x, w_t, layer, lr_col, rank, w_lr, b_lr, tm):
    kdim = x.shape[-1]
    m = x.size // kdim
    qk = w_lr.shape[1]
    assert lr_col % rank == 0 and rank % SUBLANES == 0 and 3 * rank <= LANES
    whole = lambda a: pl.BlockSpec(a.shape, lambda i: (0,) * a.ndim)
    if x.ndim == 3:
        x_spec = pl.BlockSpec((tm // x.shape[1], x.shape[1], kdim), lambda i: (i, 0, 0))
    else:
        x_spec = pl.BlockSpec((tm, kdim), lambda i: (i, 0))
    return pl.pallas_call(
        functools.partial(_prep_x_kernel, rank=rank),
        grid=(m // tm,),
        in_specs=[x_spec,
                  pl.BlockSpec((1, rank, kdim), lambda i: (layer, lr_col // rank, 0)),
                  whole(w_lr), whole(b_lr)],
        out_specs=[pl.BlockSpec((tm, kdim), lambda i: (i, 0)),
                   pl.BlockSpec((tm, qk), lambda i: (i, 0))],
        out_shape=[jax.ShapeDtypeStruct((m, kdim), _BF16),
                   jax.ShapeDtypeStruct((m, qk), _F32)],
        compiler_params=pltpu.CompilerParams(
            dimension_semantics=("arbitrary",), vmem_limit_bytes=VMEM_LIMIT_BYTES),
        name="prep_x",
    )(x, w_t, w_lr, b_lr)


N_SAMPLE_IN = 19


def _proj_in_kernel(*refs, n_plain, shift, silu_tiles, sigmoid_tiles, sample_steps):
    x_ref, wa_ref, we_ref = refs[:3]
    wbf_ref = refs[-1]
    if sample_steps:
        unit_in = refs[3:3 + N_SAMPLE_IN]
        hf_ref, hv_ref = refs[3 + N_SAMPLE_IN:5 + N_SAMPLE_IN]
        unit_out = refs[5 + N_SAMPLE_IN:8 + N_SAMPLE_IN]
    else:
        hf_ref, hv_ref = refs[3:5]
    j = pl.program_id(0)
    i = pl.program_id(1)
    tn, kdim = wbf_ref.shape

    @pl.when(i == 0)
    def _():
        @pl.when(j < n_plain)
        def _():
            wbf_ref[...] = wa_ref[0].astype(_BF16)

        @pl.when(j >= n_plain)
        def _():
            wbf_ref[0:tn - shift, :] = wa_ref[0, shift:tn, :].astype(_BF16)
            wbf_ref[tn - shift:tn, :] = we_ref[0].astype(_BF16)

    is_silu = functools.reduce(jnp.logical_or, [j == t for t in silu_tiles])
    is_sigmoid = functools.reduce(jnp.logical_or, [j == t for t in sigmoid_tiles])

    is_value = functools.reduce(jnp.logical_or, [j == t for t in V_TILES])

    def slab(cols, kind):
        acc = lax.dot_general(x_ref[...], wbf_ref[cols, :], _NT, preferred_element_type=_F32)
        if kind == "gate":
            s = 0.5 * jnp.tanh(0.5 * acc) + 0.5
            acc = jnp.where(is_sigmoid, s, acc * s)
        if kind == "value":
            hv_ref[:, cols] = acc.astype(hv_ref.dtype)
        else:
            hf_ref[:, cols] = acc

    def tile(kind):
        finish_unit = None
        if sample_steps:
            finish_unit = _mixer_sample_unit(*unit_in, *unit_out, steps=sample_steps)
        slab(slice(0, tn // 4), kind)
        if finish_unit is not None:
            finish_unit()
        slab(slice(tn // 4, tn), kind)

    @pl.when(is_silu | is_sigmoid)
    def _():
        tile("gate")

    @pl.when(is_value)
    def _():
        tile("value")

    @pl.when(jnp.logical_not(is_silu | is_sigmoid | is_value))
    def _():
        tile("plain")


PROJ_TILES = 16
V_TILES = (2, 3, 8, 9)
F_TILES = tuple(t for t in range(PROJ_TILES) if t not in V_TILES)


def _held_tile_maps(active, ni):
    cols, rows = [], []
    for j in range(PROJ_TILES):
        done = [t for t in active if t <= j]
        if done:
            t = done[-1]
            cols.append(active.index(t))
            rows.append(ni - 1 if t % 2 == 0 else 0)
        else:
            cols.append(0)
            rows.append(0 if active[0] % 2 == 0 else ni - 1)
    return cols, rows


def _pick(j, table):
    r = table[0]
    for t in range(1, len(table)):
        if table[t] != table[t - 1]:
            r = jnp.where(j >= t, table[t], r)
    return r


def _proj_in(x, w_t, layer, rank, tm, sample=None):
    m, kdim = x.shape
    d_model = kdim
    tn = d_model // 2
    nj = PROJ_TILES
    n_plain = 6
    bf16_rows = 2 * SUBLANES
    assert rank % bf16_rows == 0 and tn % rank == 0 and m % tm == 0
    ni = m // tm

    def row_block(j, i):
        return jnp.where(j % 2 == 0, i, ni - 1 - i)
    in_specs = [
        pl.BlockSpec((tm, kdim), lambda j, i: (row_block(j, i), 0)),
        pl.BlockSpec((1, tn, kdim), lambda j, i: (layer, j, 0)),
        pl.BlockSpec((1, rank, kdim), lambda j, i: (layer, (j + 1) * (tn // rank), 0)),
    ]
    def out_spec(active):
        cols, rows = _held_tile_maps(active, ni)

        def index(j, i):
            is_active = functools.reduce(jnp.logical_or, [j == t for t in active])
            return jnp.where(is_active, row_block(j, i), _pick(j, rows)), _pick(j, cols)
        return pl.BlockSpec((tm, tn), index)
    out_specs = [out_spec(F_TILES), out_spec(V_TILES)]
    out_shape = [jax.ShapeDtypeStruct((m, len(F_TILES) * tn), _F32),
                 jax.ShapeDtypeStruct((m, len(V_TILES) * tn), _BF16)]
    operands = [x, w_t, w_t]
    sample_steps = 0
    if sample is not None:
        unit_operands, sample_steps, n_units, unit_specs = sample
        assert n_units <= nj * ni

        def unit_of(j, i):
            u = jnp.minimum(j * ni + i, n_units - 1)
            return u // HEAD_GROUPS, u % HEAD_GROUPS
        unit_in, unit_out, unit_shape = unit_specs(unit_of)
        assert len(unit_in) == N_SAMPLE_IN
        in_specs += unit_in
        out_specs += unit_out
        out_shape += unit_shape
        operands += unit_operands
    kern = functools.partial(_proj_in_kernel, n_plain=n_plain, shift=rank,
                             silu_tiles=(4, 5, 10, 11), sigmoid_tiles=(12, 13, 14, 15),
                             sample_steps=sample_steps)
    return pl.pallas_call(
        kern,
        grid=(nj, ni),
        in_specs=in_specs,
        out_specs=out_specs,
        out_shape=out_shape,
        scratch_shapes=[pltpu.VMEM((tn, kdim), _BF16)],
        compiler_params=pltpu.CompilerParams(
            dimension_semantics=("arbitrary", "arbitrary"),
            vmem_limit_bytes=VMEM_LIMIT_BYTES),
        name="proj_in_mix" if sample is not None else "proj_in",
    )(*operands)


def _h_offsets(d_model):
    wq, wv = d_model // 8, d_model // 4
    d = d_model
    h_f = dict(qa=(0, wq), ka=(d // 2, wq), za=(d, wv), qb=(2 * d, wq), kb=(2 * d + d // 2, wq),
               zb=(3 * d, wv), ga=(4 * d, wv), gb=(5 * d, wv))
    h_v = dict(va=(0, wv), vb=(d, wv))
    return h_f, h_v


def _h_cols(d_model, g):
    h_f, h_v = _h_offsets(d_model)
    return {n: slice(o + g * w, o + (g + 1) * w) for n, (o, w) in {**h_f, **h_v}.items()}


def _mixer_prompt_kernel(h_ref, hv_ref, la_ref, gnw_ref, rnw_ref, rnb_ref,
                         cos_ref, sin_ref, lg_ref, wout_ref, m_ref, sg_ref, sr_ref, wout_bf_ref,
                         qdec_ref, kdec_ref, dmat_ref):
    wout_bf_ref[...] = wout_ref[0].astype(_BF16)
    chunk = h_ref.shape[0]
    d_model = m_ref.shape[1]
    dk_a, dv_a = sg_ref.shape[3:]
    dk_b, dv_b = sr_ref.shape[3:]

    @pl.when(pl.program_id(1) == 0)
    def _():
        sg_ref[...] = jnp.zeros(sg_ref.shape, _F32)
        sr_ref[...] = jnp.zeros(sr_ref.shape, _F32)

    @pl.when((pl.program_id(0) == 0) & (pl.program_id(1) == 0))
    def _():
        tpos = (_iota((chunk, 2 * dk_b), 0) + 1).astype(_F32)
        tdiff = _iota((chunk, chunk), 0) - _iota((chunk, chunk), 1)
        for g in range(HEAD_GROUPS):
            lg = lg_ref[g]
            qdec_ref[g] = jnp.exp(lg * tpos)
            kdec_ref[g] = jnp.exp(lg * (chunk - tpos))
            for hh in range(2):
                lg1 = lg[:, hh * dk_b:hh * dk_b + 1]
                dmat_ref[2 * g + hh] = jnp.exp(
                    jnp.where(tdiff >= 0, lg1 * tdiff.astype(_F32), -jnp.inf))

    rows = _iota((chunk, 1), 0)
    cos, sin = cos_ref[...], sin_ref[...]
    n_sub = chunk // GLA_SUB
    groups = range(HEAD_GROUPS)
    cols = [_h_cols(d_model, g) for g in groups]
    ksl = [slice(hh * dk_b, (hh + 1) * dk_b) for hh in range(2)]
    vsl = [slice(hh * dv_b, (hh + 1) * dv_b) for hh in range(2)]


    gla, ret = [], []
    for g in groups:
        c = cols[g]
        b = _cumsum_rows_mxu(la_ref[:, g * dk_a:(g + 1) * dk_a]) * LOG2_E
        q = h_ref[:, c["qa"]] * (dk_a ** -0.5)
        k = h_ref[:, c["ka"]]
        v = hv_ref[:, c["va"]]
        s_gla = sg_ref[0, 0, g]
        gla.append(dict(b=b, q=q, k=k, v=v, s=s_gla, o=_dot(q * jnp.exp2(b), s_gla)))

        qb = _rope(h_ref[:, c["qb"]], cos, sin)
        kb = _rope(h_ref[:, c["kb"]], cos, sin) * (dk_b ** -0.5)
        vb = hv_ref[:, c["vb"]]
        lg = lg_ref[g]
        q_dec = qb * qdec_ref[g]
        s_ret = [sr_ref[0, 0, 2 * g + hh] for hh in range(2)]
        ret.append(dict(
            vb=vb, lg=lg, s=s_ret, k_dec=kb * kdec_ref[g],
            o=[_dot(q_dec[:, ksl[hh]], s_ret[hh]) for hh in range(2)],
            qk=[_dot_nt(qb[:, ksl[hh]], kb[:, ksl[hh]]) for hh in range(2)]))

    for g in groups:
        q, k, b = gla[g]["q"], gla[g]["k"], gla[g]["b"]
        a_off = [jnp.zeros((GLA_SUB, chunk), _F32)]
        for i in range(1, n_sub):
            lo = i * GLA_SUB
            r = b[lo - 1:lo, :]
            k_pre = k * jnp.exp2(jnp.where(rows < lo, r - b, -jnp.inf))
            a_off.append(_dot_nt(q[lo:lo + GLA_SUB] * jnp.exp2(b[lo:lo + GLA_SUB] - r), k_pre))
        gla[g]["a"] = a_off

    for g in groups:
        q, k, b, v = gla[g]["q"], gla[g]["k"], gla[g]["b"], gla[g]["v"]
        intra = []
        for i in range(n_sub):
            lo = i * GLA_SUB
            sub = slice(lo, lo + GLA_SUB)
            a = _gla_pair_scores(gla[g]["a"][i], q[sub], k[sub], b[sub], lo, lambda s: (s, GLA_SUB))
            intra.append(_dot(a, v))
        gla[g]["o"] = gla[g]["o"] + jnp.concatenate(intra, axis=0)
        for hh in range(2):
            scores = ret[g]["qk"][hh] * dmat_ref[2 * g + hh]
            ret[g]["o"][hh] = ret[g]["o"][hh] + _dot(scores, ret[g]["vb"][:, vsl[hh]])

    for g in groups:
        k, b, v = gla[g]["k"], gla[g]["b"], gla[g]["v"]
        b_last = b[chunk - 1:chunk, :]
        decay_col = _rows_to_cols([jnp.exp2(b_last)])[:, 0:1]
        sg_ref[0, 0, g] = decay_col * gla[g]["s"] + _dot_tn(k * jnp.exp2(b_last - b), v)
        for hh in range(2):
            lg1 = ret[g]["lg"][:, hh * dk_b:hh * dk_b + 1]
            sr_ref[0, 0, 2 * g + hh] = (jnp.exp(lg1 * chunk) * ret[g]["s"][hh]
                                        + _dot_tn(ret[g]["k_dec"][:, ksl[hh]], ret[g]["vb"][:, vsl[hh]]))

    for g in groups:
        c = cols[g]
        gv = slice(g * dv_a, (g + 1) * dv_a)
        o_a = _rms_heads(gla[g]["o"], gnw_ref[...]) * h_ref[:, c["za"]]
        o_b = []
        for hh in range(2):
            nsl = slice(g * dv_a + hh * dv_b, g * dv_a + (hh + 1) * dv_b)
            o_b.append(_group_norm(ret[g]["o"][hh], rnw_ref[:, nsl], rnb_ref[:, nsl]))
        o_b = jnp.concatenate(o_b, axis=1) * h_ref[:, c["zb"]]
        merged = h_ref[:, c["ga"]] * o_a + h_ref[:, c["gb"]] * o_b
        m_ref[:, gv] = merged.astype(m_ref.dtype)


def _mixer_prompt(h, h_v, la, params, rope, lg, w_out, layer, batch, seq, d_model):
    wq, wv = d_model // 8, d_model // 4
    dk_a, dv_a = wq, wv
    dk_b, dv_b = wq // 2, wv // 2
    chunk = PROMPT_CHUNK
    nc = seq // chunk
    e_rows = w_out.shape[1]
    bf16_rows = 2 * SUBLANES
    assert e_rows % (batch * nc) == 0 and (e_rows // (batch * nc)) % bf16_rows == 0
    w_rows = e_rows // (batch * nc)
    whole = lambda a: pl.BlockSpec(a.shape, lambda b, c: (0,) * a.ndim)
    in_specs = [pl.BlockSpec((chunk, h.shape[1]), lambda b, c: (b * nc + c, 0)),
                pl.BlockSpec((chunk, h_v.shape[1]), lambda b, c: (b * nc + c, 0)),
                pl.BlockSpec((chunk, la.shape[1]), lambda b, c: (b * nc + c, 0))]
    in_specs += [whole(p) for p in params]
    in_specs += [pl.BlockSpec((chunk, wq), lambda b, c: (c, 0)),
                 pl.BlockSpec((chunk, wq), lambda b, c: (c, 0)),
                 whole(lg),
                 pl.BlockSpec((1, w_rows, w_out.shape[2]), lambda b, c: (layer, b * nc + c, 0))]
    out_specs = [
        pl.BlockSpec((chunk, d_model), lambda b, c: (b * nc + c, 0)),
        pl.BlockSpec((1, 1, GLA_HEADS, dk_a, dv_a), lambda b, c: (0, b, 0, 0, 0)),
        pl.BlockSpec((1, 1, RET_HEADS, dk_b, dv_b), lambda b, c: (0, b, 0, 0, 0)),
        pl.BlockSpec((w_rows, w_out.shape[2]), lambda b, c: (b * nc + c, 0)),
    ]
    out_shape = [
        jax.ShapeDtypeStruct((batch * seq, d_model), _BF16),
        jax.ShapeDtypeStruct((1, batch, GLA_HEADS, dk_a, dv_a), _F32),
        jax.ShapeDtypeStruct((1, batch, RET_HEADS, dk_b, dv_b), _F32),
        jax.ShapeDtypeStruct(w_out.shape[1:], _BF16),
    ]
    return pl.pallas_call(
        _mixer_prompt_kernel,
        grid=(batch, nc),
        in_specs=in_specs,
        out_specs=out_specs,
        out_shape=out_shape,
        scratch_shapes=[pltpu.VMEM((HEAD_GROUPS, chunk, wq), _F32),
                        pltpu.VMEM((HEAD_GROUPS, chunk, wq), _F32),
                        pltpu.VMEM((RET_HEADS, chunk, chunk), _F32)],
        compiler_params=pltpu.CompilerParams(
            dimension_semantics=("arbitrary", "arbitrary"),
            vmem_limit_bytes=VMEM_LIMIT_BYTES),
        name="mixer_prompt",
    )(h, h_v, la, *params, *rope, lg, w_out)


def _mixer_sample_unit(qa_ref, ka_ref, va_ref, za_ref, qb_ref, kb_ref, vb_ref, zb_ref,
                       ga_ref, gb_ref, la_ref, gnw_ref, rnw_ref, rnb_ref,
                       cos_ref, sin_ref, lg_ref, sg0_ref, sr0_ref, m_ref, sg_ref, sr_ref,
                       *, steps):
    rows_n, dk_a = qa_ref.shape
    nseq = rows_n // steps
    dk_b = qb_ref.shape[1] // 2
    dv_b = vb_ref.shape[1] // 2
    rows = _iota((rows_n, 1), 0)
    rows_p = _iota((LANES, 1), 0)
    pad = LANES - rows_n
    ksl = [slice(hh * dk_b, (hh + 1) * dk_b) for hh in range(2)]
    vsl = [slice(hh * dv_b, (hh + 1) * dv_b) for hh in range(2)]

    def pad_rows(x):
        return jnp.concatenate([x, jnp.zeros((pad, x.shape[1]), x.dtype)], axis=0)

    def seq_rows(x, n, r=rows):
        return jnp.where((r >= n * steps) & (r < (n + 1) * steps), x, 0.0)

    b = _seg_cumsum(la_ref[...], steps) * LOG2_E
    q = qa_ref[...] * (dk_a ** -0.5)
    k = ka_ref[...]
    v = pad_rows(va_ref[...]).astype(_BF16)
    q_dec = q * jnp.exp2(b)
    s_gla = [sg0_ref[0, n, 0] for n in range(nseq)]
    o_a = _dot(seq_rows(q_dec, 0), s_gla[0])
    for n in range(1, nseq):
        o_a = o_a + _dot(seq_rows(q_dec, n), s_gla[n])

    cos, sin = cos_ref[...], sin_ref[...]
    qb = _rope(qb_ref[...], cos, sin)
    kb = _rope(kb_ref[...], cos, sin) * (dk_b ** -0.5)
    vb = pad_rows(vb_ref[...]).astype(_BF16)
    lg = lg_ref[0]
    lg1 = [lg[:, hh * dk_b:hh * dk_b + 1] for hh in range(2)]
    tpos = ((_iota(qb.shape, 0) & (steps - 1)) + 1).astype(_F32)
    qb_dec = qb * jnp.exp(lg * tpos)
    kb_p = pad_rows(kb)
    s_ret = [[sr0_ref[0, n, hh] for n in range(nseq)] for hh in range(2)]
    o_r, qk = [], []
    for hh in range(2):
        o_h = _dot(seq_rows(qb_dec[:, ksl[hh]], 0), s_ret[hh][0])
        for n in range(1, nseq):
            o_h = o_h + _dot(seq_rows(qb_dec[:, ksl[hh]], n), s_ret[hh][n])
        o_r.append(o_h)
        qk.append(_dot_nt(qb[:, ksl[hh]], kb_p[:, ksl[hh]]))

    a = jnp.zeros((rows_n, LANES), _F32)
    a = _gla_pair_scores(a, q, k, b, 0, lambda s: (s, (s // steps + 1) * steps))
    o_a = o_a + _dot(a, v)
    r_i = _iota((rows_n, LANES), 0)
    c_i = _iota((rows_n, LANES), 1)
    same_seq = (r_i >> (steps.bit_length() - 1)) == (c_i >> (steps.bit_length() - 1))
    tdiff = r_i - c_i
    for hh in range(2):
        decay = jnp.exp(jnp.where((tdiff >= 0) & same_seq, lg1[hh] * tdiff.astype(_F32), -jnp.inf))
        o_r[hh] = o_r[hh] + _dot(qk[hh] * decay, vb[:, vsl[hh]])

    def finish():
        b_last = [b[(n + 1) * steps - 1:(n + 1) * steps, :] for n in range(nseq)]
        b_end = jnp.zeros_like(b)
        for n in range(nseq):
            b_end = b_end + seq_rows(jnp.broadcast_to(b_last[n], b.shape), n)
        k_dec = k * jnp.exp2(b_end - b)
        decay_cols = _rows_to_cols([jnp.exp2(r) for r in b_last])
        for n in range(nseq):
            sg_ref[0, n, 0] = (decay_cols[:, n:n + 1] * s_gla[n]
                               + _dot_tn(seq_rows(k_dec, n), v[:rows_n]))
        kb_dec = kb * jnp.exp(lg * (steps - tpos))
        for hh in range(2):
            for n in range(nseq):
                sr_ref[0, n, hh] = (jnp.exp(lg1[hh] * steps) * s_ret[hh][n]
                                    + _dot_tn(seq_rows(kb_dec[:, ksl[hh]], n), vb[:rows_n, vsl[hh]]))
        o_an = _rms_heads(o_a, gnw_ref[...]) * za_ref[...]
        o_bn = jnp.concatenate([_group_norm(o_r[hh], rnw_ref[:, vsl[hh]], rnb_ref[:, vsl[hh]])
                                for hh in range(2)], axis=1) * zb_ref[...]
        merged = ga_ref[...] * o_an + gb_ref[...] * o_bn
        m_ref[...] = merged.astype(m_ref.dtype)

    return finish


H_GROUPS = ("qa", "ka", "va", "za", "qb", "kb", "vb", "zb", "ga", "gb")


def _sample_units(h, h_v, la, params, rope, lg, state_gla, state_ret, layer, nseq_total, steps,
                  d_model):
    wq, wv = d_model // 8, d_model // 4
    dk_a, dv_a = wq, wv
    dk_b, dv_b = wq // 2, wv // 2
    nseq = SAMPLE_SEQS
    rows_n = nseq * steps
    assert steps & (steps - 1) == 0 and nseq_total % nseq == 0
    off_f, off_v = _h_offsets(d_model)
    offsets = {**off_f, **off_v}

    def specs(unit_of):
        def at(fn):
            return lambda *ids: fn(*unit_of(*ids))

        def h_spec(name):
            col_off, width = offsets[name]
            base = col_off // width
            return pl.BlockSpec((rows_n, width), at(lambda p, g: (p, base + g)))
        in_specs = [h_spec(name) for name in H_GROUPS]
        in_specs += [
            pl.BlockSpec((rows_n, wq), at(lambda p, g: (p, g))),
            pl.BlockSpec((1, wv), at(lambda p, g: (0, 0))),
            pl.BlockSpec((1, wv), at(lambda p, g: (0, g))),
            pl.BlockSpec((1, wv), at(lambda p, g: (0, g))),
            pl.BlockSpec((rows_n, wq), at(lambda p, g: (0, 0))),
            pl.BlockSpec((rows_n, wq), at(lambda p, g: (0, 0))),
            pl.BlockSpec((1, 1, wq), at(lambda p, g: (g, 0, 0))),
            pl.BlockSpec((1, nseq, 1, dk_a, dv_a), at(lambda p, g: (layer, p, g, 0, 0))),
            pl.BlockSpec((1, nseq, 2, dk_b, dv_b), at(lambda p, g: (layer, p, g, 0, 0))),
        ]
        out_specs = [
            pl.BlockSpec((rows_n, wv), at(lambda p, g: (p, g))),
            pl.BlockSpec((1, nseq, 1, dk_a, dv_a), at(lambda p, g: (0, p, g, 0, 0))),
            pl.BlockSpec((1, nseq, 2, dk_b, dv_b), at(lambda p, g: (0, p, g, 0, 0))),
        ]
        out_shape = [
            jax.ShapeDtypeStruct((nseq_total * steps, d_model), _BF16),
            jax.ShapeDtypeStruct((1, nseq_total, GLA_HEADS, dk_a, dv_a), _F32),
            jax.ShapeDtypeStruct((1, nseq_total, RET_HEADS, dk_b, dv_b), _F32),
        ]
        return in_specs, out_specs, out_shape

    operands = [h_v if name in off_v else h for name in H_GROUPS]
    operands += [la, *params, *rope, lg, state_gla, state_ret]
    n_units = (nseq_total // nseq) * HEAD_GROUPS
    return operands, steps, n_units, specs


PROJ_OUT_SLABS = 4

def _out_proj_ln(m, x, w, ln_w, ln_b, alpha):
    r = alpha * x + jnp.dot(m, w, preferred_element_type=_F32)
    mu = jnp.mean(r, axis=-1, keepdims=True)
    d = r - mu
    var = jnp.mean(d * d, axis=-1, keepdims=True)
    return d * lax.rsqrt(var + LN_EPS) * ln_w + ln_b


def _proj_out_kernel(m_ref, x_ref, w_ref, lnw_ref, lnb_ref, y_ref, *, alpha):
    tm = m_ref.shape[0]
    slab = min(tm, max(tm // PROJ_OUT_SLABS, LANES))
    for r in range(0, tm, slab):
        if x_ref.ndim == 3:
            steps = x_ref.shape[1]
            seqs = slice(r // steps, (r + slab) // steps)
            x = x_ref[seqs].reshape(slab, x_ref.shape[2])
        else:
            x = x_ref[r:r + slab, :]
        y = _out_proj_ln(m_ref[r:r + slab, :], x, w_ref[...], lnw_ref[...], lnb_ref[...], alpha)
        if x_ref.ndim == 3:
            y_ref[seqs] = y.reshape(slab // steps, steps, y.shape[1])
        else:
            y_ref[r:r + slab, :] = y


def _proj_out(merged, x, w_out, ln_w, ln_b, alpha, tm):
    d = x.shape[-1]
    m = x.size // d
    e = merged.shape[1]
    if x.ndim == 3:
        xy_spec = pl.BlockSpec((tm // x.shape[1], x.shape[1], d), lambda i: (i, 0, 0))
    else:
        xy_spec = pl.BlockSpec((tm, d), lambda i: (i, 0))
    return pl.pallas_call(
        functools.partial(_proj_out_kernel, alpha=alpha),
        grid=(m // tm,),
        in_specs=[
            pl.BlockSpec((tm, e), lambda i: (i, 0)),
            xy_spec,
            pl.BlockSpec((e, d), lambda i: (0, 0)),
            pl.BlockSpec((1, d), lambda i: (0, 0)),
            pl.BlockSpec((1, d), lambda i: (0, 0)),
        ],
        out_specs=xy_spec,
        out_shape=jax.ShapeDtypeStruct(x.shape, _F32),
        compiler_params=pltpu.CompilerParams(
            dimension_semantics=("arbitrary",),
            vmem_limit_bytes=VMEM_LIMIT_BYTES),
        name="proj_out",
    )(merged, x, w_out, ln_w, ln_b)


def _rope_tables(pos, dk):
    inv = 1.0 / (ROPE_BASE ** jnp.linspace(0.0, 1.0, dk // 2, dtype=_F32))
    ang = pos.astype(_F32)[:, None] * inv[None, :]
    cos = jnp.repeat(jnp.cos(ang), 2, axis=1)
    sin = jnp.stack([-jnp.sin(ang), jnp.sin(ang)], axis=-1).reshape(ang.shape[0], dk)
    return jnp.tile(cos, (1, 2)), jnp.tile(sin, (1, 2))


def _pick_tile(n, pref):
    t = min(n, pref)
    while n % t:
        t //= 2
    return t


def kernel(x_prompt, x_sample, state_gla, state_ret, w_in, w_lr, b_lr, gla_norm_w,
           ret_norm_w, ret_norm_b, w_out, ln_w, ln_b):
    depth, d_model, _ = w_in.shape
    batch, seq, _ = x_prompt.shape
    dec_batch, dec_seq, _ = x_sample.shape
    rank = w_lr.shape[1]
    dk_b = d_model // 2 // RET_HEADS
    assert state_gla.shape[2] == GLA_HEADS and state_ret.shape[2] == RET_HEADS
    assert rank <= LANES and seq % PROMPT_CHUNK == 0
    alpha = (2.0 * depth) ** 0.25

    lg_heads = jnp.log(1.0 - 2.0 ** (-5.0 - jnp.arange(RET_HEADS, dtype=_F32)))
    lg = jnp.repeat(lg_heads, dk_b).reshape(HEAD_GROUPS, 1, 2 * dk_b)
    rope_p = _rope_tables(jnp.arange(seq, dtype=jnp.int32), dk_b)
    pos_s = PAST_LEN + jnp.arange(dec_seq, dtype=jnp.int32)
    rope_s = _rope_tables(jnp.tile(pos_s, SAMPLE_SEQS), dk_b)

    hp = x_prompt.reshape(batch * seq, d_model)
    hs = x_sample
    n_rows_s = dec_batch * dec_seq
    w_t = jnp.swapaxes(w_in, 1, 2)
    gla_p, ret_p, gla_s, ret_s = [], [], [], []
    for l in range(depth):
        lr_lo = 3 * d_model
        w_lr_p = jnp.pad(w_lr[l], ((0, LANES - rank), (0, 0)))
        params = (gla_norm_w[l][None, :], ret_norm_w[l][None, :], ret_norm_b[l][None, :])
        lnw, lnb = ln_w[l][None, :], ln_b[l][None, :]

        xp_b, la_p = _prep_x(hp, w_t, l, lr_lo, rank, w_lr_p, b_lr[l][None, :],
                             _pick_tile(hp.shape[0], 512))
        xs_b, la_s = _prep_x(hs, w_t, l, lr_lo, rank, w_lr_p, b_lr[l][None, :],
                             _pick_tile(n_rows_s, 512))
        h_s, hv_s = _proj_in(xs_b, w_t, l, rank, _pick_tile(xs_b.shape[0], 1024))
        units = _sample_units(h_s, hv_s, la_s, params, rope_s, lg, state_gla, state_ret, l,
                              dec_batch, dec_seq, d_model)
        h_p, hv_p, merged_s, sg_s, sr_s = _proj_in(xp_b, w_t, l, rank,
                                                   _pick_tile(xp_b.shape[0], 1024), sample=units)

        merged, sg, sr, w_o = _mixer_prompt(h_p, hv_p, la_p, params, rope_p, lg, w_out, l,
                                            batch, seq, d_model)
        hp = _proj_out(merged, hp, w_o, lnw, lnb, alpha, _pick_tile(hp.shape[0], 512))
        gla_p.append(sg)
        ret_p.append(sr)

        merged, sg, sr = merged_s, sg_s, sr_s
        hs = _proj_out(merged, hs, w_o, lnw, lnb, alpha, _pick_tile(n_rows_s, 256))
        gla_s.append(sg)
        ret_s.append(sr)

    cat = lambda parts: parts[0] if len(parts) == 1 else jnp.concatenate(parts, axis=0)
    return (hp.reshape(batch, seq, d_model), hs,
            cat(gla_p), cat(ret_p), cat(gla_s), cat(ret_s))
```

```python
import functools

import jax
import jax.numpy as jnp
from jax import lax
from jax.experimental import pallas as pl
from jax.experimental.pallas import tpu as pltpu

GLA_HEADS = 4
RET_HEADS = 8
HEAD_GROUPS = 4
GLA_TAU = 16.0
LOG2_E = 1.4426950408889634
ROPE_BASE = 10000.0
LN_EPS = 1e-5
HEAD_NORM_EPS = 1e-6
PAST_LEN = 16384

LANES = 128
SUBLANES = 8
VMEM_LIMIT_BYTES = 56 * 1024 * 1024
VMEM_TEMP_FACTOR = 1.25

PROMPT_CHUNK = 128
GLA_SUB = 16
SAMPLE_SEQS = 4

_F32 = jnp.float32
_BF16 = jnp.bfloat16
_NT = (((1,), (1,)), ((), ()))
_TN = (((0,), (0,)), ((), ()))


def _vmem_limit(in_specs, operands, out_specs, out_shape, scratch=()):
    def nbytes(shape, dtype):
        n = jnp.dtype(dtype).itemsize
        for s in shape:
            n *= s
        return n
    total = sum(2 * nbytes(spec.block_shape, op.dtype) for spec, op in zip(in_specs, operands))
    total += sum(2 * nbytes(spec.block_shape, o.dtype) for spec, o in zip(out_specs, out_shape))
    total += sum(nbytes(s.shape, s.dtype) for s in scratch)
    mib = 1024 * 1024
    return min(VMEM_LIMIT_BYTES, -(-int(total * VMEM_TEMP_FACTOR) // mib) * mib)


def _dot(a, b):
    return jnp.dot(a.astype(_BF16), b.astype(_BF16), preferred_element_type=_F32)


def _dot_nt(a, b):
    return lax.dot_general(a.astype(_BF16), b.astype(_BF16), _NT, preferred_element_type=_F32)


def _dot_tn(a, b):
    return lax.dot_general(a.astype(_BF16), b.astype(_BF16), _TN, preferred_element_type=_F32)


def _log_sigmoid(x):
    return jnp.minimum(x, 0.0) - jnp.log(1.0 + jnp.exp(-jnp.abs(x)))


def _silu(x):
    return x * jax.nn.sigmoid(x)


def _iota(shape, axis):
    return lax.broadcasted_iota(jnp.int32, shape, axis)


def _seg_cumsum(x, seg):
    pos = _iota(x.shape, 0) & (seg - 1)
    s = 1
    while s < seg:
        x = x + jnp.where(pos >= s, pltpu.roll(x, s, axis=0), 0.0)
        s *= 2
    return x


def _cumsum_rows_mxu(x):
    n = x.shape[0]
    hi = x.astype(_BF16)
    r1 = x - hi.astype(_F32)
    mid = r1.astype(_BF16)
    lo = (r1 - mid.astype(_F32)).astype(_BF16)
    tri3 = (_iota((n, 3 * n), 0) >= (_iota((n, 3 * n), 1) & (n - 1))).astype(_BF16)
    return jnp.dot(tri3, jnp.concatenate([hi, mid, lo], axis=0), preferred_element_type=_F32)


def _rows_to_cols(rows):
    n = rows[0].shape[1]
    rid = _iota((LANES, n), 0)
    stack = jnp.zeros((LANES, n), _F32)
    for i, r in enumerate(rows):
        stack = jnp.where(rid == i, r, stack)
    return stack.T


def _rope(x, cos, sin_signed):
    w = x.shape[1]
    even = (_iota(x.shape, 1) & 1) == 0
    swapped = jnp.where(even, pltpu.roll(x, w - 1, axis=1), pltpu.roll(x, 1, axis=1))
    return x * cos + swapped * sin_signed


def _gla_pair_scores(a, q, k, b2, lane0, row_lo_hi):
    n = q.shape[0]
    lanes = _iota((SUBLANES, a.shape[1]), 1)
    tiles = [a[r:r + SUBLANES] for r in range(0, n, SUBLANES)]
    for s in range(n):
        lo, hi = row_lo_hi(s)
        for j, r0 in enumerate(range(0, n, SUBLANES)):
            if r0 + SUBLANES <= lo or r0 >= hi:
                continue
            e = jnp.exp2(b2[r0:r0 + SUBLANES] - b2[s:s + 1, :])
            col = jnp.sum(q[r0:r0 + SUBLANES] * k[s:s + 1, :] * e, axis=-1, keepdims=True)
            take = lanes == lane0 + s
            if lo > r0 or hi < r0 + SUBLANES:
                rows = _iota((SUBLANES, 1), 0) + r0
                take = take & (rows >= lo) & (rows < hi)
            tiles[j] = jnp.where(take, col, tiles[j])
    return jnp.concatenate(tiles, axis=0)


def _rms_heads(o, w):
    return o * lax.rsqrt(jnp.mean(o * o, axis=-1, keepdims=True) + HEAD_NORM_EPS) * w


def _group_norm(o, w, b):
    mu = jnp.mean(o, axis=-1, keepdims=True)
    d = o - mu
    var = jnp.mean(d * d, axis=-1, keepdims=True)
    return d * lax.rsqrt(var + HEAD_NORM_EPS) * w + b


def _prep_x_kernel(x_ref, wcol_ref, wlr_ref, blr_ref, xb_ref, la_ref, *, rank):
    x = x_ref[...]
    if x.ndim == 3:
        x = x.reshape(x.shape[0] * x.shape[1], x.shape[2])
    xb = x.astype(_BF16)
    xb_ref[...] = xb
    wrow = wcol_ref[0]
    w_lra = jnp.concatenate([wrow, jnp.zeros((LANES - rank, wrow.shape[1]), _F32)], axis=0)
    lr = _dot_nt(xb, w_lra)
    w = wlr_ref[...]
    w_hi = w.astype(_BF16).astype(_F32)
    w_lo = w - w_hi
    lr_hi = lr.astype(_BF16).astype(_F32)
    lr_lo = lr - lr_hi
    lr_cat = lr_hi + pltpu.roll(lr_hi, rank, axis=1) + pltpu.roll(lr_lo, 2 * rank, axis=1)
    w_cat = jnp.concatenate([w_hi[:rank], w_lo[:rank], w_hi[:rank],
                             jnp.zeros((LANES - 3 * rank, w.shape[1]), _F32)], axis=0)
    logit = _dot(lr_cat, w_cat) + blr_ref[...]
    la_ref[...] = _log_sigmoid(logit) * (1.0 / GLA_TAU)


def _prep_x(x, w_t, layer, lr_col, rank, w_lr, b_lr, tm):
    kdim = x.shape[-1]
    m = x.size // kdim
    qk = w_lr.shape[1]
    assert lr_col % rank == 0 and rank % SUBLANES == 0 and 3 * rank <= LANES
    whole = lambda a: pl.BlockSpec(a.shape, lambda i: (0,) * a.ndim)
    if x.ndim == 3:
        x_spec = pl.BlockSpec((tm // x.shape[1], x.shape[1], kdim), lambda i: (i, 0, 0))
    else:
        x_spec = pl.BlockSpec((tm, kdim), lambda i: (i, 0))
    operands = [x, w_t, w_lr, b_lr]
    in_specs = [x_spec,
                pl.BlockSpec((1, rank, kdim), lambda i: (layer, lr_col // rank, 0)),
                whole(w_lr), whole(b_lr)]
    out_specs = [pl.BlockSpec((tm, kdim), lambda i: (i, 0)),
                 pl.BlockSpec((tm, qk), lambda i: (i, 0))]
    out_shape = [jax.ShapeDtypeStruct((m, kdim), _BF16),
                 jax.ShapeDtypeStruct((m, qk), _F32)]
    return pl.pallas_call(
        functools.partial(_prep_x_kernel, rank=rank),
        grid=(m // tm,),
        in_specs=in_specs,
        out_specs=out_specs,
        out_shape=out_shape,
        compiler_params=pltpu.CompilerParams(
            dimension_semantics=("arbitrary",),
            vmem_limit_bytes=_vmem_limit(in_specs, operands, out_specs, out_shape)),
        name="prep_x",
    )(*operands)


N_SAMPLE_IN = 19


def _proj_in_kernel(*refs, n_plain, shift, silu_tiles, sigmoid_tiles, sample_steps):
    x_ref, wa_ref, we_ref = refs[:3]
    wbf_ref = refs[-1]
    if sample_steps:
        unit_in = refs[3:3 + N_SAMPLE_IN]
        h_ref = refs[3 + N_SAMPLE_IN]
        unit_out = refs[4 + N_SAMPLE_IN:7 + N_SAMPLE_IN]
    else:
        h_ref = refs[3]
    j = pl.program_id(0)
    i = pl.program_id(1)
    tn, kdim = wbf_ref.shape

    @pl.when(i == 0)
    def _():
        @pl.when(j < n_plain)
        def _():
            wbf_ref[...] = wa_ref[0].astype(_BF16)

        @pl.when(j >= n_plain)
        def _():
            wbf_ref[0:tn - shift, :] = wa_ref[0, shift:tn, :].astype(_BF16)
            wbf_ref[tn - shift:tn, :] = we_ref[0].astype(_BF16)

    is_silu = functools.reduce(jnp.logical_or, [j == t for t in silu_tiles])
    is_sigmoid = functools.reduce(jnp.logical_or, [j == t for t in sigmoid_tiles])

    def slab(cols, activated):
        acc = lax.dot_general(x_ref[...], wbf_ref[cols, :], _NT, preferred_element_type=_F32)
        if activated:
            s = 0.5 * jnp.tanh(0.5 * acc) + 0.5
            acc = jnp.where(is_sigmoid, s, acc * s)
        h_ref[:, cols] = acc

    def tile(activated):
        finish_unit = None
        if sample_steps:
            finish_unit = _mixer_sample_unit(*unit_in, *unit_out, steps=sample_steps)
        slab(slice(0, tn // 4), activated)
        if finish_unit is not None:
            finish_unit()
        slab(slice(tn // 4, tn), activated)

    @pl.when(is_silu | is_sigmoid)
    def _():
        tile(True)

    @pl.when(jnp.logical_not(is_silu | is_sigmoid))
    def _():
        tile(False)


PROJ_TILES = 16


def _proj_in(x, w_t, layer, rank, tm, sample=None):
    m, kdim = x.shape
    d_model = kdim
    tn = d_model // 2
    nj = PROJ_TILES
    n_plain = 6
    bf16_rows = 2 * SUBLANES
    assert rank % bf16_rows == 0 and tn % rank == 0 and m % tm == 0
    ni = m // tm

    def row_block(j, i):
        return jnp.where(j % 2 == 0, i, ni - 1 - i)
    in_specs = [
        pl.BlockSpec((tm, kdim), lambda j, i: (row_block(j, i), 0)),
        pl.BlockSpec((1, tn, kdim), lambda j, i: (layer, j, 0)),
        pl.BlockSpec((1, rank, kdim), lambda j, i: (layer, (j + 1) * (tn // rank), 0)),
    ]
    out_specs = [pl.BlockSpec((tm, tn), lambda j, i: (row_block(j, i), j))]
    out_shape = [jax.ShapeDtypeStruct((m, nj * tn), _F32)]
    operands = [x, w_t, w_t]
    sample_steps = 0
    if sample is not None:
        unit_operands, sample_steps, n_units, unit_specs = sample
        assert n_units <= nj * ni

        def unit_of(j, i):
            u = jnp.minimum(j * ni + i, n_units - 1)
            return u // HEAD_GROUPS, u % HEAD_GROUPS
        unit_in, unit_out, unit_shape = unit_specs(unit_of)
        assert len(unit_in) == N_SAMPLE_IN
        in_specs += unit_in
        out_specs += unit_out
        out_shape += unit_shape
        operands += unit_operands
    kern = functools.partial(_proj_in_kernel, n_plain=n_plain, shift=rank,
                             silu_tiles=(4, 5, 10, 11), sigmoid_tiles=(12, 13, 14, 15),
                             sample_steps=sample_steps)
    scratch = [pltpu.VMEM((tn, kdim), _BF16)]
    return pl.pallas_call(
        kern,
        grid=(nj, ni),
        in_specs=in_specs,
        out_specs=out_specs,
        out_shape=out_shape,
        scratch_shapes=scratch,
        compiler_params=pltpu.CompilerParams(
            dimension_semantics=("arbitrary", "arbitrary"),
            vmem_limit_bytes=_vmem_limit(in_specs, operands, out_specs, out_shape, scratch)),
        name="proj_in_mix" if sample is not None else "proj_in",
    )(*operands)


def _h_cols(d_model, g):
    wq, wv = d_model // 8, d_model // 4
    d = d_model
    offs = dict(qa=(0, wq), ka=(d // 2, wq), va=(d, wv), za=(2 * d, wv),
                qb=(3 * d, wq), kb=(3 * d + d // 2, wq), vb=(4 * d, wv), zb=(5 * d, wv),
                ga=(6 * d, wv), gb=(7 * d, wv))
    return {n: slice(o + g * w, o + (g + 1) * w) for n, (o, w) in offs.items()}


def _mixer_prompt_kernel(h_ref, la_ref, gnw_ref, rnw_ref, rnb_ref,
                         cos_ref, sin_ref, lg_ref, wout_ref, m_ref, sg_ref, sr_ref, wout_bf_ref,
                         qdec_ref, kdec_ref, dmat_ref):
    wout_bf_ref[...] = wout_ref[0].astype(_BF16)
    chunk = h_ref.shape[0]
    d_model = m_ref.shape[1]
    dk_a, dv_a = sg_ref.shape[3:]
    dk_b, dv_b = sr_ref.shape[3:]

    @pl.when(pl.program_id(1) == 0)
    def _():
        sg_ref[...] = jnp.zeros(sg_ref.shape, _F32)
        sr_ref[...] = jnp.zeros(sr_ref.shape, _F32)

    @pl.when((pl.program_id(0) == 0) & (pl.program_id(1) == 0))
    def _():
        tpos = (_iota((chunk, 2 * dk_b), 0) + 1).astype(_F32)
        tdiff = _iota((chunk, chunk), 0) - _iota((chunk, chunk), 1)
        for g in range(HEAD_GROUPS):
            lg = lg_ref[g]
            qdec_ref[g] = jnp.exp(lg * tpos)
            kdec_ref[g] = jnp.exp(lg * (chunk - tpos))
            for hh in range(2):
                lg1 = lg[:, hh * dk_b:hh * dk_b + 1]
                dmat_ref[2 * g + hh] = jnp.exp(
                    jnp.where(tdiff >= 0, lg1 * tdiff.astype(_F32), -jnp.inf))

    rows = _iota((chunk, 1), 0)
    cos, sin = cos_ref[...], sin_ref[...]
    n_sub = chunk // GLA_SUB
    groups = range(HEAD_GROUPS)
    cols = [_h_cols(d_model, g) for g in groups]
    ksl = [slice(hh * dk_b, (hh + 1) * dk_b) for hh in range(2)]
    vsl = [slice(hh * dv_b, (hh + 1) * dv_b) for hh in range(2)]


    gla, ret = [], []
    for g in groups:
        c = cols[g]
        b = _cumsum_rows_mxu(la_ref[:, g * dk_a:(g + 1) * dk_a]) * LOG2_E
        q = h_ref[:, c["qa"]] * (dk_a ** -0.5)
        k = h_ref[:, c["ka"]]
        v = h_ref[:, c["va"]].astype(_BF16)
        s_gla = sg_ref[0, 0, g]
        gla.append(dict(b=b, q=q, k=k, v=v, s=s_gla, o=_dot(q * jnp.exp2(b), s_gla)))

        qb = _rope(h_ref[:, c["qb"]], cos, sin)
        kb = _rope(h_ref[:, c["kb"]], cos, sin) * (dk_b ** -0.5)
        vb = h_ref[:, c["vb"]].astype(_BF16)
        lg = lg_ref[g]
        q_dec = qb * qdec_ref[g]
        s_ret = [sr_ref[0, 0, 2 * g + hh] for hh in range(2)]
        ret.append(dict(
            vb=vb, lg=lg, s=s_ret, k_dec=kb * kdec_ref[g],
            o=[_dot(q_dec[:, ksl[hh]], s_ret[hh]) for hh in range(2)],
            qk=[_dot_nt(qb[:, ksl[hh]], kb[:, ksl[hh]]) for hh in range(2)]))

    for g in groups:
        q, k, b = gla[g]["q"], gla[g]["k"], gla[g]["b"]
        a_off = [jnp.zeros((GLA_SUB, chunk), _F32)]
        for i in range(1, n_sub):
            lo = i * GLA_SUB
            r = b[lo - 1:lo, :]
            k_pre = k * jnp.exp2(jnp.where(rows < lo, r - b, -jnp.inf))
            a_off.append(_dot_nt(q[lo:lo + GLA_SUB] * jnp.exp2(b[lo:lo + GLA_SUB] - r), k_pre))
        gla[g]["a"] = a_off

    for g in groups:
        q, k, b, v = gla[g]["q"], gla[g]["k"], gla[g]["b"], gla[g]["v"]
        intra = []
        for i in range(n_sub):
            lo = i * GLA_SUB
            sub = slice(lo, lo + GLA_SUB)
            a = _gla_pair_scores(gla[g]["a"][i], q[sub], k[sub], b[sub], lo, lambda s: (s, GLA_SUB))
            intra.append(_dot(a, v))
        gla[g]["o"] = gla[g]["o"] + jnp.concatenate(intra, axis=0)
        for hh in range(2):
            scores = ret[g]["qk"][hh] * dmat_ref[2 * g + hh]
            ret[g]["o"][hh] = ret[g]["o"][hh] + _dot(scores, ret[g]["vb"][:, vsl[hh]])

    for g in groups:
        k, b, v = gla[g]["k"], gla[g]["b"], gla[g]["v"]
        b_last = b[chunk - 1:chunk, :]
        decay_col = _rows_to_cols([jnp.exp2(b_last)])[:, 0:1]
        sg_ref[0, 0, g] = decay_col * gla[g]["s"] + _dot_tn(k * jnp.exp2(b_last - b), v)
        for hh in range(2):
            lg1 = ret[g]["lg"][:, hh * dk_b:hh * dk_b + 1]
            sr_ref[0, 0, 2 * g + hh] = (jnp.exp(lg1 * chunk) * ret[g]["s"][hh]
                                        + _dot_tn(ret[g]["k_dec"][:, ksl[hh]], ret[g]["vb"][:, vsl[hh]]))

    for g in groups:
        c = cols[g]
        gv = slice(g * dv_a, (g + 1) * dv_a)
        o_a = _rms_heads(gla[g]["o"], gnw_ref[...]) * h_ref[:, c["za"]]
        o_b = []
        for hh in range(2):
            nsl = slice(g * dv_a + hh * dv_b, g * dv_a + (hh + 1) * dv_b)
            o_b.append(_group_norm(ret[g]["o"][hh], rnw_ref[:, nsl], rnb_ref[:, nsl]))
        o_b = jnp.concatenate(o_b, axis=1) * h_ref[:, c["zb"]]
        merged = h_ref[:, c["ga"]] * o_a + h_ref[:, c["gb"]] * o_b
        m_ref[:, gv] = merged.astype(m_ref.dtype)


def _mixer_prompt(h, la, params, rope, lg, w_out, layer, batch, seq, d_model):
    wq, wv = d_model // 8, d_model // 4
    dk_a, dv_a = wq, wv
    dk_b, dv_b = wq // 2, wv // 2
    chunk = PROMPT_CHUNK
    nc = seq // chunk
    e_rows = w_out.shape[1]
    bf16_rows = 2 * SUBLANES
    assert e_rows % (batch * nc) == 0 and (e_rows // (batch * nc)) % bf16_rows == 0
    w_rows = e_rows // (batch * nc)
    whole = lambda a: pl.BlockSpec(a.shape, lambda b, c: (0,) * a.ndim)
    in_specs = [pl.BlockSpec((chunk, h.shape[1]), lambda b, c: (b * nc + c, 0)),
                pl.BlockSpec((chunk, la.shape[1]), lambda b, c: (b * nc + c, 0))]
    in_specs += [whole(p) for p in params]
    in_specs += [pl.BlockSpec((chunk, wq), lambda b, c: (c, 0)),
                 pl.BlockSpec((chunk, wq), lambda b, c: (c, 0)),
                 whole(lg),
                 pl.BlockSpec((1, w_rows, w_out.shape[2]), lambda b, c: (layer, b * nc + c, 0))]
    out_specs = [
        pl.BlockSpec((chunk, d_model), lambda b, c: (b * nc + c, 0)),
        pl.BlockSpec((1, 1, GLA_HEADS, dk_a, dv_a), lambda b, c: (0, b, 0, 0, 0)),
        pl.BlockSpec((1, 1, RET_HEADS, dk_b, dv_b), lambda b, c: (0, b, 0, 0, 0)),
        pl.BlockSpec((w_rows, w_out.shape[2]), lambda b, c: (b * nc + c, 0)),
    ]
    out_shape = [
        jax.ShapeDtypeStruct((batch * seq, d_model), _BF16),
        jax.ShapeDtypeStruct((1, batch, GLA_HEADS, dk_a, dv_a), _F32),
        jax.ShapeDtypeStruct((1, batch, RET_HEADS, dk_b, dv_b), _F32),
        jax.ShapeDtypeStruct(w_out.shape[1:], _BF16),
    ]
    operands = [h, la, *params, *rope, lg, w_out]
    scratch = [pltpu.VMEM((HEAD_GROUPS, chunk, wq), _F32),
               pltpu.VMEM((HEAD_GROUPS, chunk, wq), _F32),
               pltpu.VMEM((RET_HEADS, chunk, chunk), _F32)]
    return pl.pallas_call(
        _mixer_prompt_kernel,
        grid=(batch, nc),
        in_specs=in_specs,
        out_specs=out_specs,
        out_shape=out_shape,
        scratch_shapes=scratch,
        compiler_params=pltpu.CompilerParams(
            dimension_semantics=("arbitrary", "arbitrary"),
            vmem_limit_bytes=_vmem_limit(in_specs, operands, out_specs, out_shape, scratch)),
        name="mixer_prompt",
    )(*operands)


def _mixer_sample_unit(qa_ref, ka_ref, va_ref, za_ref, qb_ref, kb_ref, vb_ref, zb_ref,
                       ga_ref, gb_ref, la_ref, gnw_ref, rnw_ref, rnb_ref,
                       cos_ref, sin_ref, lg_ref, sg0_ref, sr0_ref, m_ref, sg_ref, sr_ref,
                       *, steps):
    rows_n, dk_a = qa_ref.shape
    nseq = rows_n // steps
    dk_b = qb_ref.shape[1] // 2
    dv_b = vb_ref.shape[1] // 2
    rows = _iota((rows_n, 1), 0)
    rows_p = _iota((LANES, 1), 0)
    pad = LANES - rows_n
    ksl = [slice(hh * dk_b, (hh + 1) * dk_b) for hh in range(2)]
    vsl = [slice(hh * dv_b, (hh + 1) * dv_b) for hh in range(2)]

    def pad_rows(x):
        return jnp.concatenate([x, jnp.zeros((pad, x.shape[1]), x.dtype)], axis=0)

    def seq_rows(x, n, r=rows):
        return jnp.where((r >= n * steps) & (r < (n + 1) * steps), x, 0.0)

    b = _seg_cumsum(la_ref[...], steps) * LOG2_E
    q = qa_ref[...] * (dk_a ** -0.5)
    k = ka_ref[...]
    v = pad_rows(va_ref[...]).astype(_BF16)
    q_dec = q * jnp.exp2(b)
    s_gla = [sg0_ref[0, n, 0] for n in range(nseq)]
    o_a = _dot(seq_rows(q_dec, 0), s_gla[0])
    for n in range(1, nseq):
        o_a = o_a + _dot(seq_rows(q_dec, n), s_gla[n])

    cos, sin = cos_ref[...], sin_ref[...]
    qb = _rope(qb_ref[...], cos, sin)
    kb = _rope(kb_ref[...], cos, sin) * (dk_b ** -0.5)
    vb = pad_rows(vb_ref[...]).astype(_BF16)
    lg = lg_ref[0]
    lg1 = [lg[:, hh * dk_b:hh * dk_b + 1] for hh in range(2)]
    tpos = ((_iota(qb.shape, 0) & (steps - 1)) + 1).astype(_F32)
    qb_dec = qb * jnp.exp(lg * tpos)
    kb_p = pad_rows(kb)
    s_ret = [[sr0_ref[0, n, hh] for n in range(nseq)] for hh in range(2)]
    o_r, qk = [], []
    for hh in range(2):
        o_h = _dot(seq_rows(qb_dec[:, ksl[hh]], 0), s_ret[hh][0])
        for n in range(1, nseq):
            o_h = o_h + _dot(seq_rows(qb_dec[:, ksl[hh]], n), s_ret[hh][n])
        o_r.append(o_h)
        qk.append(_dot_nt(qb[:, ksl[hh]], kb_p[:, ksl[hh]]))

    a = jnp.zeros((rows_n, LANES), _F32)
    a = _gla_pair_scores(a, q, k, b, 0, lambda s: (s, (s // steps + 1) * steps))
    o_a = o_a + _dot(a, v)
    r_i = _iota((rows_n, LANES), 0)
    c_i = _iota((rows_n, LANES), 1)
    same_seq = (r_i >> (steps.bit_length() - 1)) == (c_i >> (steps.bit_length() - 1))
    tdiff = r_i - c_i
    for hh in range(2):
        decay = jnp.exp(jnp.where((tdiff >= 0) & same_seq, lg1[hh] * tdiff.astype(_F32), -jnp.inf))
        o_r[hh] = o_r[hh] + _dot(qk[hh] * decay, vb[:, vsl[hh]])

    def finish():
        b_last = [b[(n + 1) * steps - 1:(n + 1) * steps, :] for n in range(nseq)]
        b_end = jnp.zeros_like(b)
        for n in range(nseq):
            b_end = b_end + seq_rows(jnp.broadcast_to(b_last[n], b.shape), n)
        k_dec = k * jnp.exp2(b_end - b)
        decay_cols = _rows_to_cols([jnp.exp2(r) for r in b_last])
        for n in range(nseq):
            sg_ref[0, n, 0] = (decay_cols[:, n:n + 1] * s_gla[n]
                               + _dot_tn(seq_rows(k_dec, n), v[:rows_n]))
        kb_dec = kb * jnp.exp(lg * (steps - tpos))
        for hh in range(2):
            for n in range(nseq):
                sr_ref[0, n, hh] = (jnp.exp(lg1[hh] * steps) * s_ret[hh][n]
                                    + _dot_tn(seq_rows(kb_dec[:, ksl[hh]], n), vb[:rows_n, vsl[hh]]))
        o_an = _rms_heads(o_a, gnw_ref[...]) * za_ref[...]
        o_bn = jnp.concatenate([_group_norm(o_r[hh], rnw_ref[:, vsl[hh]], rnb_ref[:, vsl[hh]])
                                for hh in range(2)], axis=1) * zb_ref[...]
        merged = ga_ref[...] * o_an + gb_ref[...] * o_bn
        m_ref[...] = merged.astype(m_ref.dtype)

    return finish


def _sample_units(h, la, params, rope, lg, state_gla, state_ret, layer, nseq_total, steps, d_model):
    wq, wv = d_model // 8, d_model // 4
    dk_a, dv_a = wq, wv
    dk_b, dv_b = wq // 2, wv // 2
    nseq = SAMPLE_SEQS
    rows_n = nseq * steps
    assert steps & (steps - 1) == 0 and nseq_total % nseq == 0
    d = d_model

    def specs(unit_of):
        def at(fn):
            return lambda *ids: fn(*unit_of(*ids))

        def h_spec(width, col_off):
            base = col_off // width
            return pl.BlockSpec((rows_n, width), at(lambda p, g: (p, base + g)))
        in_specs = [h_spec(wq, 0), h_spec(wq, d // 2), h_spec(wv, d), h_spec(wv, 2 * d),
                    h_spec(wq, 3 * d), h_spec(wq, 3 * d + d // 2), h_spec(wv, 4 * d),
                    h_spec(wv, 5 * d), h_spec(wv, 6 * d), h_spec(wv, 7 * d)]
        in_specs += [
            pl.BlockSpec((rows_n, wq), at(lambda p, g: (p, g))),
            pl.BlockSpec((1, wv), at(lambda p, g: (0, 0))),
            pl.BlockSpec((1, wv), at(lambda p, g: (0, g))),
            pl.BlockSpec((1, wv), at(lambda p, g: (0, g))),
            pl.BlockSpec((rows_n, wq), at(lambda p, g: (0, 0))),
            pl.BlockSpec((rows_n, wq), at(lambda p, g: (0, 0))),
            pl.BlockSpec((1, 1, wq), at(lambda p, g: (g, 0, 0))),
            pl.BlockSpec((1, nseq, 1, dk_a, dv_a), at(lambda p, g: (layer, p, g, 0, 0))),
            pl.BlockSpec((1, nseq, 2, dk_b, dv_b), at(lambda p, g: (layer, p, g, 0, 0))),
        ]
        out_specs = [
            pl.BlockSpec((rows_n, wv), at(lambda p, g: (p, g))),
            pl.BlockSpec((1, nseq, 1, dk_a, dv_a), at(lambda p, g: (0, p, g, 0, 0))),
            pl.BlockSpec((1, nseq, 2, dk_b, dv_b), at(lambda p, g: (0, p, g, 0, 0))),
        ]
        out_shape = [
            jax.ShapeDtypeStruct((nseq_total * steps, d_model), _BF16),
            jax.ShapeDtypeStruct((1, nseq_total, GLA_HEADS, dk_a, dv_a), _F32),
            jax.ShapeDtypeStruct((1, nseq_total, RET_HEADS, dk_b, dv_b), _F32),
        ]
        return in_specs, out_specs, out_shape

    operands = [h] * 10 + [la, *params, *rope, lg, state_gla, state_ret]
    n_units = (nseq_total // nseq) * HEAD_GROUPS
    return operands, steps, n_units, specs


PROJ_OUT_SLABS = 4

def _out_proj_ln(m, x, w, ln_w, ln_b, alpha):
    r = alpha * x + jnp.dot(m, w, preferred_element_type=_F32)
    mu = jnp.mean(r, axis=-1, keepdims=True)
    d = r - mu
    var = jnp.mean(d * d, axis=-1, keepdims=True)
    return d * lax.rsqrt(var + LN_EPS) * ln_w + ln_b


def _proj_out_kernel(m_ref, x_ref, w_ref, lnw_ref, lnb_ref, y_ref, *, alpha):
    tm = m_ref.shape[0]
    slab = min(tm, max(tm // PROJ_OUT_SLABS, LANES))
    for r in range(0, tm, slab):
        if x_ref.ndim == 3:
            steps = x_ref.shape[1]
            seqs = slice(r // steps, (r + slab) // steps)
            x = x_ref[seqs].reshape(slab, x_ref.shape[2])
        else:
            x = x_ref[r:r + slab, :]
        y = _out_proj_ln(m_ref[r:r + slab, :], x, w_ref[...], lnw_ref[...], lnb_ref[...], alpha)
        if x_ref.ndim == 3:
            y_ref[seqs] = y.reshape(slab // steps, steps, y.shape[1])
        else:
            y_ref[r:r + slab, :] = y


def _proj_out(merged, x, w_out, ln_w, ln_b, alpha, tm):
    d = x.shape[-1]
    m = x.size // d
    e = merged.shape[1]
    if x.ndim == 3:
        xy_spec = pl.BlockSpec((tm // x.shape[1], x.shape[1], d), lambda i: (i, 0, 0))
    else:
        xy_spec = pl.BlockSpec((tm, d), lambda i: (i, 0))
    operands = [merged, x, w_out, ln_w, ln_b]
    in_specs = [
        pl.BlockSpec((tm, e), lambda i: (i, 0)),
        xy_spec,
        pl.BlockSpec((e, d), lambda i: (0, 0)),
        pl.BlockSpec((1, d), lambda i: (0, 0)),
        pl.BlockSpec((1, d), lambda i: (0, 0)),
    ]
    out_shape = jax.ShapeDtypeStruct(x.shape, _F32)
    return pl.pallas_call(
        functools.partial(_proj_out_kernel, alpha=alpha),
        grid=(m // tm,),
        in_specs=in_specs,
        out_specs=xy_spec,
        out_shape=out_shape,
        compiler_params=pltpu.CompilerParams(
            dimension_semantics=("arbitrary",),
            vmem_limit_bytes=_vmem_limit(in_specs, operands, [xy_spec], [out_shape])),
        name="proj_out",
    )(*operands)


def _rope_tables(pos, dk):
    inv = 1.0 / (ROPE_BASE ** jnp.linspace(0.0, 1.0, dk // 2, dtype=_F32))
    ang = pos.astype(_F32)[:, None] * inv[None, :]
    cos = jnp.repeat(jnp.cos(ang), 2, axis=1)
    sin = jnp.stack([-jnp.sin(ang), jnp.sin(ang)], axis=-1).reshape(ang.shape[0], dk)
    return jnp.tile(cos, (1, 2)), jnp.tile(sin, (1, 2))


def _pick_tile(n, pref):
    t = min(n, pref)
    while n % t:
        t //= 2
    return t


def kernel(x_prompt, x_sample, state_gla, state_ret, w_in, w_lr, b_lr, gla_norm_w,
           ret_norm_w, ret_norm_b, w_out, ln_w, ln_b):
    depth, d_model, _ = w_in.shape
    batch, seq, _ = x_prompt.shape
    dec_batch, dec_seq, _ = x_sample.shape
    rank = w_lr.shape[1]
    dk_b = d_model // 2 // RET_HEADS
    assert state_gla.shape[2] == GLA_HEADS and state_ret.shape[2] == RET_HEADS
    assert rank <= LANES and seq % PROMPT_CHUNK == 0
    alpha = (2.0 * depth) ** 0.25

    lg_heads = jnp.log(1.0 - 2.0 ** (-5.0 - jnp.arange(RET_HEADS, dtype=_F32)))
    lg = jnp.repeat(lg_heads, dk_b).reshape(HEAD_GROUPS, 1, 2 * dk_b)
    rope_p = _rope_tables(jnp.arange(seq, dtype=jnp.int32), dk_b)
    pos_s = PAST_LEN + jnp.arange(dec_seq, dtype=jnp.int32)
    rope_s = _rope_tables(jnp.tile(pos_s, SAMPLE_SEQS), dk_b)

    hp = x_prompt.reshape(batch * seq, d_model)
    hs = x_sample
    n_rows_s = dec_batch * dec_seq
    w_t = jnp.swapaxes(w_in, 1, 2)
    gla_p, ret_p, gla_s, ret_s = [], [], [], []
    for l in range(depth):
        lr_lo = 3 * d_model
        w_lr_p = jnp.pad(w_lr[l], ((0, LANES - rank), (0, 0)))
        params = (gla_norm_w[l][None, :], ret_norm_w[l][None, :], ret_norm_b[l][None, :])
        lnw, lnb = ln_w[l][None, :], ln_b[l][None, :]

        xp_b, la_p = _prep_x(hp, w_t, l, lr_lo, rank, w_lr_p, b_lr[l][None, :],
                             _pick_tile(hp.shape[0], 512))
        xs_b, la_s = _prep_x(hs, w_t, l, lr_lo, rank, w_lr_p, b_lr[l][None, :],
                             _pick_tile(n_rows_s, 512))
        (h_s,) = _proj_in(xs_b, w_t, l, rank, _pick_tile(xs_b.shape[0], 1024))
        units = _sample_units(h_s, la_s, params, rope_s, lg, state_gla, state_ret, l,
                              dec_batch, dec_seq, d_model)
        h_p, merged_s, sg_s, sr_s = _proj_in(xp_b, w_t, l, rank, _pick_tile(xp_b.shape[0], 1024),
                                             sample=units)

        merged, sg, sr, w_o = _mixer_prompt(h_p, la_p, params, rope_p, lg, w_out, l,
                                            batch, seq, d_model)
        hp = _proj_out(merged, hp, w_o, lnw, lnb, alpha, _pick_tile(hp.shape[0], 512))
        gla_p.append(sg)
        ret_p.append(sr)

        merged, sg, sr = merged_s, sg_s, sr_s
        hs = _proj_out(merged, hs, w_o, lnw, lnb, alpha, _pick_tile(n_rows_s, 256))
        gla_s.append(sg)
        ret_s.append(sr)

    cat = lambda parts: parts[0] if len(parts) == 1 else jnp.concatenate(parts, axis=0)
    return (hp.reshape(batch, seq, d_model), hs,
            cat(gla_p), cat(ret_p), cat(gla_s), cat(ret_s))
```

```python
import functools

import jax
import jax.numpy as jnp
from jax import lax
from jax.experimental import pallas as pl
from jax.experimental.pallas import tpu as pltpu

GLA_HEADS = 4
RET_HEADS = 8
HEAD_GROUPS = 4
GLA_TAU = 16.0
LOG2_E = 1.4426950408889634
ROPE_BASE = 10000.0
LN_EPS = 1e-5
HEAD_NORM_EPS = 1e-6
PAST_LEN = 16384

LANES = 128
SUBLANES = 8
VMEM_LIMIT_BYTES = 56 * 1024 * 1024
VMEM_TEMP_FACTOR = 1.25

PROMPT_CHUNK = 128
GLA_SUB = 16
SAMPLE_SEQS = 4

_F32 = jnp.float32
_BF16 = jnp.bfloat16
_NT = (((1,), (1,)), ((), ()))
_TN = (((0,), (0,)), ((), ()))


def _vmem_limit(in_specs, operands, out_specs, out_shape, scratch=()):
    def nbytes(shape, dtype):
        n = jnp.dtype(dtype).itemsize
        for s in shape:
            n *= s
        return n
    total = sum(2 * nbytes(spec.block_shape, op.dtype) for spec, op in zip(in_specs, operands))
    total += sum(2 * nbytes(spec.block_shape, o.dtype) for spec, o in zip(out_specs, out_shape))
    total += sum(nbytes(s.shape, s.dtype) for s in scratch)
    mib = 1024 * 1024
    return min(VMEM_LIMIT_BYTES, -(-int(total * VMEM_TEMP_FACTOR) // mib) * mib)


def _dot(a, b):
    return jnp.dot(a.astype(_BF16), b.astype(_BF16), preferred_element_type=_F32)


def _dot_nt(a, b):
    return lax.dot_general(a.astype(_BF16), b.astype(_BF16), _NT, preferred_element_type=_F32)


def _dot_tn(a, b):
    return lax.dot_general(a.astype(_BF16), b.astype(_BF16), _TN, preferred_element_type=_F32)


def _log_sigmoid(x):
    return jnp.minimum(x, 0.0) - jnp.log(1.0 + jnp.exp(-jnp.abs(x)))


def _iota(shape, axis):
    return lax.broadcasted_iota(jnp.int32, shape, axis)


def _seg_cumsum(x, seg):
    pos = _iota(x.shape, 0) & (seg - 1)
    s = 1
    while s < seg:
        x = x + jnp.where(pos >= s, pltpu.roll(x, s, axis=0), 0.0)
        s *= 2
    return x


def _cumsum_rows_mxu(x):
    n = x.shape[0]
    hi = x.astype(_BF16)
    r1 = x - hi.astype(_F32)
    mid = r1.astype(_BF16)
    lo = (r1 - mid.astype(_F32)).astype(_BF16)
    tri3 = (_iota((n, 3 * n), 0) >= (_iota((n, 3 * n), 1) & (n - 1))).astype(_BF16)
    return jnp.dot(tri3, jnp.concatenate([hi, mid, lo], axis=0), preferred_element_type=_F32)


def _rows_to_cols(rows):
    n = rows[0].shape[1]
    rid = _iota((LANES, n), 0)
    stack = jnp.zeros((LANES, n), _F32)
    for i, r in enumerate(rows):
        stack = jnp.where(rid == i, r, stack)
    return stack.T


def _rope(x, cos, sin_signed):
    w = x.shape[1]
    even = (_iota(x.shape, 1) & 1) == 0
    swapped = jnp.where(even, pltpu.roll(x, w - 1, axis=1), pltpu.roll(x, 1, axis=1))
    return x * cos + swapped * sin_signed


def _gla_pair_scores(a, q, k, b2, lane0, row_lo_hi):
    n = q.shape[0]
    lanes = _iota((SUBLANES, a.shape[1]), 1)
    tiles = [a[r:r + SUBLANES] for r in range(0, n, SUBLANES)]
    for s in range(n):
        lo, hi = row_lo_hi(s)
        for j, r0 in enumerate(range(0, n, SUBLANES)):
            if r0 + SUBLANES <= lo or r0 >= hi:
                continue
            e = jnp.exp2(b2[r0:r0 + SUBLANES] - b2[s:s + 1, :])
            col = jnp.sum(q[r0:r0 + SUBLANES] * k[s:s + 1, :] * e, axis=-1, keepdims=True)
            take = lanes == lane0 + s
            if lo > r0 or hi < r0 + SUBLANES:
                rows = _iota((SUBLANES, 1), 0) + r0
                take = take & (rows >= lo) & (rows < hi)
            tiles[j] = jnp.where(take, col, tiles[j])
    return jnp.concatenate(tiles, axis=0)


def _rms_heads(o, w):
    return o * lax.rsqrt(jnp.mean(o * o, axis=-1, keepdims=True) + HEAD_NORM_EPS) * w


def _group_norm(o, w, b):
    mu = jnp.mean(o, axis=-1, keepdims=True)
    d = o - mu
    var = jnp.mean(d * d, axis=-1, keepdims=True)
    return d * lax.rsqrt(var + HEAD_NORM_EPS) * w + b


def _prep_x_kernel(x_ref, wcol_ref, wlr_ref, blr_ref, xb_ref, la_ref, *, rank):
    x = x_ref[...]
    if x.ndim == 3:
        x = x.reshape(x.shape[0] * x.shape[1], x.shape[2])
    xb = x.astype(_BF16)
    xb_ref[...] = xb
    wrow = wcol_ref[0]
    w_lra = jnp.concatenate([wrow, jnp.zeros((LANES - rank, wrow.shape[1]), _F32)], axis=0)
    lr = _dot_nt(xb, w_lra)
    w = wlr_ref[...]
    w_hi = w.astype(_BF16).astype(_F32)
    w_lo = w - w_hi
    lr_hi = lr.astype(_BF16).astype(_F32)
    lr_lo = lr - lr_hi
    lr_cat = lr_hi + pltpu.roll(lr_hi, rank, axis=1) + pltpu.roll(lr_lo, 2 * rank, axis=1)
    w_cat = jnp.concatenate([w_hi[:rank], w_lo[:rank], w_hi[:rank],
                             jnp.zeros((LANES - 3 * rank, w.shape[1]), _F32)], axis=0)
    logit = _dot(lr_cat, w_cat) + blr_ref[...]
    la_ref[...] = _log_sigmoid(logit) * (1.0 / GLA_TAU)


def _prep_x(x, w_t, layer, lr_col, rank, w_lr, b_lr, tm):
    kdim = x.shape[-1]
    m = x.size // kdim
    qk = w_lr.shape[1]
    assert lr_col % rank == 0 and rank % SUBLANES == 0 and 3 * rank <= LANES
    whole = lambda a: pl.BlockSpec(a.shape, lambda i: (0,) * a.ndim)
    if x.ndim == 3:
        x_spec = pl.BlockSpec((tm // x.shape[1], x.shape[1], kdim), lambda i: (i, 0, 0))
    else:
        x_spec = pl.BlockSpec((tm, kdim), lambda i: (i, 0))
    operands = [x, w_t, w_lr, b_lr]
    in_specs = [x_spec,
                pl.BlockSpec((1, rank, kdim), lambda i: (layer, lr_col // rank, 0)),
                whole(w_lr), whole(b_lr)]
    out_specs = [pl.BlockSpec((tm, kdim), lambda i: (i, 0)),
                 pl.BlockSpec((tm, qk), lambda i: (i, 0))]
    out_shape = [jax.ShapeDtypeStruct((m, kdim), _BF16),
                 jax.ShapeDtypeStruct((m, qk), _F32)]
    return pl.pallas_call(
        functools.partial(_prep_x_kernel, rank=rank),
        grid=(m // tm,),
        in_specs=in_specs,
        out_specs=out_specs,
        out_shape=out_shape,
        compiler_params=pltpu.CompilerParams(
            dimension_semantics=("arbitrary",),
            vmem_limit_bytes=_vmem_limit(in_specs, operands, out_specs, out_shape)),
        name="prep_x",
    )(*operands)


N_SAMPLE_IN = 19


def _proj_in_kernel(*refs, n_plain, shift, silu_tiles, sigmoid_tiles, sample_steps):
    x_ref, wa_ref, we_ref = refs[:3]
    wbf_ref = refs[-1]
    if sample_steps:
        unit_in = refs[3:3 + N_SAMPLE_IN]
        h_ref = refs[3 + N_SAMPLE_IN]
        unit_out = refs[4 + N_SAMPLE_IN:7 + N_SAMPLE_IN]
    else:
        h_ref = refs[3]
    j = pl.program_id(0)
    i = pl.program_id(1)
    tn, kdim = wbf_ref.shape

    @pl.when(i == 0)
    def _():
        @pl.when(j < n_plain)
        def _():
            wbf_ref[...] = wa_ref[0].astype(_BF16)

        @pl.when(j >= n_plain)
        def _():
            wbf_ref[0:tn - shift, :] = wa_ref[0, shift:tn, :].astype(_BF16)
            wbf_ref[tn - shift:tn, :] = we_ref[0].astype(_BF16)

    is_silu = functools.reduce(jnp.logical_or, [j == t for t in silu_tiles])
    is_sigmoid = functools.reduce(jnp.logical_or, [j == t for t in sigmoid_tiles])

    def slab(cols, activated):
        acc = lax.dot_general(x_ref[...], wbf_ref[cols, :], _NT, preferred_element_type=_F32)
        if activated:
            s = 0.5 * jnp.tanh(0.5 * acc) + 0.5
            acc = jnp.where(is_sigmoid, s, acc * s)
        h_ref[:, cols] = acc

    def tile(activated):
        finish_unit = None
        if sample_steps:
            finish_unit = _mixer_sample_unit(*unit_in, *unit_out, steps=sample_steps)
        slab(slice(0, tn // 4), activated)
        if finish_unit is not None:
            finish_unit()
        slab(slice(tn // 4, tn), activated)

    @pl.when(is_silu | is_sigmoid)
    def _():
        tile(True)

    @pl.when(jnp.logical_not(is_silu | is_sigmoid))
    def _():
        tile(False)


PROJ_TILES = 16


def _proj_in(x, w_t, layer, rank, tm, sample=None):
    m, kdim = x.shape
    d_model = kdim
    tn = d_model // 2
    nj = PROJ_TILES
    n_plain = 6
    bf16_rows = 2 * SUBLANES
    assert rank % bf16_rows == 0 and tn % rank == 0 and m % tm == 0
    ni = m // tm

    def row_block(j, i):
        return jnp.where(j % 2 == 0, i, ni - 1 - i)
    in_specs = [
        pl.BlockSpec((tm, kdim), lambda j, i: (row_block(j, i), 0)),
        pl.BlockSpec((1, tn, kdim), lambda j, i: (layer, j, 0)),
        pl.BlockSpec((1, rank, kdim), lambda j, i: (layer, (j + 1) * (tn // rank), 0)),
    ]
    out_specs = [pl.BlockSpec((tm, tn), lambda j, i: (row_block(j, i), j))]
    out_shape = [jax.ShapeDtypeStruct((m, nj * tn), _F32)]
    operands = [x, w_t, w_t]
    sample_steps = 0
    if sample is not None:
        unit_operands, sample_steps, n_units, unit_specs = sample
        assert n_units <= nj * ni

        def unit_of(j, i):
            u = jnp.minimum(j * ni + i, n_units - 1)
            return u // HEAD_GROUPS, u % HEAD_GROUPS
        unit_in, unit_out, unit_shape = unit_specs(unit_of)
        assert len(unit_in) == N_SAMPLE_IN
        in_specs += unit_in
        out_specs += unit_out
        out_shape += unit_shape
        operands += unit_operands
    kern = functools.partial(_proj_in_kernel, n_plain=n_plain, shift=rank,
                             silu_tiles=(4, 5, 10, 11), sigmoid_tiles=(12, 13, 14, 15),
                             sample_steps=sample_steps)
    scratch = [pltpu.VMEM((tn, kdim), _BF16)]
    return pl.pallas_call(
        kern,
        grid=(nj, ni),
        in_specs=in_specs,
        out_specs=out_specs,
        out_shape=out_shape,
        scratch_shapes=scratch,
        compiler_params=pltpu.CompilerParams(
            dimension_semantics=("arbitrary", "arbitrary"),
            vmem_limit_bytes=_vmem_limit(in_specs, operands, out_specs, out_shape, scratch)),
        name="proj_in_mix" if sample is not None else "proj_in",
    )(*operands)


def _h_cols(d_model, g):
    wq, wv = d_model // 8, d_model // 4
    d = d_model
    offs = dict(qa=(0, wq), ka=(d // 2, wq), va=(d, wv), za=(2 * d, wv),
                qb=(3 * d, wq), kb=(3 * d + d // 2, wq), vb=(4 * d, wv), zb=(5 * d, wv),
                ga=(6 * d, wv), gb=(7 * d, wv))
    return {n: slice(o + g * w, o + (g + 1) * w) for n, (o, w) in offs.items()}


def _mixer_prompt_kernel(h_ref, la_ref, gnw_ref, rnw_ref, rnb_ref,
                         cos_ref, sin_ref, lg_ref, wout_ref, m_ref, sg_ref, sr_ref, wout_bf_ref,
                         qdec_ref, kdec_ref, dmat_ref):
    wout_bf_ref[...] = wout_ref[0].astype(_BF16)
    chunk = h_ref.shape[0]
    d_model = m_ref.shape[1]
    dk_a, dv_a = sg_ref.shape[3:]
    dk_b, dv_b = sr_ref.shape[3:]

    @pl.when(pl.program_id(1) == 0)
    def _():
        sg_ref[...] = jnp.zeros(sg_ref.shape, _F32)
        sr_ref[...] = jnp.zeros(sr_ref.shape, _F32)

    @pl.when((pl.program_id(0) == 0) & (pl.program_id(1) == 0))
    def _():
        tpos = (_iota((chunk, 2 * dk_b), 0) + 1).astype(_F32)
        tdiff = _iota((chunk, chunk), 0) - _iota((chunk, chunk), 1)
        for g in range(HEAD_GROUPS):
            lg = lg_ref[g]
            qdec_ref[g] = jnp.exp(lg * tpos)
            kdec_ref[g] = jnp.exp(lg * (chunk - tpos))
            for hh in range(2):
                lg1 = lg[:, hh * dk_b:hh * dk_b + 1]
                dmat_ref[2 * g + hh] = jnp.exp(
                    jnp.where(tdiff >= 0, lg1 * tdiff.astype(_F32), -jnp.inf))

    rows = _iota((chunk, 1), 0)
    cos = jnp.concatenate([cos_ref[...]] * 2, axis=1)
    sin = jnp.concatenate([sin_ref[...]] * 2, axis=1)
    n_sub = chunk // GLA_SUB
    groups = range(HEAD_GROUPS)
    cols = [_h_cols(d_model, g) for g in groups]
    ksl = [slice(hh * dk_b, (hh + 1) * dk_b) for hh in range(2)]
    vsl = [slice(hh * dv_b, (hh + 1) * dv_b) for hh in range(2)]


    gla, ret = [], []
    for g in groups:
        c = cols[g]
        b = _cumsum_rows_mxu(la_ref[:, g * dk_a:(g + 1) * dk_a]) * LOG2_E
        q = h_ref[:, c["qa"]] * (dk_a ** -0.5)
        k = h_ref[:, c["ka"]]
        v = h_ref[:, c["va"]].astype(_BF16)
        s_gla = sg_ref[0, 0, g]
        gla.append(dict(b=b, q=q, k=k, v=v, s=s_gla, o=_dot(q * jnp.exp2(b), s_gla)))

        qb = _rope(h_ref[:, c["qb"]], cos, sin)
        kb = _rope(h_ref[:, c["kb"]], cos, sin) * (dk_b ** -0.5)
        vb = h_ref[:, c["vb"]].astype(_BF16)
        lg = lg_ref[g]
        q_dec = qb * qdec_ref[g]
        s_ret = [sr_ref[0, 0, 2 * g + hh] for hh in range(2)]
        ret.append(dict(
            vb=vb, lg=lg, s=s_ret, k_dec=kb * kdec_ref[g],
            o=[_dot(q_dec[:, ksl[hh]], s_ret[hh]) for hh in range(2)],
            qk=[_dot_nt(qb[:, ksl[hh]], kb[:, ksl[hh]]) for hh in range(2)]))

    for g in groups:
        q, k, b = gla[g]["q"], gla[g]["k"], gla[g]["b"]
        a_off = [jnp.zeros((GLA_SUB, chunk), _F32)]
        for i in range(1, n_sub):
            lo = i * GLA_SUB
            r = b[lo - 1:lo, :]
            k_pre = k * jnp.exp2(jnp.where(rows < lo, r - b, -jnp.inf))
            a_off.append(_dot_nt(q[lo:lo + GLA_SUB] * jnp.exp2(b[lo:lo + GLA_SUB] - r), k_pre))
        gla[g]["a"] = a_off

    for g in groups:
        q, k, b, v = gla[g]["q"], gla[g]["k"], gla[g]["b"], gla[g]["v"]
        intra = []
        for i in range(n_sub):
            lo = i * GLA_SUB
            sub = slice(lo, lo + GLA_SUB)
            a = _gla_pair_scores(gla[g]["a"][i], q[sub], k[sub], b[sub], lo, lambda s: (s, GLA_SUB))
            intra.append(_dot(a, v))
        gla[g]["o"] = gla[g]["o"] + jnp.concatenate(intra, axis=0)
        for hh in range(2):
            scores = ret[g]["qk"][hh] * dmat_ref[2 * g + hh]
            ret[g]["o"][hh] = ret[g]["o"][hh] + _dot(scores, ret[g]["vb"][:, vsl[hh]])

    for g in groups:
        k, b, v = gla[g]["k"], gla[g]["b"], gla[g]["v"]
        b_last = b[chunk - 1:chunk, :]
        decay_col = _rows_to_cols([jnp.exp2(b_last)])[:, 0:1]
        sg_ref[0, 0, g] = decay_col * gla[g]["s"] + _dot_tn(k * jnp.exp2(b_last - b), v)
        for hh in range(2):
            lg1 = ret[g]["lg"][:, hh * dk_b:hh * dk_b + 1]
            sr_ref[0, 0, 2 * g + hh] = (jnp.exp(lg1 * chunk) * ret[g]["s"][hh]
                                        + _dot_tn(ret[g]["k_dec"][:, ksl[hh]], ret[g]["vb"][:, vsl[hh]]))

    for g in groups:
        c = cols[g]
        gv = slice(g * dv_a, (g + 1) * dv_a)
        o_a = _rms_heads(gla[g]["o"], gnw_ref[...]) * h_ref[:, c["za"]]
        o_b = []
        for hh in range(2):
            nsl = slice(g * dv_a + hh * dv_b, g * dv_a + (hh + 1) * dv_b)
            o_b.append(_group_norm(ret[g]["o"][hh], rnw_ref[:, nsl], rnb_ref[:, nsl]))
        o_b = jnp.concatenate(o_b, axis=1) * h_ref[:, c["zb"]]
        merged = h_ref[:, c["ga"]] * o_a + h_ref[:, c["gb"]] * o_b
        m_ref[:, gv] = merged.astype(m_ref.dtype)


def _mixer_prompt(h, la, params, rope, lg, w_out, layer, batch, seq, d_model):
    wq, wv = d_model // 8, d_model // 4
    dk_a, dv_a = wq, wv
    dk_b, dv_b = wq // 2, wv // 2
    chunk = PROMPT_CHUNK
    nc = seq // chunk
    e_rows = w_out.shape[1]
    bf16_rows = 2 * SUBLANES
    assert e_rows % (batch * nc) == 0 and (e_rows // (batch * nc)) % bf16_rows == 0
    w_rows = e_rows // (batch * nc)
    whole = lambda a: pl.BlockSpec(a.shape, lambda b, c: (0,) * a.ndim)
    in_specs = [pl.BlockSpec((chunk, h.shape[1]), lambda b, c: (b * nc + c, 0)),
                pl.BlockSpec((chunk, la.shape[1]), lambda b, c: (b * nc + c, 0))]
    in_specs += [whole(p) for p in params]
    in_specs += [pl.BlockSpec((chunk, dk_b), lambda b, c: (c, 0)),
                 pl.BlockSpec((chunk, dk_b), lambda b, c: (c, 0)),
                 whole(lg),
                 pl.BlockSpec((1, w_rows, w_out.shape[2]), lambda b, c: (layer, b * nc + c, 0))]
    out_specs = [
        pl.BlockSpec((chunk, d_model), lambda b, c: (b * nc + c, 0)),
        pl.BlockSpec((1, 1, GLA_HEADS, dk_a, dv_a), lambda b, c: (0, b, 0, 0, 0)),
        pl.BlockSpec((1, 1, RET_HEADS, dk_b, dv_b), lambda b, c: (0, b, 0, 0, 0)),
        pl.BlockSpec((w_rows, w_out.shape[2]), lambda b, c: (b * nc + c, 0)),
    ]
    out_shape = [
        jax.ShapeDtypeStruct((batch * seq, d_model), _BF16),
        jax.ShapeDtypeStruct((1, batch, GLA_HEADS, dk_a, dv_a), _F32),
        jax.ShapeDtypeStruct((1, batch, RET_HEADS, dk_b, dv_b), _F32),
        jax.ShapeDtypeStruct(w_out.shape[1:], _BF16),
    ]
    operands = [h, la, *params, *rope, lg, w_out]
    scratch = [pltpu.VMEM((HEAD_GROUPS, chunk, wq), _F32),
               pltpu.VMEM((HEAD_GROUPS, chunk, wq), _F32),
               pltpu.VMEM((RET_HEADS, chunk, chunk), _F32)]
    return pl.pallas_call(
        _mixer_prompt_kernel,
        grid=(batch, nc),
        in_specs=in_specs,
        out_specs=out_specs,
        out_shape=out_shape,
        scratch_shapes=scratch,
        compiler_params=pltpu.CompilerParams(
            dimension_semantics=("arbitrary", "arbitrary"),
            vmem_limit_bytes=_vmem_limit(in_specs, operands, out_specs, out_shape, scratch)),
        name="mixer_prompt",
    )(*operands)


def _mixer_sample_unit(qa_ref, ka_ref, va_ref, za_ref, qb_ref, kb_ref, vb_ref, zb_ref,
                       ga_ref, gb_ref, la_ref, gnw_ref, rnw_ref, rnb_ref,
                       cos_ref, sin_ref, lg_ref, sg0_ref, sr0_ref, m_ref, sg_ref, sr_ref,
                       *, steps):
    rows_n, dk_a = qa_ref.shape
    nseq = rows_n // steps
    dk_b = qb_ref.shape[1] // 2
    dv_b = vb_ref.shape[1] // 2
    rows = _iota((rows_n, 1), 0)
    rows_p = _iota((LANES, 1), 0)
    pad = LANES - rows_n
    ksl = [slice(hh * dk_b, (hh + 1) * dk_b) for hh in range(2)]
    vsl = [slice(hh * dv_b, (hh + 1) * dv_b) for hh in range(2)]

    def pad_rows(x):
        return jnp.concatenate([x, jnp.zeros((pad, x.shape[1]), x.dtype)], axis=0)

    def seq_rows(x, n, r=rows):
        return jnp.where((r >= n * steps) & (r < (n + 1) * steps), x, 0.0)

    b = _seg_cumsum(la_ref[...], steps) * LOG2_E
    q = qa_ref[...] * (dk_a ** -0.5)
    k = ka_ref[...]
    v = pad_rows(va_ref[...]).astype(_BF16)
    q_dec = q * jnp.exp2(b)
    s_gla = [sg0_ref[0, n, 0] for n in range(nseq)]
    o_a = _dot(seq_rows(q_dec, 0), s_gla[0])
    for n in range(1, nseq):
        o_a = o_a + _dot(seq_rows(q_dec, n), s_gla[n])

    cos = jnp.concatenate([cos_ref[...]] * 2, axis=1)
    sin = jnp.concatenate([sin_ref[...]] * 2, axis=1)
    qb = _rope(qb_ref[...], cos, sin)
    kb = _rope(kb_ref[...], cos, sin) * (dk_b ** -0.5)
    vb = pad_rows(vb_ref[...]).astype(_BF16)
    lg = lg_ref[0]
    lg1 = [lg[:, hh * dk_b:hh * dk_b + 1] for hh in range(2)]
    tpos = ((_iota(qb.shape, 0) & (steps - 1)) + 1).astype(_F32)
    qb_dec = qb * jnp.exp(lg * tpos)
    kb_p = pad_rows(kb)
    s_ret = [[sr0_ref[0, n, hh] for n in range(nseq)] for hh in range(2)]
    o_r, qk = [], []
    for hh in range(2):
        o_h = _dot(seq_rows(qb_dec[:, ksl[hh]], 0), s_ret[hh][0])
        for n in range(1, nseq):
            o_h = o_h + _dot(seq_rows(qb_dec[:, ksl[hh]], n), s_ret[hh][n])
        o_r.append(o_h)
        qk.append(_dot_nt(qb[:, ksl[hh]], kb_p[:, ksl[hh]]))

    a = jnp.zeros((rows_n, LANES), _F32)
    a = _gla_pair_scores(a, q, k, b, 0, lambda s: (s, (s // steps + 1) * steps))
    o_a = o_a + _dot(a, v)
    r_i = _iota((rows_n, LANES), 0)
    c_i = _iota((rows_n, LANES), 1)
    same_seq = (r_i >> (steps.bit_length() - 1)) == (c_i >> (steps.bit_length() - 1))
    tdiff = r_i - c_i
    for hh in range(2):
        decay = jnp.exp(jnp.where((tdiff >= 0) & same_seq, lg1[hh] * tdiff.astype(_F32), -jnp.inf))
        o_r[hh] = o_r[hh] + _dot(qk[hh] * decay, vb[:, vsl[hh]])

    def finish():
        b_last = [b[(n + 1) * steps - 1:(n + 1) * steps, :] for n in range(nseq)]
        b_end = jnp.zeros_like(b)
        for n in range(nseq):
            b_end = b_end + seq_rows(jnp.broadcast_to(b_last[n], b.shape), n)
        k_dec = k * jnp.exp2(b_end - b)
        decay_cols = _rows_to_cols([jnp.exp2(r) for r in b_last])
        for n in range(nseq):
            sg_ref[0, n, 0] = (decay_cols[:, n:n + 1] * s_gla[n]
                               + _dot_tn(seq_rows(k_dec, n), v[:rows_n]))
        kb_dec = kb * jnp.exp(lg * (steps - tpos))
        for hh in range(2):
            for n in range(nseq):
                sr_ref[0, n, hh] = (jnp.exp(lg1[hh] * steps) * s_ret[hh][n]
                                    + _dot_tn(seq_rows(kb_dec[:, ksl[hh]], n), vb[:rows_n, vsl[hh]]))
        o_an = _rms_heads(o_a, gnw_ref[...]) * za_ref[...]
        o_bn = jnp.concatenate([_group_norm(o_r[hh], rnw_ref[:, vsl[hh]], rnb_ref[:, vsl[hh]])
                                for hh in range(2)], axis=1) * zb_ref[...]
        merged = ga_ref[...] * o_an + gb_ref[...] * o_bn
        m_ref[...] = merged.astype(m_ref.dtype)

    return finish


def _sample_units(h, la, params, rope, lg, state_gla, state_ret, layer, nseq_total, steps, d_model):
    wq, wv = d_model // 8, d_model // 4
    dk_a, dv_a = wq, wv
    dk_b, dv_b = wq // 2, wv // 2
    nseq = SAMPLE_SEQS
    rows_n = nseq * steps
    assert steps & (steps - 1) == 0 and nseq_total % nseq == 0
    d = d_model

    def specs(unit_of):
        def at(fn):
            return lambda *ids: fn(*unit_of(*ids))

        def h_spec(width, col_off):
            base = col_off // width
            return pl.BlockSpec((rows_n, width), at(lambda p, g: (p, base + g)))
        in_specs = [h_spec(wq, 0), h_spec(wq, d // 2), h_spec(wv, d), h_spec(wv, 2 * d),
                    h_spec(wq, 3 * d), h_spec(wq, 3 * d + d // 2), h_spec(wv, 4 * d),
                    h_spec(wv, 5 * d), h_spec(wv, 6 * d), h_spec(wv, 7 * d)]
        in_specs += [
            pl.BlockSpec((rows_n, wq), at(lambda p, g: (p, g))),
            pl.BlockSpec((1, wv), at(lambda p, g: (0, 0))),
            pl.BlockSpec((1, wv), at(lambda p, g: (0, g))),
            pl.BlockSpec((1, wv), at(lambda p, g: (0, g))),
            pl.BlockSpec((rows_n, dk_b), at(lambda p, g: (0, 0))),
            pl.BlockSpec((rows_n, dk_b), at(lambda p, g: (0, 0))),
            pl.BlockSpec((1, 1, wq), at(lambda p, g: (g, 0, 0))),
            pl.BlockSpec((1, nseq, 1, dk_a, dv_a), at(lambda p, g: (layer, p, g, 0, 0))),
            pl.BlockSpec((1, nseq, 2, dk_b, dv_b), at(lambda p, g: (layer, p, g, 0, 0))),
        ]
        out_specs = [
            pl.BlockSpec((rows_n, wv), at(lambda p, g: (p, g))),
            pl.BlockSpec((1, nseq, 1, dk_a, dv_a), at(lambda p, g: (0, p, g, 0, 0))),
            pl.BlockSpec((1, nseq, 2, dk_b, dv_b), at(lambda p, g: (0, p, g, 0, 0))),
        ]
        out_shape = [
            jax.ShapeDtypeStruct((nseq_total * steps, d_model), _BF16),
            jax.ShapeDtypeStruct((1, nseq_total, GLA_HEADS, dk_a, dv_a), _F32),
            jax.ShapeDtypeStruct((1, nseq_total, RET_HEADS, dk_b, dv_b), _F32),
        ]
        return in_specs, out_specs, out_shape

    operands = [h] * 10 + [la, *params, *rope, lg, state_gla, state_ret]
    n_units = (nseq_total // nseq) * HEAD_GROUPS
    return operands, steps, n_units, specs


PROJ_OUT_SLABS = 4

def _out_proj_ln(m, x, w, ln_w, ln_b, alpha):
    r = alpha * x + jnp.dot(m, w, preferred_element_type=_F32)
    mu = jnp.mean(r, axis=-1, keepdims=True)
    d = r - mu
    var = jnp.mean(d * d, axis=-1, keepdims=True)
    return d * lax.rsqrt(var + LN_EPS) * ln_w + ln_b


def _proj_out_kernel(m_ref, x_ref, w_ref, lnw_ref, lnb_ref, y_ref, *, alpha):
    tm = m_ref.shape[0]
    slab = min(tm, max(tm // PROJ_OUT_SLABS, LANES))
    for r in range(0, tm, slab):
        if x_ref.ndim == 3:
            steps = x_ref.shape[1]
            seqs = slice(r // steps, (r + slab) // steps)
            x = x_ref[seqs].reshape(slab, x_ref.shape[2])
        else:
            x = x_ref[r:r + slab, :]
        y = _out_proj_ln(m_ref[r:r + slab, :], x, w_ref[...], lnw_ref[...], lnb_ref[...], alpha)
        if x_ref.ndim == 3:
            y_ref[seqs] = y.reshape(slab // steps, steps, y.shape[1])
        else:
            y_ref[r:r + slab, :] = y


def _proj_out(merged, x, w_out, ln_w, ln_b, alpha, tm):
    d = x.shape[-1]
    m = x.size // d
    e = merged.shape[1]
    if x.ndim == 3:
        xy_spec = pl.BlockSpec((tm // x.shape[1], x.shape[1], d), lambda i: (i, 0, 0))
    else:
        xy_spec = pl.BlockSpec((tm, d), lambda i: (i, 0))
    operands = [merged, x, w_out, ln_w, ln_b]
    in_specs = [
        pl.BlockSpec((tm, e), lambda i: (i, 0)),
        xy_spec,
        pl.BlockSpec((e, d), lambda i: (0, 0)),
        pl.BlockSpec((1, d), lambda i: (0, 0)),
        pl.BlockSpec((1, d), lambda i: (0, 0)),
    ]
    out_shape = jax.ShapeDtypeStruct(x.shape, _F32)
    return pl.pallas_call(
        functools.partial(_proj_out_kernel, alpha=alpha),
        grid=(m // tm,),
        in_specs=in_specs,
        out_specs=xy_spec,
        out_shape=out_shape,
        compiler_params=pltpu.CompilerParams(
            dimension_semantics=("arbitrary",),
            vmem_limit_bytes=_vmem_limit(in_specs, operands, [xy_spec], [out_shape])),
        name="proj_out",
    )(*operands)


def _rope_tables(pos, dk):
    inv = 1.0 / (ROPE_BASE ** jnp.linspace(0.0, 1.0, dk // 2, dtype=_F32))
    inv_lanes = jnp.repeat(inv, 2)
    sign = jnp.tile(jnp.asarray([-1.0, 1.0], _F32), dk // 2)
    ang = pos.astype(_F32)[:, None] * inv_lanes[None, :]
    return jnp.cos(ang), jnp.sin(ang) * sign[None, :]


def _pick_tile(n, pref):
    t = min(n, pref)
    while n % t:
        t //= 2
    return t


def kernel(x_prompt, x_sample, state_gla, state_ret, w_in, w_lr, b_lr, gla_norm_w,
           ret_norm_w, ret_norm_b, w_out, ln_w, ln_b):
    depth, d_model, _ = w_in.shape
    batch, seq, _ = x_prompt.shape
    dec_batch, dec_seq, _ = x_sample.shape
    rank = w_lr.shape[1]
    dk_b = d_model // 2 // RET_HEADS
    assert state_gla.shape[2] == GLA_HEADS and state_ret.shape[2] == RET_HEADS
    assert rank <= LANES and seq % PROMPT_CHUNK == 0
    alpha = (2.0 * depth) ** 0.25

    lg_heads = jnp.log(1.0 - 2.0 ** (-5.0 - jnp.arange(RET_HEADS, dtype=_F32)))
    lg = jnp.repeat(lg_heads, dk_b).reshape(HEAD_GROUPS, 1, 2 * dk_b)
    rope_p = _rope_tables(jnp.arange(seq, dtype=jnp.int32), dk_b)
    pos_s = PAST_LEN + jnp.arange(dec_seq, dtype=jnp.int32)
    rope_s = _rope_tables(jnp.tile(pos_s, SAMPLE_SEQS), dk_b)

    hp = x_prompt.reshape(batch * seq, d_model)
    hs = x_sample
    n_rows_s = dec_batch * dec_seq
    w_t = jnp.swapaxes(w_in, 1, 2)
    gla_p, ret_p, gla_s, ret_s = [], [], [], []
    for l in range(depth):
        lr_lo = 3 * d_model
        w_lr_p = jnp.pad(w_lr[l], ((0, LANES - rank), (0, 0)))
        params = (gla_norm_w[l][None, :], ret_norm_w[l][None, :], ret_norm_b[l][None, :])
        lnw, lnb = ln_w[l][None, :], ln_b[l][None, :]

        xp_b, la_p = _prep_x(hp, w_t, l, lr_lo, rank, w_lr_p, b_lr[l][None, :],
                             _pick_tile(hp.shape[0], 1024))
        xs_b, la_s = _prep_x(hs, w_t, l, lr_lo, rank, w_lr_p, b_lr[l][None, :],
                             _pick_tile(n_rows_s, 512))
        (h_s,) = _proj_in(xs_b, w_t, l, rank, _pick_tile(xs_b.shape[0], 1024))
        units = _sample_units(h_s, la_s, params, rope_s, lg, state_gla, state_ret, l,
                              dec_batch, dec_seq, d_model)
        h_p, merged_s, sg_s, sr_s = _proj_in(xp_b, w_t, l, rank, _pick_tile(xp_b.shape[0], 1024),
                                             sample=units)

        merged, sg, sr, w_o = _mixer_prompt(h_p, la_p, params, rope_p, lg, w_out, l,
                                            batch, seq, d_model)
        hp = _proj_out(merged, hp, w_o, lnw, lnb, alpha, _pick_tile(hp.shape[0], 512))
        gla_p.append(sg)
        ret_p.append(sr)

        merged, sg, sr = merged_s, sg_s, sr_s
        hs = _proj_out(merged, hs, w_o, lnw, lnb, alpha, _pick_tile(n_rows_s, 256))
        gla_s.append(sg)
        ret_s.append(sr)

    cat = lambda parts: parts[0] if len(parts) == 1 else jnp.concatenate(parts, axis=0)
    return (hp.reshape(batch, seq, d_model), hs,
            cat(gla_p), cat(ret_p), cat(gla_s), cat(ret_s))
```

```python
import functools

import jax
import jax.numpy as jnp
from jax import lax
from jax.experimental import pallas as pl
from jax.experimental.pallas import tpu as pltpu

GLA_HEADS = 4
RET_HEADS = 8
HEAD_GROUPS = 4
GLA_TAU = 16.0
LOG2_E = 1.4426950408889634
ROPE_BASE = 10000.0
LN_EPS = 1e-5
HEAD_NORM_EPS = 1e-6
PAST_LEN = 16384

LANES = 128
SUBLANES = 8
VMEM_LIMIT_BYTES = 56 * 1024 * 1024
VMEM_TEMP_FACTOR = 1.15

PROMPT_CHUNK = 128
GLA_SUB = 16
SAMPLE_SEQS = 4

_F32 = jnp.float32
_BF16 = jnp.bfloat16
_NT = (((1,), (1,)), ((), ()))
_TN = (((0,), (0,)), ((), ()))


def _vmem_limit(in_specs, operands, out_specs, out_shape, scratch=()):
    def nbytes(shape, dtype):
        n = jnp.dtype(dtype).itemsize
        for s in shape:
            n *= s
        return n
    total = sum(2 * nbytes(spec.block_shape, op.dtype) for spec, op in zip(in_specs, operands))
    total += sum(2 * nbytes(spec.block_shape, o.dtype) for spec, o in zip(out_specs, out_shape))
    total += sum(nbytes(s.shape, s.dtype) for s in scratch)
    mib = 1024 * 1024
    return min(VMEM_LIMIT_BYTES, -(-int(total * VMEM_TEMP_FACTOR) // mib) * mib)


def _dot(a, b):
    return jnp.dot(a.astype(_BF16), b.astype(_BF16), preferred_element_type=_F32)


def _dot_nt(a, b):
    return lax.dot_general(a.astype(_BF16), b.astype(_BF16), _NT, preferred_element_type=_F32)


def _dot_tn(a, b):
    return lax.dot_general(a.astype(_BF16), b.astype(_BF16), _TN, preferred_element_type=_F32)


def _log_sigmoid(x):
    return jnp.minimum(x, 0.0) - jnp.log(1.0 + jnp.exp(-jnp.abs(x)))


def _iota(shape, axis):
    return lax.broadcasted_iota(jnp.int32, shape, axis)


def _seg_cumsum(x, seg):
    pos = _iota(x.shape, 0) & (seg - 1)
    s = 1
    while s < seg:
        x = x + jnp.where(pos >= s, pltpu.roll(x, s, axis=0), 0.0)
        s *= 2
    return x


def _cumsum_rows_mxu(x):
    n = x.shape[0]
    hi = x.astype(_BF16)
    r1 = x - hi.astype(_F32)
    mid = r1.astype(_BF16)
    lo = (r1 - mid.astype(_F32)).astype(_BF16)
    tri3 = (_iota((n, 3 * n), 0) >= (_iota((n, 3 * n), 1) & (n - 1))).astype(_BF16)
    return jnp.dot(tri3, jnp.concatenate([hi, mid, lo], axis=0), preferred_element_type=_F32)


def _rows_to_cols(rows):
    n = rows[0].shape[1]
    rid = _iota((LANES, n), 0)
    stack = jnp.zeros((LANES, n), _F32)
    for i, r in enumerate(rows):
        stack = jnp.where(rid == i, r, stack)
    return stack.T


def _rope(x, cos, sin_signed):
    w = x.shape[1]
    even = (_iota(x.shape, 1) & 1) == 0
    swapped = jnp.where(even, pltpu.roll(x, w - 1, axis=1), pltpu.roll(x, 1, axis=1))
    return x * cos + swapped * sin_signed


def _gla_pair_scores(a, q, k, b2, lane0, row_lo_hi):
    n = q.shape[0]
    lanes = _iota((SUBLANES, a.shape[1]), 1)
    tiles = [a[r:r + SUBLANES] for r in range(0, n, SUBLANES)]
    for s in range(n):
        lo, hi = row_lo_hi(s)
        for j, r0 in enumerate(range(0, n, SUBLANES)):
            if r0 + SUBLANES <= lo or r0 >= hi:
                continue
            e = jnp.exp2(b2[r0:r0 + SUBLANES] - b2[s:s + 1, :])
            col = jnp.sum(q[r0:r0 + SUBLANES] * k[s:s + 1, :] * e, axis=-1, keepdims=True)
            take = lanes == lane0 + s
            if lo > r0 or hi < r0 + SUBLANES:
                rows = _iota((SUBLANES, 1), 0) + r0
                take = take & (rows >= lo) & (rows < hi)
            tiles[j] = jnp.where(take, col, tiles[j])
    return jnp.concatenate(tiles, axis=0)


def _rms_heads(o, w):
    return o * lax.rsqrt(jnp.mean(o * o, axis=-1, keepdims=True) + HEAD_NORM_EPS) * w


def _group_norm(o, w, b):
    mu = jnp.mean(o, axis=-1, keepdims=True)
    d = o - mu
    var = jnp.mean(d * d, axis=-1, keepdims=True)
    return d * lax.rsqrt(var + HEAD_NORM_EPS) * w + b


def _prep_x_kernel(x_ref, wcol_ref, wlr_ref, blr_ref, xb_ref, la_ref, *, rank):
    x = x_ref[...]
    if x.ndim == 3:
        x = x.reshape(x.shape[0] * x.shape[1], x.shape[2])
    xb = x.astype(_BF16)
    xb_ref[...] = xb
    wrow = wcol_ref[0]
    w_lra = jnp.concatenate([wrow, jnp.zeros((LANES - rank, wrow.shape[1]), _F32)], axis=0)
    lr = _dot_nt(xb, w_lra)
    w = wlr_ref[...]
    w_hi = w.astype(_BF16).astype(_F32)
    w_lo = w - w_hi
    lr_hi = lr.astype(_BF16).astype(_F32)
    lr_lo = lr - lr_hi
    lr_cat = lr_hi + pltpu.roll(lr_hi, rank, axis=1) + pltpu.roll(lr_lo, 2 * rank, axis=1)
    w_cat = jnp.concatenate([w_hi[:rank], w_lo[:rank], w_hi[:rank],
                             jnp.zeros((LANES - 3 * rank, w.shape[1]), _F32)], axis=0)
    logit = _dot(lr_cat, w_cat) + blr_ref[...]
    la_ref[...] = _log_sigmoid(logit) * (1.0 / GLA_TAU)


def _prep_x(x, w_t, layer, lr_col, rank, w_lr, b_lr, tm):
    kdim = x.shape[-1]
    m = x.size // kdim
    qk = w_lr.shape[1]
    assert lr_col % rank == 0 and rank % SUBLANES == 0 and 3 * rank <= LANES
    whole = lambda a: pl.BlockSpec(a.shape, lambda i: (0,) * a.ndim)
    if x.ndim == 3:
        x_spec = pl.BlockSpec((tm // x.shape[1], x.shape[1], kdim), lambda i: (i, 0, 0))
    else:
        x_spec = pl.BlockSpec((tm, kdim), lambda i: (i, 0))
    operands = [x, w_t, w_lr, b_lr]
    in_specs = [x_spec,
                pl.BlockSpec((1, rank, kdim), lambda i: (layer, lr_col // rank, 0)),
                whole(w_lr), whole(b_lr)]
    out_specs = [pl.BlockSpec((tm, kdim), lambda i: (i, 0)),
                 pl.BlockSpec((tm, qk), lambda i: (i, 0))]
    out_shape = [jax.ShapeDtypeStruct((m, kdim), _BF16),
                 jax.ShapeDtypeStruct((m, qk), _F32)]
    return pl.pallas_call(
        functools.partial(_prep_x_kernel, rank=rank),
        grid=(m // tm,),
        in_specs=in_specs,
        out_specs=out_specs,
        out_shape=out_shape,
        compiler_params=pltpu.CompilerParams(
            dimension_semantics=("arbitrary",),
            vmem_limit_bytes=_vmem_limit(in_specs, operands, out_specs, out_shape)),
        name="prep_x",
    )(*operands)


N_SAMPLE_IN = 19


def _proj_in_kernel(*refs, n_plain, shift, silu_tiles, sigmoid_tiles, sample_steps):
    x_ref, wa_ref, we_ref = refs[:3]
    wbf_ref = refs[-1]
    if sample_steps:
        unit_in = refs[3:3 + N_SAMPLE_IN]
        h_ref = refs[3 + N_SAMPLE_IN]
        unit_out = refs[4 + N_SAMPLE_IN:7 + N_SAMPLE_IN]
    else:
        h_ref = refs[3]
    j = pl.program_id(0)
    i = pl.program_id(1)
    tn, kdim = wbf_ref.shape

    @pl.when(i == 0)
    def _():
        @pl.when(j < n_plain)
        def _():
            wbf_ref[...] = wa_ref[0].astype(_BF16)

        @pl.when(j >= n_plain)
        def _():
            wbf_ref[0:tn - shift, :] = wa_ref[0, shift:tn, :].astype(_BF16)
            wbf_ref[tn - shift:tn, :] = we_ref[0].astype(_BF16)

    is_silu = functools.reduce(jnp.logical_or, [j == t for t in silu_tiles])
    is_sigmoid = functools.reduce(jnp.logical_or, [j == t for t in sigmoid_tiles])

    def slab(cols, activated):
        acc = lax.dot_general(x_ref[...], wbf_ref[cols, :], _NT, preferred_element_type=_F32)
        if activated:
            s = 0.5 * jnp.tanh(0.5 * acc) + 0.5
            acc = jnp.where(is_sigmoid, s, acc * s)
        h_ref[:, cols] = acc

    def tile(activated):
        finish_unit = None
        if sample_steps:
            finish_unit = _mixer_sample_unit(*unit_in, *unit_out, steps=sample_steps)
        slab(slice(0, tn // 4), activated)
        if finish_unit is not None:
            finish_unit()
        slab(slice(tn // 4, tn), activated)

    @pl.when(is_silu | is_sigmoid)
    def _():
        tile(True)

    @pl.when(jnp.logical_not(is_silu | is_sigmoid))
    def _():
        tile(False)


PROJ_TILES = 16


def _proj_in(x, w_t, layer, rank, tm, sample=None):
    m, kdim = x.shape
    d_model = kdim
    tn = d_model // 2
    nj = PROJ_TILES
    n_plain = 6
    bf16_rows = 2 * SUBLANES
    assert rank % bf16_rows == 0 and tn % rank == 0 and m % tm == 0
    ni = m // tm

    def row_block(j, i):
        return jnp.where(j % 2 == 0, i, ni - 1 - i)
    in_specs = [
        pl.BlockSpec((tm, kdim), lambda j, i: (row_block(j, i), 0)),
        pl.BlockSpec((1, tn, kdim), lambda j, i: (layer, j, 0)),
        pl.BlockSpec((1, rank, kdim), lambda j, i: (layer, (j + 1) * (tn // rank), 0)),
    ]
    out_specs = [pl.BlockSpec((tm, tn), lambda j, i: (row_block(j, i), j))]
    out_shape = [jax.ShapeDtypeStruct((m, nj * tn), _F32)]
    operands = [x, w_t, w_t]
    sample_steps = 0
    if sample is not None:
        unit_operands, sample_steps, n_units, unit_specs = sample
        assert n_units <= nj * ni

        def unit_of(j, i):
            u = jnp.minimum(j * ni + i, n_units - 1)
            return u // HEAD_GROUPS, u % HEAD_GROUPS
        unit_in, unit_out, unit_shape = unit_specs(unit_of)
        assert len(unit_in) == N_SAMPLE_IN
        in_specs += unit_in
        out_specs += unit_out
        out_shape += unit_shape
        operands += unit_operands
    kern = functools.partial(_proj_in_kernel, n_plain=n_plain, shift=rank,
                             silu_tiles=(4, 5, 10, 11), sigmoid_tiles=(12, 13, 14, 15),
                             sample_steps=sample_steps)
    scratch = [pltpu.VMEM((tn, kdim), _BF16)]
    return pl.pallas_call(
        kern,
        grid=(nj, ni),
        in_specs=in_specs,
        out_specs=out_specs,
        out_shape=out_shape,
        scratch_shapes=scratch,
        compiler_params=pltpu.CompilerParams(
            dimension_semantics=("arbitrary", "arbitrary"),
            vmem_limit_bytes=_vmem_limit(in_specs, operands, out_specs, out_shape, scratch)),
        name="proj_in_mix" if sample is not None else "proj_in",
    )(*operands)


def _h_cols(d_model, g):
    wq, wv = d_model // 8, d_model // 4
    d = d_model
    offs = dict(qa=(0, wq), ka=(d // 2, wq), va=(d, wv), za=(2 * d, wv),
                qb=(3 * d, wq), kb=(3 * d + d // 2, wq), vb=(4 * d, wv), zb=(5 * d, wv),
                ga=(6 * d, wv), gb=(7 * d, wv))
    return {n: slice(o + g * w, o + (g + 1) * w) for n, (o, w) in offs.items()}


def _mixer_prompt_kernel(h_ref, la_ref, gnw_ref, rnw_ref, rnb_ref,
                         cos_ref, sin_ref, lg_ref, wout_ref, m_ref, sg_ref, sr_ref, wout_bf_ref,
                         qdec_ref, kdec_ref, dmat_ref):
    wout_bf_ref[...] = wout_ref[0].astype(_BF16)
    chunk = h_ref.shape[0]
    d_model = m_ref.shape[1]
    dk_a, dv_a = sg_ref.shape[3:]
    dk_b, dv_b = sr_ref.shape[3:]

    @pl.when(pl.program_id(1) == 0)
    def _():
        sg_ref[...] = jnp.zeros(sg_ref.shape, _F32)
        sr_ref[...] = jnp.zeros(sr_ref.shape, _F32)

    @pl.when((pl.program_id(0) == 0) & (pl.program_id(1) == 0))
    def _():
        tpos = (_iota((chunk, 2 * dk_b), 0) + 1).astype(_F32)
        tdiff = _iota((chunk, chunk), 0) - _iota((chunk, chunk), 1)
        for g in range(HEAD_GROUPS):
            lg = lg_ref[g]
            qdec_ref[g] = jnp.exp(lg * tpos)
            kdec_ref[g] = jnp.exp(lg * (chunk - tpos))
            for hh in range(2):
                lg1 = lg[:, hh * dk_b:hh * dk_b + 1]
                dmat_ref[2 * g + hh] = jnp.exp(
                    jnp.where(tdiff >= 0, lg1 * tdiff.astype(_F32), -jnp.inf))

    rows = _iota((chunk, 1), 0)
    cos = jnp.concatenate([cos_ref[...]] * 2, axis=1)
    sin = jnp.concatenate([sin_ref[...]] * 2, axis=1)
    n_sub = chunk // GLA_SUB
    groups = range(HEAD_GROUPS)
    cols = [_h_cols(d_model, g) for g in groups]
    ksl = [slice(hh * dk_b, (hh + 1) * dk_b) for hh in range(2)]
    vsl = [slice(hh * dv_b, (hh + 1) * dv_b) for hh in range(2)]


    gla, ret = [], []
    for g in groups:
        c = cols[g]
        b = _cumsum_rows_mxu(la_ref[:, g * dk_a:(g + 1) * dk_a]) * LOG2_E
        q = h_ref[:, c["qa"]] * (dk_a ** -0.5)
        k = h_ref[:, c["ka"]]
        v = h_ref[:, c["va"]].astype(_BF16)
        s_gla = sg_ref[0, 0, g]
        gla.append(dict(b=b, q=q, k=k, v=v, s=s_gla, o=_dot(q * jnp.exp2(b), s_gla)))

        qb = _rope(h_ref[:, c["qb"]], cos, sin)
        kb = _rope(h_ref[:, c["kb"]], cos, sin) * (dk_b ** -0.5)
        vb = h_ref[:, c["vb"]].astype(_BF16)
        lg = lg_ref[g]
        q_dec = qb * qdec_ref[g]
        s_ret = [sr_ref[0, 0, 2 * g + hh] for hh in range(2)]
        ret.append(dict(
            vb=vb, lg=lg, s=s_ret, k_dec=kb * kdec_ref[g],
            o=[_dot(q_dec[:, ksl[hh]], s_ret[hh]) for hh in range(2)],
            qk=[_dot_nt(qb[:, ksl[hh]], kb[:, ksl[hh]]) for hh in range(2)]))

    for g in groups:
        q, k, b = gla[g]["q"], gla[g]["k"], gla[g]["b"]
        a_off = [jnp.zeros((GLA_SUB, chunk), _F32)]
        for i in range(1, n_sub):
            lo = i * GLA_SUB
            r = b[lo - 1:lo, :]
            k_pre = k * jnp.exp2(jnp.where(rows < lo, r - b, -jnp.inf))
            a_off.append(_dot_nt(q[lo:lo + GLA_SUB] * jnp.exp2(b[lo:lo + GLA_SUB] - r), k_pre))
        gla[g]["a"] = a_off

    for g in groups:
        q, k, b, v = gla[g]["q"], gla[g]["k"], gla[g]["b"], gla[g]["v"]
        intra = []
        for i in range(n_sub):
            lo = i * GLA_SUB
            sub = slice(lo, lo + GLA_SUB)
            a = _gla_pair_scores(gla[g]["a"][i], q[sub], k[sub], b[sub], lo, lambda s: (s, GLA_SUB))
            intra.append(_dot(a, v))
        gla[g]["o"] = gla[g]["o"] + jnp.concatenate(intra, axis=0)
        for hh in range(2):
            scores = ret[g]["qk"][hh] * dmat_ref[2 * g + hh]
            ret[g]["o"][hh] = ret[g]["o"][hh] + _dot(scores, ret[g]["vb"][:, vsl[hh]])

    for g in groups:
        k, b, v = gla[g]["k"], gla[g]["b"], gla[g]["v"]
        b_last = b[chunk - 1:chunk, :]
        decay_col = _rows_to_cols([jnp.exp2(b_last)])[:, 0:1]
        sg_ref[0, 0, g] = decay_col * gla[g]["s"] + _dot_tn(k * jnp.exp2(b_last - b), v)
        for hh in range(2):
            lg1 = ret[g]["lg"][:, hh * dk_b:hh * dk_b + 1]
            sr_ref[0, 0, 2 * g + hh] = (jnp.exp(lg1 * chunk) * ret[g]["s"][hh]
                                        + _dot_tn(ret[g]["k_dec"][:, ksl[hh]], ret[g]["vb"][:, vsl[hh]]))

    for g in groups:
        c = cols[g]
        gv = slice(g * dv_a, (g + 1) * dv_a)
        o_a = _rms_heads(gla[g]["o"], gnw_ref[...]) * h_ref[:, c["za"]]
        o_b = []
        for hh in range(2):
            nsl = slice(g * dv_a + hh * dv_b, g * dv_a + (hh + 1) * dv_b)
            o_b.append(_group_norm(ret[g]["o"][hh], rnw_ref[:, nsl], rnb_ref[:, nsl]))
        o_b = jnp.concatenate(o_b, axis=1) * h_ref[:, c["zb"]]
        merged = h_ref[:, c["ga"]] * o_a + h_ref[:, c["gb"]] * o_b
        m_ref[:, gv] = merged.astype(m_ref.dtype)


def _mixer_prompt(h, la, params, rope, lg, w_out, layer, batch, seq, d_model):
    wq, wv = d_model // 8, d_model // 4
    dk_a, dv_a = wq, wv
    dk_b, dv_b = wq // 2, wv // 2
    chunk = PROMPT_CHUNK
    nc = seq // chunk
    e_rows = w_out.shape[1]
    bf16_rows = 2 * SUBLANES
    assert e_rows % (batch * nc) == 0 and (e_rows // (batch * nc)) % bf16_rows == 0
    w_rows = e_rows // (batch * nc)
    whole = lambda a: pl.BlockSpec(a.shape, lambda b, c: (0,) * a.ndim)
    in_specs = [pl.BlockSpec((chunk, h.shape[1]), lambda b, c: (b * nc + c, 0)),
                pl.BlockSpec((chunk, la.shape[1]), lambda b, c: (b * nc + c, 0))]
    in_specs += [whole(p) for p in params]
    in_specs += [pl.BlockSpec((chunk, dk_b), lambda b, c: (c, 0)),
                 pl.BlockSpec((chunk, dk_b), lambda b, c: (c, 0)),
                 whole(lg),
                 pl.BlockSpec((1, w_rows, w_out.shape[2]), lambda b, c: (layer, b * nc + c, 0))]
    out_specs = [
        pl.BlockSpec((chunk, d_model), lambda b, c: (b * nc + c, 0)),
        pl.BlockSpec((1, 1, GLA_HEADS, dk_a, dv_a), lambda b, c: (0, b, 0, 0, 0)),
        pl.BlockSpec((1, 1, RET_HEADS, dk_b, dv_b), lambda b, c: (0, b, 0, 0, 0)),
        pl.BlockSpec((w_rows, w_out.shape[2]), lambda b, c: (b * nc + c, 0)),
    ]
    out_shape = [
        jax.ShapeDtypeStruct((batch * seq, d_model), _BF16),
        jax.ShapeDtypeStruct((1, batch, GLA_HEADS, dk_a, dv_a), _F32),
        jax.ShapeDtypeStruct((1, batch, RET_HEADS, dk_b, dv_b), _F32),
        jax.ShapeDtypeStruct(w_out.shape[1:], _BF16),
    ]
    operands = [h, la, *params, *rope, lg, w_out]
    scratch = [pltpu.VMEM((HEAD_GROUPS, chunk, wq), _F32),
               pltpu.VMEM((HEAD_GROUPS, chunk, wq), _F32),
               pltpu.VMEM((RET_HEADS, chunk, chunk), _F32)]
    return pl.pallas_call(
        _mixer_prompt_kernel,
        grid=(batch, nc),
        in_specs=in_specs,
        out_specs=out_specs,
        out_shape=out_shape,
        scratch_shapes=scratch,
        compiler_params=pltpu.CompilerParams(
            dimension_semantics=("arbitrary", "arbitrary"),
            vmem_limit_bytes=_vmem_limit(in_specs, operands, out_specs, out_shape, scratch)),
        name="mixer_prompt",
    )(*operands)


def _mixer_sample_unit(qa_ref, ka_ref, va_ref, za_ref, qb_ref, kb_ref, vb_ref, zb_ref,
                       ga_ref, gb_ref, la_ref, gnw_ref, rnw_ref, rnb_ref,
                       cos_ref, sin_ref, lg_ref, sg0_ref, sr0_ref, m_ref, sg_ref, sr_ref,
                       *, steps):
    rows_n, dk_a = qa_ref.shape
    nseq = rows_n // steps
    dk_b = qb_ref.shape[1] // 2
    dv_b = vb_ref.shape[1] // 2
    rows = _iota((rows_n, 1), 0)
    rows_p = _iota((LANES, 1), 0)
    pad = LANES - rows_n
    ksl = [slice(hh * dk_b, (hh + 1) * dk_b) for hh in range(2)]
    vsl = [slice(hh * dv_b, (hh + 1) * dv_b) for hh in range(2)]

    def pad_rows(x):
        return jnp.concatenate([x, jnp.zeros((pad, x.shape[1]), x.dtype)], axis=0)

    def seq_rows(x, n, r=rows):
        return jnp.where((r >= n * steps) & (r < (n + 1) * steps), x, 0.0)

    b = _seg_cumsum(la_ref[...], steps) * LOG2_E
    q = qa_ref[...] * (dk_a ** -0.5)
    k = ka_ref[...]
    v = pad_rows(va_ref[...]).astype(_BF16)
    q_dec = q * jnp.exp2(b)
    s_gla = [sg0_ref[0, n, 0] for n in range(nseq)]
    o_a = _dot(seq_rows(q_dec, 0), s_gla[0])
    for n in range(1, nseq):
        o_a = o_a + _dot(seq_rows(q_dec, n), s_gla[n])

    cos = jnp.concatenate([cos_ref[...]] * 2, axis=1)
    sin = jnp.concatenate([sin_ref[...]] * 2, axis=1)
    qb = _rope(qb_ref[...], cos, sin)
    kb = _rope(kb_ref[...], cos, sin) * (dk_b ** -0.5)
    vb = pad_rows(vb_ref[...]).astype(_BF16)
    lg = lg_ref[0]
    lg1 = [lg[:, hh * dk_b:hh * dk_b + 1] for hh in range(2)]
    tpos = ((_iota(qb.shape, 0) & (steps - 1)) + 1).astype(_F32)
    qb_dec = qb * jnp.exp(lg * tpos)
    kb_p = pad_rows(kb)
    s_ret = [[sr0_ref[0, n, hh] for n in range(nseq)] for hh in range(2)]
    o_r, qk = [], []
    for hh in range(2):
        o_h = _dot(seq_rows(qb_dec[:, ksl[hh]], 0), s_ret[hh][0])
        for n in range(1, nseq):
            o_h = o_h + _dot(seq_rows(qb_dec[:, ksl[hh]], n), s_ret[hh][n])
        o_r.append(o_h)
        qk.append(_dot_nt(qb[:, ksl[hh]], kb_p[:, ksl[hh]]))

    a = jnp.zeros((rows_n, LANES), _F32)
    a = _gla_pair_scores(a, q, k, b, 0, lambda s: (s, (s // steps + 1) * steps))
    o_a = o_a + _dot(a, v)
    r_i = _iota((rows_n, LANES), 0)
    c_i = _iota((rows_n, LANES), 1)
    same_seq = (r_i >> (steps.bit_length() - 1)) == (c_i >> (steps.bit_length() - 1))
    tdiff = r_i - c_i
    for hh in range(2):
        decay = jnp.exp(jnp.where((tdiff >= 0) & same_seq, lg1[hh] * tdiff.astype(_F32), -jnp.inf))
        o_r[hh] = o_r[hh] + _dot(qk[hh] * decay, vb[:, vsl[hh]])

    def finish():
        b_last = [b[(n + 1) * steps - 1:(n + 1) * steps, :] for n in range(nseq)]
        b_end = jnp.zeros_like(b)
        for n in range(nseq):
            b_end = b_end + seq_rows(jnp.broadcast_to(b_last[n], b.shape), n)
        k_dec = k * jnp.exp2(b_end - b)
        decay_cols = _rows_to_cols([jnp.exp2(r) for r in b_last])
        for n in range(nseq):
            sg_ref[0, n, 0] = (decay_cols[:, n:n + 1] * s_gla[n]
                               + _dot_tn(seq_rows(k_dec, n), v[:rows_n]))
        kb_dec = kb * jnp.exp(lg * (steps - tpos))
        for hh in range(2):
            for n in range(nseq):
                sr_ref[0, n, hh] = (jnp.exp(lg1[hh] * steps) * s_ret[hh][n]
                                    + _dot_tn(seq_rows(kb_dec[:, ksl[hh]], n), vb[:rows_n, vsl[hh]]))
        o_an = _rms_heads(o_a, gnw_ref[...]) * za_ref[...]
        o_bn = jnp.concatenate([_group_norm(o_r[hh], rnw_ref[:, vsl[hh]], rnb_ref[:, vsl[hh]])
                                for hh in range(2)], axis=1) * zb_ref[...]
        merged = ga_ref[...] * o_an + gb_ref[...] * o_bn
        m_ref[...] = merged.astype(m_ref.dtype)

    return finish


def _sample_units(h, la, params, rope, lg, state_gla, state_ret, layer, nseq_total, steps, d_model):
    wq, wv = d_model // 8, d_model // 4
    dk_a, dv_a = wq, wv
    dk_b, dv_b = wq // 2, wv // 2
    nseq = SAMPLE_SEQS
    rows_n = nseq * steps
    assert steps & (steps - 1) == 0 and nseq_total % nseq == 0
    d = d_model

    def specs(unit_of):
        def at(fn):
            return lambda *ids: fn(*unit_of(*ids))

        def h_spec(width, col_off):
            base = col_off // width
            return pl.BlockSpec((rows_n, width), at(lambda p, g: (p, base + g)))
        in_specs = [h_spec(wq, 0), h_spec(wq, d // 2), h_spec(wv, d), h_spec(wv, 2 * d),
                    h_spec(wq, 3 * d), h_spec(wq, 3 * d + d // 2), h_spec(wv, 4 * d),
                    h_spec(wv, 5 * d), h_spec(wv, 6 * d), h_spec(wv, 7 * d)]
        in_specs += [
            pl.BlockSpec((rows_n, wq), at(lambda p, g: (p, g))),
            pl.BlockSpec((1, wv), at(lambda p, g: (0, 0))),
            pl.BlockSpec((1, wv), at(lambda p, g: (0, g))),
            pl.BlockSpec((1, wv), at(lambda p, g: (0, g))),
            pl.BlockSpec((rows_n, dk_b), at(lambda p, g: (0, 0))),
            pl.BlockSpec((rows_n, dk_b), at(lambda p, g: (0, 0))),
            pl.BlockSpec((1, 1, wq), at(lambda p, g: (g, 0, 0))),
            pl.BlockSpec((1, nseq, 1, dk_a, dv_a), at(lambda p, g: (layer, p, g, 0, 0))),
            pl.BlockSpec((1, nseq, 2, dk_b, dv_b), at(lambda p, g: (layer, p, g, 0, 0))),
        ]
        out_specs = [
            pl.BlockSpec((rows_n, wv), at(lambda p, g: (p, g))),
            pl.BlockSpec((1, nseq, 1, dk_a, dv_a), at(lambda p, g: (0, p, g, 0, 0))),
            pl.BlockSpec((1, nseq, 2, dk_b, dv_b), at(lambda p, g: (0, p, g, 0, 0))),
        ]
        out_shape = [
            jax.ShapeDtypeStruct((nseq_total * steps, d_model), _BF16),
            jax.ShapeDtypeStruct((1, nseq_total, GLA_HEADS, dk_a, dv_a), _F32),
            jax.ShapeDtypeStruct((1, nseq_total, RET_HEADS, dk_b, dv_b), _F32),
        ]
        return in_specs, out_specs, out_shape

    operands = [h] * 10 + [la, *params, *rope, lg, state_gla, state_ret]
    n_units = (nseq_total // nseq) * HEAD_GROUPS
    return operands, steps, n_units, specs


PROJ_OUT_SLABS = 4

def _out_proj_ln(m, x, w, ln_w, ln_b, alpha):
    r = alpha * x + jnp.dot(m, w, preferred_element_type=_F32)
    mu = jnp.mean(r, axis=-1, keepdims=True)
    d = r - mu
    var = jnp.mean(d * d, axis=-1, keepdims=True)
    return d * lax.rsqrt(var + LN_EPS) * ln_w + ln_b


def _proj_out_kernel(m_ref, x_ref, w_ref, lnw_ref, lnb_ref, y_ref, *, alpha):
    tm = m_ref.shape[0]
    slab = min(tm, max(tm // PROJ_OUT_SLABS, LANES))
    for r in range(0, tm, slab):
        if x_ref.ndim == 3:
            steps = x_ref.shape[1]
            seqs = slice(r // steps, (r + slab) // steps)
            x = x_ref[seqs].reshape(slab, x_ref.shape[2])
        else:
            x = x_ref[r:r + slab, :]
        y = _out_proj_ln(m_ref[r:r + slab, :], x, w_ref[...], lnw_ref[...], lnb_ref[...], alpha)
        if x_ref.ndim == 3:
            y_ref[seqs] = y.reshape(slab // steps, steps, y.shape[1])
        else:
            y_ref[r:r + slab, :] = y


def _proj_out(merged, x, w_out, ln_w, ln_b, alpha, tm):
    d = x.shape[-1]
    m = x.size // d
    e = merged.shape[1]
    if x.ndim == 3:
        xy_spec = pl.BlockSpec((tm // x.shape[1], x.shape[1], d), lambda i: (i, 0, 0))
    else:
        xy_spec = pl.BlockSpec((tm, d), lambda i: (i, 0))
    operands = [merged, x, w_out, ln_w, ln_b]
    in_specs = [
        pl.BlockSpec((tm, e), lambda i: (i, 0)),
        xy_spec,
        pl.BlockSpec((e, d), lambda i: (0, 0)),
        pl.BlockSpec((1, d), lambda i: (0, 0)),
        pl.BlockSpec((1, d), lambda i: (0, 0)),
    ]
    out_shape = jax.ShapeDtypeStruct(x.shape, _F32)
    return pl.pallas_call(
        functools.partial(_proj_out_kernel, alpha=alpha),
        grid=(m // tm,),
        in_specs=in_specs,
        out_specs=xy_spec,
        out_shape=out_shape,
        compiler_params=pltpu.CompilerParams(
            dimension_semantics=("arbitrary",),
            vmem_limit_bytes=_vmem_limit(in_specs, operands, [xy_spec], [out_shape])),
        name="proj_out",
    )(*operands)


def _rope_tables(pos, dk):
    inv = 1.0 / (ROPE_BASE ** jnp.linspace(0.0, 1.0, dk // 2, dtype=_F32))
    inv_lanes = jnp.repeat(inv, 2)
    sign = jnp.tile(jnp.asarray([-1.0, 1.0], _F32), dk // 2)
    ang = pos.astype(_F32)[:, None] * inv_lanes[None, :]
    return jnp.cos(ang), jnp.sin(ang) * sign[None, :]


def _pick_tile(n, pref):
    t = min(n, pref)
    while n % t:
        t //= 2
    return t


def kernel(x_prompt, x_sample, state_gla, state_ret, w_in, w_lr, b_lr, gla_norm_w,
           ret_norm_w, ret_norm_b, w_out, ln_w, ln_b):
    depth, d_model, _ = w_in.shape
    batch, seq, _ = x_prompt.shape
    dec_batch, dec_seq, _ = x_sample.shape
    rank = w_lr.shape[1]
    dk_b = d_model // 2 // RET_HEADS
    assert state_gla.shape[2] == GLA_HEADS and state_ret.shape[2] == RET_HEADS
    assert rank <= LANES and seq % PROMPT_CHUNK == 0
    alpha = (2.0 * depth) ** 0.25

    lg_heads = jnp.log(1.0 - 2.0 ** (-5.0 - jnp.arange(RET_HEADS, dtype=_F32)))
    lg = jnp.repeat(lg_heads, dk_b).reshape(HEAD_GROUPS, 1, 2 * dk_b)
    rope_p = _rope_tables(jnp.arange(seq, dtype=jnp.int32), dk_b)
    pos_s = PAST_LEN + jnp.arange(dec_seq, dtype=jnp.int32)
    rope_s = _rope_tables(jnp.tile(pos_s, SAMPLE_SEQS), dk_b)

    hp = x_prompt.reshape(batch * seq, d_model)
    hs = x_sample
    n_rows_s = dec_batch * dec_seq
    w_t = jnp.swapaxes(w_in, 1, 2)
    gla_p, ret_p, gla_s, ret_s = [], [], [], []
    for l in range(depth):
        lr_lo = 3 * d_model
        w_lr_p = jnp.pad(w_lr[l], ((0, LANES - rank), (0, 0)))
        params = (gla_norm_w[l][None, :], ret_norm_w[l][None, :], ret_norm_b[l][None, :])
        lnw, lnb = ln_w[l][None, :], ln_b[l][None, :]

        xp_b, la_p = _prep_x(hp, w_t, l, lr_lo, rank, w_lr_p, b_lr[l][None, :],
                             _pick_tile(hp.shape[0], 1024))
        xs_b, la_s = _prep_x(hs, w_t, l, lr_lo, rank, w_lr_p, b_lr[l][None, :],
                             _pick_tile(n_rows_s, 512))
        (h_s,) = _proj_in(xs_b, w_t, l, rank, _pick_tile(xs_b.shape[0], 1024))
        units = _sample_units(h_s, la_s, params, rope_s, lg, state_gla, state_ret, l,
                              dec_batch, dec_seq, d_model)
        h_p, merged_s, sg_s, sr_s = _proj_in(xp_b, w_t, l, rank, _pick_tile(xp_b.shape[0], 1024),
                                             sample=units)

        merged, sg, sr, w_o = _mixer_prompt(h_p, la_p, params, rope_p, lg, w_out, l,
                                            batch, seq, d_model)
        hp = _proj_out(merged, hp, w_o, lnw, lnb, alpha, _pick_tile(hp.shape[0], 512))
        gla_p.append(sg)
        ret_p.append(sr)

        merged, sg, sr = merged_s, sg_s, sr_s
        hs = _proj_out(merged, hs, w_o, lnw, lnb, alpha, _pick_tile(n_rows_s, 256))
        gla_s.append(sg)
        ret_s.append(sr)

    cat = lambda parts: parts[0] if len(parts) == 1 else jnp.concatenate(parts, axis=0)
    return (hp.reshape(batch, seq, d_model), hs,
            cat(gla_p), cat(ret_p), cat(gla_s), cat(ret_s))
```

```python
import functools

import jax
import jax.numpy as jnp
from jax import lax
from jax.experimental import pallas as pl
from jax.experimental.pallas import tpu as pltpu

GLA_HEADS = 4
RET_HEADS = 8
HEAD_GROUPS = 4
GLA_TAU = 16.0
LOG2_E = 1.4426950408889634
ROPE_BASE = 10000.0
LN_EPS = 1e-5
HEAD_NORM_EPS = 1e-6
PAST_LEN = 16384

LANES = 128
SUBLANES = 8
VMEM_LIMIT_BYTES = 56 * 1024 * 1024
VMEM_TEMP_FACTOR = 1.15

PROMPT_CHUNK = 128
GLA_SUB = 16
SAMPLE_SEQS = 4

_F32 = jnp.float32
_BF16 = jnp.bfloat16
_NT = (((1,), (1,)), ((), ()))
_TN = (((0,), (0,)), ((), ()))


def _vmem_limit(in_specs, operands, out_specs, out_shape, scratch=()):
    def nbytes(shape, dtype):
        n = jnp.dtype(dtype).itemsize
        for s in shape:
            n *= s
        return n
    total = sum(2 * nbytes(spec.block_shape, op.dtype) for spec, op in zip(in_specs, operands))
    total += sum(2 * nbytes(spec.block_shape, o.dtype) for spec, o in zip(out_specs, out_shape))
    total += sum(nbytes(s.shape, s.dtype) for s in scratch)
    mib = 1024 * 1024
    return min(VMEM_LIMIT_BYTES, -(-int(total * VMEM_TEMP_FACTOR) // mib) * mib)


def _dot(a, b):
    return jnp.dot(a.astype(_BF16), b.astype(_BF16), preferred_element_type=_F32)


def _dot_nt(a, b):
    return lax.dot_general(a.astype(_BF16), b.astype(_BF16), _NT, preferred_element_type=_F32)


def _dot_tn(a, b):
    return lax.dot_general(a.astype(_BF16), b.astype(_BF16), _TN, preferred_element_type=_F32)


def _log_sigmoid(x):
    return jnp.minimum(x, 0.0) - jnp.log(1.0 + jnp.exp(-jnp.abs(x)))


def _iota(shape, axis):
    return lax.broadcasted_iota(jnp.int32, shape, axis)


def _seg_cumsum(x, seg):
    pos = _iota(x.shape, 0) & (seg - 1)
    s = 1
    while s < seg:
        x = x + jnp.where(pos >= s, pltpu.roll(x, s, axis=0), 0.0)
        s *= 2
    return x


def _cumsum_rows_mxu(x):
    n = x.shape[0]
    hi = x.astype(_BF16)
    r1 = x - hi.astype(_F32)
    mid = r1.astype(_BF16)
    lo = (r1 - mid.astype(_F32)).astype(_BF16)
    tri3 = (_iota((n, 3 * n), 0) >= (_iota((n, 3 * n), 1) & (n - 1))).astype(_BF16)
    return jnp.dot(tri3, jnp.concatenate([hi, mid, lo], axis=0), preferred_element_type=_F32)


def _rows_to_cols(rows):
    n = rows[0].shape[1]
    rid = _iota((LANES, n), 0)
    stack = jnp.zeros((LANES, n), _F32)
    for i, r in enumerate(rows):
        stack = jnp.where(rid == i, r, stack)
    return stack.T


def _rope(x, cos, sin_signed):
    w = x.shape[1]
    even = (_iota(x.shape, 1) & 1) == 0
    swapped = jnp.where(even, pltpu.roll(x, w - 1, axis=1), pltpu.roll(x, 1, axis=1))
    return x * cos + swapped * sin_signed


def _gla_pair_scores(a, q, k, b2, lane0, row_lo_hi):
    n = q.shape[0]
    lanes = _iota((SUBLANES, a.shape[1]), 1)
    tiles = [a[r:r + SUBLANES] for r in range(0, n, SUBLANES)]
    for s in range(n):
        lo, hi = row_lo_hi(s)
        for j, r0 in enumerate(range(0, n, SUBLANES)):
            if r0 + SUBLANES <= lo or r0 >= hi:
                continue
            e = jnp.exp2(b2[r0:r0 + SUBLANES] - b2[s:s + 1, :])
            col = jnp.sum(q[r0:r0 + SUBLANES] * k[s:s + 1, :] * e, axis=-1, keepdims=True)
            take = lanes == lane0 + s
            if lo > r0 or hi < r0 + SUBLANES:
                rows = _iota((SUBLANES, 1), 0) + r0
                take = take & (rows >= lo) & (rows < hi)
            tiles[j] = jnp.where(take, col, tiles[j])
    return jnp.concatenate(tiles, axis=0)


def _rms_heads(o, w):
    return o * lax.rsqrt(jnp.mean(o * o, axis=-1, keepdims=True) + HEAD_NORM_EPS) * w


def _group_norm(o, w, b):
    mu = jnp.mean(o, axis=-1, keepdims=True)
    d = o - mu
    var = jnp.mean(d * d, axis=-1, keepdims=True)
    return d * lax.rsqrt(var + HEAD_NORM_EPS) * w + b


def _prep_x_kernel(x_ref, wcol_ref, wlr_ref, blr_ref, xb_ref, la_ref, *, rank):
    x = x_ref[...]
    if x.ndim == 3:
        x = x.reshape(x.shape[0] * x.shape[1], x.shape[2])
    xb = x.astype(_BF16)
    xb_ref[...] = xb
    wrow = wcol_ref[0]
    w_lra = jnp.concatenate([wrow, jnp.zeros((LANES - rank, wrow.shape[1]), _F32)], axis=0)
    lr = _dot_nt(xb, w_lra)
    w = wlr_ref[...]
    w_hi = w.astype(_BF16).astype(_F32)
    w_lo = w - w_hi
    lr_hi = lr.astype(_BF16).astype(_F32)
    lr_lo = lr - lr_hi
    lr_cat = lr_hi + pltpu.roll(lr_hi, rank, axis=1) + pltpu.roll(lr_lo, 2 * rank, axis=1)
    w_cat = jnp.concatenate([w_hi[:rank], w_lo[:rank], w_hi[:rank],
                             jnp.zeros((LANES - 3 * rank, w.shape[1]), _F32)], axis=0)
    logit = _dot(lr_cat, w_cat) + blr_ref[...]
    la_ref[...] = _log_sigmoid(logit) * (1.0 / GLA_TAU)


def _prep_x(x, w_t, layer, lr_col, rank, w_lr, b_lr, tm):
    kdim = x.shape[-1]
    m = x.size // kdim
    qk = w_lr.shape[1]
    assert lr_col % rank == 0 and rank % SUBLANES == 0 and 3 * rank <= LANES
    whole = lambda a: pl.BlockSpec(a.shape, lambda i: (0,) * a.ndim)
    if x.ndim == 3:
        x_spec = pl.BlockSpec((tm // x.shape[1], x.shape[1], kdim), lambda i: (i, 0, 0))
    else:
        x_spec = pl.BlockSpec((tm, kdim), lambda i: (i, 0))
    operands = [x, w_t, w_lr, b_lr]
    in_specs = [x_spec,
                pl.BlockSpec((1, rank, kdim), lambda i: (layer, lr_col // rank, 0)),
                whole(w_lr), whole(b_lr)]
    out_specs = [pl.BlockSpec((tm, kdim), lambda i: (i, 0)),
                 pl.BlockSpec((tm, qk), lambda i: (i, 0))]
    out_shape = [jax.ShapeDtypeStruct((m, kdim), _BF16),
                 jax.ShapeDtypeStruct((m, qk), _F32)]
    return pl.pallas_call(
        functools.partial(_prep_x_kernel, rank=rank),
        grid=(m // tm,),
        in_specs=in_specs,
        out_specs=out_specs,
        out_shape=out_shape,
        compiler_params=pltpu.CompilerParams(
            dimension_semantics=("arbitrary",),
            vmem_limit_bytes=_vmem_limit(in_specs, operands, out_specs, out_shape)),
        name="prep_x",
    )(*operands)


N_SAMPLE_IN = 19


def _proj_in_kernel(*refs, n_plain, shift, silu_tiles, sigmoid_tiles, sample_steps):
    x_ref, wa_ref, we_ref = refs[:3]
    wbf_ref = refs[-1]
    if sample_steps:
        unit_in = refs[3:3 + N_SAMPLE_IN]
        h_ref = refs[3 + N_SAMPLE_IN]
        unit_out = refs[4 + N_SAMPLE_IN:7 + N_SAMPLE_IN]
    else:
        h_ref = refs[3]
    j = pl.program_id(0)
    i = pl.program_id(1)
    tn, kdim = wbf_ref.shape

    @pl.when(i == 0)
    def _():
        @pl.when(j < n_plain)
        def _():
            wbf_ref[...] = wa_ref[0].astype(_BF16)

        @pl.when(j >= n_plain)
        def _():
            wbf_ref[0:tn - shift, :] = wa_ref[0, shift:tn, :].astype(_BF16)
            wbf_ref[tn - shift:tn, :] = we_ref[0].astype(_BF16)

    is_silu = functools.reduce(jnp.logical_or, [j == t for t in silu_tiles])
    is_sigmoid = functools.reduce(jnp.logical_or, [j == t for t in sigmoid_tiles])

    def slab(cols, activated):
        acc = lax.dot_general(x_ref[...], wbf_ref[cols, :], _NT, preferred_element_type=_F32)
        if activated:
            s = 0.5 * jnp.tanh(0.5 * acc) + 0.5
            acc = jnp.where(is_sigmoid, s, acc * s)
        h_ref[:, cols] = acc

    def tile(activated):
        finish_unit = None
        if sample_steps:
            finish_unit = _mixer_sample_unit(*unit_in, *unit_out, steps=sample_steps)
        slab(slice(0, tn // 4), activated)
        if finish_unit is not None:
            finish_unit()
        slab(slice(tn // 4, tn), activated)

    @pl.when(is_silu | is_sigmoid)
    def _():
        tile(True)

    @pl.when(jnp.logical_not(is_silu | is_sigmoid))
    def _():
        tile(False)


PROJ_TILES = 16


def _proj_in(x, w_t, layer, rank, tm, sample=None):
    m, kdim = x.shape
    d_model = kdim
    tn = d_model // 2
    nj = PROJ_TILES
    n_plain = 6
    bf16_rows = 2 * SUBLANES
    assert rank % bf16_rows == 0 and tn % rank == 0 and m % tm == 0
    ni = m // tm

    def row_block(j, i):
        return jnp.where(j % 2 == 0, i, ni - 1 - i)
    in_specs = [
        pl.BlockSpec((tm, kdim), lambda j, i: (row_block(j, i), 0)),
        pl.BlockSpec((1, tn, kdim), lambda j, i: (layer, j, 0)),
        pl.BlockSpec((1, rank, kdim), lambda j, i: (layer, (j + 1) * (tn // rank), 0)),
    ]
    out_specs = [pl.BlockSpec((tm, tn), lambda j, i: (row_block(j, i), j))]
    out_shape = [jax.ShapeDtypeStruct((m, nj * tn), _F32)]
    operands = [x, w_t, w_t]
    sample_steps = 0
    if sample is not None:
        unit_operands, sample_steps, n_units, unit_specs = sample
        assert n_units <= nj * ni

        def unit_of(j, i):
            u = jnp.minimum(j * ni + i, n_units - 1)
            return u % (n_units // HEAD_GROUPS), u // (n_units // HEAD_GROUPS)
        unit_in, unit_out, unit_shape = unit_specs(unit_of)
        assert len(unit_in) == N_SAMPLE_IN
        in_specs += unit_in
        out_specs += unit_out
        out_shape += unit_shape
        operands += unit_operands
    kern = functools.partial(_proj_in_kernel, n_plain=n_plain, shift=rank,
                             silu_tiles=(4, 5, 10, 11), sigmoid_tiles=(12, 13, 14, 15),
                             sample_steps=sample_steps)
    scratch = [pltpu.VMEM((tn, kdim), _BF16)]
    return pl.pallas_call(
        kern,
        grid=(nj, ni),
        in_specs=in_specs,
        out_specs=out_specs,
        out_shape=out_shape,
        scratch_shapes=scratch,
        compiler_params=pltpu.CompilerParams(
            dimension_semantics=("arbitrary", "arbitrary"),
            vmem_limit_bytes=_vmem_limit(in_specs, operands, out_specs, out_shape, scratch)),
        name="proj_in_mix" if sample is not None else "proj_in",
    )(*operands)


def _h_cols(d_model, g):
    wq, wv = d_model // 8, d_model // 4
    d = d_model
    offs = dict(qa=(0, wq), ka=(d // 2, wq), va=(d, wv), za=(2 * d, wv),
                qb=(3 * d, wq), kb=(3 * d + d // 2, wq), vb=(4 * d, wv), zb=(5 * d, wv),
                ga=(6 * d, wv), gb=(7 * d, wv))
    return {n: slice(o + g * w, o + (g + 1) * w) for n, (o, w) in offs.items()}


def _mixer_prompt_kernel(h_ref, la_ref, gnw_ref, rnw_ref, rnb_ref,
                         cos_ref, sin_ref, lg_ref, wout_ref, m_ref, sg_ref, sr_ref, wout_bf_ref,
                         qdec_ref, kdec_ref, dmat_ref):
    wout_bf_ref[...] = wout_ref[0].astype(_BF16)
    chunk = h_ref.shape[0]
    d_model = m_ref.shape[1]
    dk_a, dv_a = sg_ref.shape[3:]
    dk_b, dv_b = sr_ref.shape[3:]

    @pl.when(pl.program_id(1) == 0)
    def _():
        sg_ref[...] = jnp.zeros(sg_ref.shape, _F32)
        sr_ref[...] = jnp.zeros(sr_ref.shape, _F32)

    @pl.when((pl.program_id(0) == 0) & (pl.program_id(1) == 0))
    def _():
        tpos = (_iota((chunk, 2 * dk_b), 0) + 1).astype(_F32)
        tdiff = _iota((chunk, chunk), 0) - _iota((chunk, chunk), 1)
        for g in range(HEAD_GROUPS):
            lg = lg_ref[g]
            qdec_ref[g] = jnp.exp(lg * tpos)
            kdec_ref[g] = jnp.exp(lg * (chunk - tpos))
            for hh in range(2):
                lg1 = lg[:, hh * dk_b:hh * dk_b + 1]
                dmat_ref[2 * g + hh] = jnp.exp(
                    jnp.where(tdiff >= 0, lg1 * tdiff.astype(_F32), -jnp.inf))

    rows = _iota((chunk, 1), 0)
    cos = jnp.concatenate([cos_ref[...]] * 2, axis=1)
    sin = jnp.concatenate([sin_ref[...]] * 2, axis=1)
    n_sub = chunk // GLA_SUB
    groups = range(HEAD_GROUPS)
    cols = [_h_cols(d_model, g) for g in groups]
    ksl = [slice(hh * dk_b, (hh + 1) * dk_b) for hh in range(2)]
    vsl = [slice(hh * dv_b, (hh + 1) * dv_b) for hh in range(2)]


    gla, ret = [], []
    for g in groups:
        c = cols[g]
        b = _cumsum_rows_mxu(la_ref[:, g * dk_a:(g + 1) * dk_a]) * LOG2_E
        q = h_ref[:, c["qa"]] * (dk_a ** -0.5)
        k = h_ref[:, c["ka"]]
        v = h_ref[:, c["va"]].astype(_BF16)
        s_gla = sg_ref[0, 0, g]
        gla.append(dict(b=b, q=q, k=k, v=v, s=s_gla, o=_dot(q * jnp.exp2(b), s_gla)))

        qb = _rope(h_ref[:, c["qb"]], cos, sin)
        kb = _rope(h_ref[:, c["kb"]], cos, sin) * (dk_b ** -0.5)
        vb = h_ref[:, c["vb"]].astype(_BF16)
        lg = lg_ref[g]
        q_dec = qb * qdec_ref[g]
        s_ret = [sr_ref[0, 0, 2 * g + hh] for hh in range(2)]
        ret.append(dict(
            vb=vb, lg=lg, s=s_ret, k_dec=kb * kdec_ref[g],
            o=[_dot(q_dec[:, ksl[hh]], s_ret[hh]) for hh in range(2)],
            qk=[_dot_nt(qb[:, ksl[hh]], kb[:, ksl[hh]]) for hh in range(2)]))

    for g in groups:
        q, k, b = gla[g]["q"], gla[g]["k"], gla[g]["b"]
        a_off = [jnp.zeros((GLA_SUB, chunk), _F32)]
        for i in range(1, n_sub):
            lo = i * GLA_SUB
            r = b[lo - 1:lo, :]
            k_pre = k * jnp.exp2(jnp.where(rows < lo, r - b, -jnp.inf))
            a_off.append(_dot_nt(q[lo:lo + GLA_SUB] * jnp.exp2(b[lo:lo + GLA_SUB] - r), k_pre))
        gla[g]["a"] = a_off

    for g in groups:
        q, k, b, v = gla[g]["q"], gla[g]["k"], gla[g]["b"], gla[g]["v"]
        intra = []
        for i in range(n_sub):
            lo = i * GLA_SUB
            sub = slice(lo, lo + GLA_SUB)
            a = _gla_pair_scores(gla[g]["a"][i], q[sub], k[sub], b[sub], lo, lambda s: (s, GLA_SUB))
            intra.append(_dot(a, v))
        gla[g]["o"] = gla[g]["o"] + jnp.concatenate(intra, axis=0)
        for hh in range(2):
            scores = ret[g]["qk"][hh] * dmat_ref[2 * g + hh]
            ret[g]["o"][hh] = ret[g]["o"][hh] + _dot(scores, ret[g]["vb"][:, vsl[hh]])

    for g in groups:
        k, b, v = gla[g]["k"], gla[g]["b"], gla[g]["v"]
        b_last = b[chunk - 1:chunk, :]
        decay_col = _rows_to_cols([jnp.exp2(b_last)])[:, 0:1]
        sg_ref[0, 0, g] = decay_col * gla[g]["s"] + _dot_tn(k * jnp.exp2(b_last - b), v)
        for hh in range(2):
            lg1 = ret[g]["lg"][:, hh * dk_b:hh * dk_b + 1]
            sr_ref[0, 0, 2 * g + hh] = (jnp.exp(lg1 * chunk) * ret[g]["s"][hh]
                                        + _dot_tn(ret[g]["k_dec"][:, ksl[hh]], ret[g]["vb"][:, vsl[hh]]))

    for g in groups:
        c = cols[g]
        gv = slice(g * dv_a, (g + 1) * dv_a)
        o_a = _rms_heads(gla[g]["o"], gnw_ref[...]) * h_ref[:, c["za"]]
        o_b = []
        for hh in range(2):
            nsl = slice(g * dv_a + hh * dv_b, g * dv_a + (hh + 1) * dv_b)
            o_b.append(_group_norm(ret[g]["o"][hh], rnw_ref[:, nsl], rnb_ref[:, nsl]))
        o_b = jnp.concatenate(o_b, axis=1) * h_ref[:, c["zb"]]
        merged = h_ref[:, c["ga"]] * o_a + h_ref[:, c["gb"]] * o_b
        m_ref[:, gv] = merged.astype(m_ref.dtype)


def _mixer_prompt(h, la, params, rope, lg, w_out, layer, batch, seq, d_model):
    wq, wv = d_model // 8, d_model // 4
    dk_a, dv_a = wq, wv
    dk_b, dv_b = wq // 2, wv // 2
    chunk = PROMPT_CHUNK
    nc = seq // chunk
    e_rows = w_out.shape[1]
    bf16_rows = 2 * SUBLANES
    assert e_rows % (batch * nc) == 0 and (e_rows // (batch * nc)) % bf16_rows == 0
    w_rows = e_rows // (batch * nc)
    whole = lambda a: pl.BlockSpec(a.shape, lambda b, c: (0,) * a.ndim)
    in_specs = [pl.BlockSpec((chunk, h.shape[1]), lambda b, c: (b * nc + c, 0)),
                pl.BlockSpec((chunk, la.shape[1]), lambda b, c: (b * nc + c, 0))]
    in_specs += [whole(p) for p in params]
    in_specs += [pl.BlockSpec((chunk, dk_b), lambda b, c: (c, 0)),
                 pl.BlockSpec((chunk, dk_b), lambda b, c: (c, 0)),
                 whole(lg),
                 pl.BlockSpec((1, w_rows, w_out.shape[2]), lambda b, c: (layer, b * nc + c, 0))]
    out_specs = [
        pl.BlockSpec((chunk, d_model), lambda b, c: (b * nc + c, 0)),
        pl.BlockSpec((1, 1, GLA_HEADS, dk_a, dv_a), lambda b, c: (0, b, 0, 0, 0)),
        pl.BlockSpec((1, 1, RET_HEADS, dk_b, dv_b), lambda b, c: (0, b, 0, 0, 0)),
        pl.BlockSpec((w_rows, w_out.shape[2]), lambda b, c: (b * nc + c, 0)),
    ]
    out_shape = [
        jax.ShapeDtypeStruct((batch * seq, d_model), _BF16),
        jax.ShapeDtypeStruct((1, batch, GLA_HEADS, dk_a, dv_a), _F32),
        jax.ShapeDtypeStruct((1, batch, RET_HEADS, dk_b, dv_b), _F32),
        jax.ShapeDtypeStruct(w_out.shape[1:], _BF16),
    ]
    operands = [h, la, *params, *rope, lg, w_out]
    scratch = [pltpu.VMEM((HEAD_GROUPS, chunk, wq), _F32),
               pltpu.VMEM((HEAD_GROUPS, chunk, wq), _F32),
               pltpu.VMEM((RET_HEADS, chunk, chunk), _F32)]
    return pl.pallas_call(
        _mixer_prompt_kernel,
        grid=(batch, nc),
        in_specs=in_specs,
        out_specs=out_specs,
        out_shape=out_shape,
        scratch_shapes=scratch,
        compiler_params=pltpu.CompilerParams(
            dimension_semantics=("arbitrary", "arbitrary"),
            vmem_limit_bytes=_vmem_limit(in_specs, operands, out_specs, out_shape, scratch)),
        name="mixer_prompt",
    )(*operands)


def _mixer_sample_unit(qa_ref, ka_ref, va_ref, za_ref, qb_ref, kb_ref, vb_ref, zb_ref,
                       ga_ref, gb_ref, la_ref, gnw_ref, rnw_ref, rnb_ref,
                       cos_ref, sin_ref, lg_ref, sg0_ref, sr0_ref, m_ref, sg_ref, sr_ref,
                       *, steps):
    rows_n, dk_a = qa_ref.shape
    nseq = rows_n // steps
    dk_b = qb_ref.shape[1] // 2
    dv_b = vb_ref.shape[1] // 2
    rows = _iota((rows_n, 1), 0)
    rows_p = _iota((LANES, 1), 0)
    pad = LANES - rows_n
    ksl = [slice(hh * dk_b, (hh + 1) * dk_b) for hh in range(2)]
    vsl = [slice(hh * dv_b, (hh + 1) * dv_b) for hh in range(2)]

    def pad_rows(x):
        return jnp.concatenate([x, jnp.zeros((pad, x.shape[1]), x.dtype)], axis=0)

    def seq_rows(x, n, r=rows):
        return jnp.where((r >= n * steps) & (r < (n + 1) * steps), x, 0.0)

    b = _seg_cumsum(la_ref[...], steps) * LOG2_E
    q = qa_ref[...] * (dk_a ** -0.5)
    k = ka_ref[...]
    v = pad_rows(va_ref[...]).astype(_BF16)
    q_dec = q * jnp.exp2(b)
    s_gla = [sg0_ref[0, n, 0] for n in range(nseq)]
    o_a = _dot(seq_rows(q_dec, 0), s_gla[0])
    for n in range(1, nseq):
        o_a = o_a + _dot(seq_rows(q_dec, n), s_gla[n])

    cos = jnp.concatenate([cos_ref[...]] * 2, axis=1)
    sin = jnp.concatenate([sin_ref[...]] * 2, axis=1)
    qb = _rope(qb_ref[...], cos, sin)
    kb = _rope(kb_ref[...], cos, sin) * (dk_b ** -0.5)
    vb = pad_rows(vb_ref[...]).astype(_BF16)
    lg = lg_ref[0]
    lg1 = [lg[:, hh * dk_b:hh * dk_b + 1] for hh in range(2)]
    tpos = ((_iota(qb.shape, 0) & (steps - 1)) + 1).astype(_F32)
    qb_dec = qb * jnp.exp(lg * tpos)
    kb_p = pad_rows(kb)
    s_ret = [[sr0_ref[0, n, hh] for n in range(nseq)] for hh in range(2)]
    o_r, qk = [], []
    for hh in range(2):
        o_h = _dot(seq_rows(qb_dec[:, ksl[hh]], 0), s_ret[hh][0])
        for n in range(1, nseq):
            o_h = o_h + _dot(seq_rows(qb_dec[:, ksl[hh]], n), s_ret[hh][n])
        o_r.append(o_h)
        qk.append(_dot_nt(qb[:, ksl[hh]], kb_p[:, ksl[hh]]))

    a = jnp.zeros((rows_n, LANES), _F32)
    a = _gla_pair_scores(a, q, k, b, 0, lambda s: (s, (s // steps + 1) * steps))
    o_a = o_a + _dot(a, v)
    r_i = _iota((rows_n, LANES), 0)
    c_i = _iota((rows_n, LANES), 1)
    same_seq = (r_i >> (steps.bit_length() - 1)) == (c_i >> (steps.bit_length() - 1))
    tdiff = r_i - c_i
    for hh in range(2):
        decay = jnp.exp(jnp.where((tdiff >= 0) & same_seq, lg1[hh] * tdiff.astype(_F32), -jnp.inf))
        o_r[hh] = o_r[hh] + _dot(qk[hh] * decay, vb[:, vsl[hh]])

    def finish():
        b_last = [b[(n + 1) * steps - 1:(n + 1) * steps, :] for n in range(nseq)]
        b_end = jnp.zeros_like(b)
        for n in range(nseq):
            b_end = b_end + seq_rows(jnp.broadcast_to(b_last[n], b.shape), n)
        k_dec = k * jnp.exp2(b_end - b)
        decay_cols = _rows_to_cols([jnp.exp2(r) for r in b_last])
        for n in range(nseq):
            sg_ref[0, n, 0] = (decay_cols[:, n:n + 1] * s_gla[n]
                               + _dot_tn(seq_rows(k_dec, n), v[:rows_n]))
        kb_dec = kb * jnp.exp(lg * (steps - tpos))
        for hh in range(2):
            for n in range(nseq):
                sr_ref[0, n, hh] = (jnp.exp(lg1[hh] * steps) * s_ret[hh][n]
                                    + _dot_tn(seq_rows(kb_dec[:, ksl[hh]], n), vb[:rows_n, vsl[hh]]))
        o_an = _rms_heads(o_a, gnw_ref[...]) * za_ref[...]
        o_bn = jnp.concatenate([_group_norm(o_r[hh], rnw_ref[:, vsl[hh]], rnb_ref[:, vsl[hh]])
                                for hh in range(2)], axis=1) * zb_ref[...]
        merged = ga_ref[...] * o_an + gb_ref[...] * o_bn
        m_ref[...] = merged.astype(m_ref.dtype)

    return finish


def _sample_units(h, la, params, rope, lg, state_gla, state_ret, layer, nseq_total, steps, d_model):
    wq, wv = d_model // 8, d_model // 4
    dk_a, dv_a = wq, wv
    dk_b, dv_b = wq // 2, wv // 2
    nseq = SAMPLE_SEQS
    rows_n = nseq * steps
    assert steps & (steps - 1) == 0 and nseq_total % nseq == 0
    d = d_model

    def specs(unit_of):
        def at(fn):
            return lambda *ids: fn(*unit_of(*ids))

        def h_spec(width, col_off):
            base = col_off // width
            return pl.BlockSpec((rows_n, width), at(lambda p, g: (p, base + g)))
        in_specs = [h_spec(wq, 0), h_spec(wq, d // 2), h_spec(wv, d), h_spec(wv, 2 * d),
                    h_spec(wq, 3 * d), h_spec(wq, 3 * d + d // 2), h_spec(wv, 4 * d),
                    h_spec(wv, 5 * d), h_spec(wv, 6 * d), h_spec(wv, 7 * d)]
        in_specs += [
            pl.BlockSpec((rows_n, wq), at(lambda p, g: (p, g))),
            pl.BlockSpec((1, wv), at(lambda p, g: (0, 0))),
            pl.BlockSpec((1, wv), at(lambda p, g: (0, g))),
            pl.BlockSpec((1, wv), at(lambda p, g: (0, g))),
            pl.BlockSpec((rows_n, dk_b), at(lambda p, g: (0, 0))),
            pl.BlockSpec((rows_n, dk_b), at(lambda p, g: (0, 0))),
            pl.BlockSpec((1, 1, wq), at(lambda p, g: (g, 0, 0))),
            pl.BlockSpec((1, nseq, 1, dk_a, dv_a), at(lambda p, g: (layer, p, g, 0, 0))),
            pl.BlockSpec((1, nseq, 2, dk_b, dv_b), at(lambda p, g: (layer, p, g, 0, 0))),
        ]
        out_specs = [
            pl.BlockSpec((rows_n, wv), at(lambda p, g: (p, g))),
            pl.BlockSpec((1, nseq, 1, dk_a, dv_a), at(lambda p, g: (0, p, g, 0, 0))),
            pl.BlockSpec((1, nseq, 2, dk_b, dv_b), at(lambda p, g: (0, p, g, 0, 0))),
        ]
        out_shape = [
            jax.ShapeDtypeStruct((nseq_total * steps, d_model), _BF16),
            jax.ShapeDtypeStruct((1, nseq_total, GLA_HEADS, dk_a, dv_a), _F32),
            jax.ShapeDtypeStruct((1, nseq_total, RET_HEADS, dk_b, dv_b), _F32),
        ]
        return in_specs, out_specs, out_shape

    operands = [h] * 10 + [la, *params, *rope, lg, state_gla, state_ret]
    n_units = (nseq_total // nseq) * HEAD_GROUPS
    return operands, steps, n_units, specs


PROJ_OUT_SLABS = 4

def _out_proj_ln(m, x, w, ln_w, ln_b, alpha):
    r = alpha * x + jnp.dot(m, w, preferred_element_type=_F32)
    mu = jnp.mean(r, axis=-1, keepdims=True)
    d = r - mu
    var = jnp.mean(d * d, axis=-1, keepdims=True)
    return d * lax.rsqrt(var + LN_EPS) * ln_w + ln_b


def _proj_out_kernel(m_ref, x_ref, w_ref, lnw_ref, lnb_ref, y_ref, *, alpha):
    tm = m_ref.shape[0]
    slab = min(tm, max(tm // PROJ_OUT_SLABS, LANES))
    for r in range(0, tm, slab):
        if x_ref.ndim == 3:
            steps = x_ref.shape[1]
            seqs = slice(r // steps, (r + slab) // steps)
            x = x_ref[seqs].reshape(slab, x_ref.shape[2])
        else:
            x = x_ref[r:r + slab, :]
        y = _out_proj_ln(m_ref[r:r + slab, :], x, w_ref[...], lnw_ref[...], lnb_ref[...], alpha)
        if x_ref.ndim == 3:
            y_ref[seqs] = y.reshape(slab // steps, steps, y.shape[1])
        else:
            y_ref[r:r + slab, :] = y


def _proj_out(merged, x, w_out, ln_w, ln_b, alpha, tm):
    d = x.shape[-1]
    m = x.size // d
    e = merged.shape[1]
    if x.ndim == 3:
        xy_spec = pl.BlockSpec((tm // x.shape[1], x.shape[1], d), lambda i: (i, 0, 0))
    else:
        xy_spec = pl.BlockSpec((tm, d), lambda i: (i, 0))
    operands = [merged, x, w_out, ln_w, ln_b]
    in_specs = [
        pl.BlockSpec((tm, e), lambda i: (i, 0)),
        xy_spec,
        pl.BlockSpec((e, d), lambda i: (0, 0)),
        pl.BlockSpec((1, d), lambda i: (0, 0)),
        pl.BlockSpec((1, d), lambda i: (0, 0)),
    ]
    out_shape = jax.ShapeDtypeStruct(x.shape, _F32)
    return pl.pallas_call(
        functools.partial(_proj_out_kernel, alpha=alpha),
        grid=(m // tm,),
        in_specs=in_specs,
        out_specs=xy_spec,
        out_shape=out_shape,
        compiler_params=pltpu.CompilerParams(
            dimension_semantics=("arbitrary",),
            vmem_limit_bytes=_vmem_limit(in_specs, operands, [xy_spec], [out_shape])),
        name="proj_out",
    )(*operands)


def _rope_tables(pos, dk):
    inv = 1.0 / (ROPE_BASE ** jnp.linspace(0.0, 1.0, dk // 2, dtype=_F32))
    inv_lanes = jnp.repeat(inv, 2)
    sign = jnp.tile(jnp.asarray([-1.0, 1.0], _F32), dk // 2)
    ang = pos.astype(_F32)[:, None] * inv_lanes[None, :]
    return jnp.cos(ang), jnp.sin(ang) * sign[None, :]


def _pick_tile(n, pref):
    t = min(n, pref)
    while n % t:
        t //= 2
    return t


def kernel(x_prompt, x_sample, state_gla, state_ret, w_in, w_lr, b_lr, gla_norm_w,
           ret_norm_w, ret_norm_b, w_out, ln_w, ln_b):
    depth, d_model, _ = w_in.shape
    batch, seq, _ = x_prompt.shape
    dec_batch, dec_seq, _ = x_sample.shape
    rank = w_lr.shape[1]
    dk_b = d_model // 2 // RET_HEADS
    assert state_gla.shape[2] == GLA_HEADS and state_ret.shape[2] == RET_HEADS
    assert rank <= LANES and seq % PROMPT_CHUNK == 0
    alpha = (2.0 * depth) ** 0.25

    lg_heads = jnp.log(1.0 - 2.0 ** (-5.0 - jnp.arange(RET_HEADS, dtype=_F32)))
    lg = jnp.repeat(lg_heads, dk_b).reshape(HEAD_GROUPS, 1, 2 * dk_b)
    rope_p = _rope_tables(jnp.arange(seq, dtype=jnp.int32), dk_b)
    pos_s = PAST_LEN + jnp.arange(dec_seq, dtype=jnp.int32)
    rope_s = _rope_tables(jnp.tile(pos_s, SAMPLE_SEQS), dk_b)

    hp = x_prompt.reshape(batch * seq, d_model)
    hs = x_sample
    n_rows_s = dec_batch * dec_seq
    w_t = jnp.swapaxes(w_in, 1, 2)
    gla_p, ret_p, gla_s, ret_s = [], [], [], []
    for l in range(depth):
        lr_lo = 3 * d_model
        w_lr_p = jnp.pad(w_lr[l], ((0, LANES - rank), (0, 0)))
        params = (gla_norm_w[l][None, :], ret_norm_w[l][None, :], ret_norm_b[l][None, :])
        lnw, lnb = ln_w[l][None, :], ln_b[l][None, :]

        xp_b, la_p = _prep_x(hp, w_t, l, lr_lo, rank, w_lr_p, b_lr[l][None, :],
                             _pick_tile(hp.shape[0], 1024))
        xs_b, la_s = _prep_x(hs, w_t, l, lr_lo, rank, w_lr_p, b_lr[l][None, :],
                             _pick_tile(n_rows_s, 512))
        (h_s,) = _proj_in(xs_b, w_t, l, rank, _pick_tile(xs_b.shape[0], 1024))
        units = _sample_units(h_s, la_s, params, rope_s, lg, state_gla, state_ret, l,
                              dec_batch, dec_seq, d_model)
        h_p, merged_s, sg_s, sr_s = _proj_in(xp_b, w_t, l, rank, _pick_tile(xp_b.shape[0], 1024),
                                             sample=units)

        merged, sg, sr, w_o = _mixer_prompt(h_p, la_p, params, rope_p, lg, w_out, l,
                                            batch, seq, d_model)
        hp = _proj_out(merged, hp, w_o, lnw, lnb, alpha, _pick_tile(hp.shape[0], 512))
        gla_p.append(sg)
        ret_p.append(sr)

        merged, sg, sr = merged_s, sg_s, sr_s
        hs = _proj_out(merged, hs, w_o, lnw, lnb, alpha, _pick_tile(n_rows_s, 256))
        gla_s.append(sg)
        ret_s.append(sr)

    cat = lambda parts: parts[0] if len(parts) == 1 else jnp.concatenate(parts, axis=0)
    return (hp.reshape(batch, seq, d_model), hs,
            cat(gla_p), cat(ret_p), cat(gla_s), cat(ret_s))
```

```python
import functools

import jax
import jax.numpy as jnp
from jax import lax
from jax.experimental import pallas as pl
from jax.experimental.pallas import tpu as pltpu

GLA_HEADS = 4
RET_HEADS = 8
HEAD_GROUPS = 4
GLA_TAU = 16.0
LOG2_E = 1.4426950408889634
ROPE_BASE = 10000.0
LN_EPS = 1e-5
HEAD_NORM_EPS = 1e-6
PAST_LEN = 16384

LANES = 128
SUBLANES = 8
VMEM_LIMIT_BYTES = 56 * 1024 * 1024
VMEM_TEMP_FACTOR = 1.15

PROMPT_CHUNK = 128
GLA_SUB = 16
SAMPLE_SEQS = 4

_F32 = jnp.float32
_BF16 = jnp.bfloat16
_NT = (((1,), (1,)), ((), ()))
_TN = (((0,), (0,)), ((), ()))


def _vmem_limit(in_specs, operands, out_specs, out_shape, scratch=()):
    def nbytes(shape, dtype):
        n = jnp.dtype(dtype).itemsize
        for s in shape:
            n *= s
        return n
    total = sum(2 * nbytes(spec.block_shape, op.dtype) for spec, op in zip(in_specs, operands))
    total += sum(2 * nbytes(spec.block_shape, o.dtype) for spec, o in zip(out_specs, out_shape))
    total += sum(nbytes(s.shape, s.dtype) for s in scratch)
    mib = 1024 * 1024
    return min(VMEM_LIMIT_BYTES, -(-int(total * VMEM_TEMP_FACTOR) // mib) * mib)


def _dot(a, b):
    return jnp.dot(a.astype(_BF16), b.astype(_BF16), preferred_element_type=_F32)


def _dot_nt(a, b):
    return lax.dot_general(a.astype(_BF16), b.astype(_BF16), _NT, preferred_element_type=_F32)


def _dot_tn(a, b):
    return lax.dot_general(a.astype(_BF16), b.astype(_BF16), _TN, preferred_element_type=_F32)


def _log_sigmoid(x):
    return jnp.minimum(x, 0.0) - jnp.log(1.0 + jnp.exp(-jnp.abs(x)))


def _iota(shape, axis):
    return lax.broadcasted_iota(jnp.int32, shape, axis)


def _seg_cumsum(x, seg):
    pos = _iota(x.shape, 0) & (seg - 1)
    s = 1
    while s < seg:
        x = x + jnp.where(pos >= s, pltpu.roll(x, s, axis=0), 0.0)
        s *= 2
    return x


def _cumsum_rows_mxu(x):
    n = x.shape[0]
    hi = x.astype(_BF16)
    r1 = x - hi.astype(_F32)
    mid = r1.astype(_BF16)
    lo = (r1 - mid.astype(_F32)).astype(_BF16)
    tri3 = (_iota((n, 3 * n), 0) >= (_iota((n, 3 * n), 1) & (n - 1))).astype(_BF16)
    return jnp.dot(tri3, jnp.concatenate([hi, mid, lo], axis=0), preferred_element_type=_F32)


def _rows_to_cols(rows):
    n = rows[0].shape[1]
    rid = _iota((LANES, n), 0)
    stack = jnp.zeros((LANES, n), _F32)
    for i, r in enumerate(rows):
        stack = jnp.where(rid == i, r, stack)
    return stack.T


def _rope(x, cos, sin_signed):
    w = x.shape[1]
    even = (_iota(x.shape, 1) & 1) == 0
    swapped = jnp.where(even, pltpu.roll(x, w - 1, axis=1), pltpu.roll(x, 1, axis=1))
    return x * cos + swapped * sin_signed


def _gla_pair_scores(a, q, k, b2, lane0, row_lo_hi):
    n = q.shape[0]
    lanes = _iota((SUBLANES, a.shape[1]), 1)
    tiles = [a[r:r + SUBLANES] for r in range(0, n, SUBLANES)]
    for s in range(n):
        lo, hi = row_lo_hi(s)
        for j, r0 in enumerate(range(0, n, SUBLANES)):
            if r0 + SUBLANES <= lo or r0 >= hi:
                continue
            e = jnp.exp2(b2[r0:r0 + SUBLANES] - b2[s:s + 1, :])
            col = jnp.sum(q[r0:r0 + SUBLANES] * k[s:s + 1, :] * e, axis=-1, keepdims=True)
            take = lanes == lane0 + s
            if lo > r0 or hi < r0 + SUBLANES:
                rows = _iota((SUBLANES, 1), 0) + r0
                take = take & (rows >= lo) & (rows < hi)
            tiles[j] = jnp.where(take, col, tiles[j])
    return jnp.concatenate(tiles, axis=0)


def _rms_heads(o, w):
    return o * lax.rsqrt(jnp.mean(o * o, axis=-1, keepdims=True) + HEAD_NORM_EPS) * w


def _group_norm(o, w, b):
    mu = jnp.mean(o, axis=-1, keepdims=True)
    d = o - mu
    var = jnp.mean(d * d, axis=-1, keepdims=True)
    return d * lax.rsqrt(var + HEAD_NORM_EPS) * w + b


def _prep_x_kernel(x_ref, wcol_ref, wlr_ref, blr_ref, xb_ref, la_ref, *, rank):
    x = x_ref[...]
    if x.ndim == 3:
        x = x.reshape(x.shape[0] * x.shape[1], x.shape[2])
    xb = x.astype(_BF16)
    xb_ref[...] = xb
    wrow = wcol_ref[0]
    w_lra = jnp.concatenate([wrow, jnp.zeros((LANES - rank, wrow.shape[1]), _F32)], axis=0)
    lr = _dot_nt(xb, w_lra)
    w = wlr_ref[...]
    w_hi = w.astype(_BF16).astype(_F32)
    w_lo = w - w_hi
    lr_hi = lr.astype(_BF16).astype(_F32)
    lr_lo = lr - lr_hi
    lr_cat = lr_hi + pltpu.roll(lr_hi, rank, axis=1) + pltpu.roll(lr_lo, 2 * rank, axis=1)
    w_cat = jnp.concatenate([w_hi[:rank], w_lo[:rank], w_hi[:rank],
                             jnp.zeros((LANES - 3 * rank, w.shape[1]), _F32)], axis=0)
    logit = _dot(lr_cat, w_cat) + blr_ref[...]
    la_ref[...] = _log_sigmoid(logit) * (1.0 / GLA_TAU)


def _prep_x(x, w_t, layer, lr_col, rank, w_lr, b_lr, tm):
    kdim = x.shape[-1]
    m = x.size // kdim
    qk = w_lr.shape[1]
    assert lr_col % rank == 0 and rank % SUBLANES == 0 and 3 * rank <= LANES
    whole = lambda a: pl.BlockSpec(a.shape, lambda i: (0,) * a.ndim)
    if x.ndim == 3:
        x_spec = pl.BlockSpec((tm // x.shape[1], x.shape[1], kdim), lambda i: (i, 0, 0))
    else:
        x_spec = pl.BlockSpec((tm, kdim), lambda i: (i, 0))
    operands = [x, w_t, w_lr, b_lr]
    in_specs = [x_spec,
                pl.BlockSpec((1, rank, kdim), lambda i: (layer, lr_col // rank, 0)),
                whole(w_lr), whole(b_lr)]
    out_specs = [pl.BlockSpec((tm, kdim), lambda i: (i, 0)),
                 pl.BlockSpec((tm, qk), lambda i: (i, 0))]
    out_shape = [jax.ShapeDtypeStruct((m, kdim), _BF16),
                 jax.ShapeDtypeStruct((m, qk), _F32)]
    return pl.pallas_call(
        functools.partial(_prep_x_kernel, rank=rank),
        grid=(m // tm,),
        in_specs=in_specs,
        out_specs=out_specs,
        out_shape=out_shape,
        compiler_params=pltpu.CompilerParams(
            dimension_semantics=("arbitrary",),
            vmem_limit_bytes=_vmem_limit(in_specs, operands, out_specs, out_shape)),
        name="prep_x",
    )(*operands)


N_SAMPLE_IN = 19


def _proj_in_kernel(*refs, n_plain, shift, silu_tiles, sigmoid_tiles, sample_steps):
    x_ref, wa_ref, we_ref = refs[:3]
    wbf_ref = refs[-1]
    if sample_steps:
        unit_in = refs[3:3 + N_SAMPLE_IN]
        h_ref = refs[3 + N_SAMPLE_IN]
        unit_out = refs[4 + N_SAMPLE_IN:7 + N_SAMPLE_IN]
    else:
        h_ref = refs[3]
    j = pl.program_id(0)
    i = pl.program_id(1)
    tn, kdim = wbf_ref.shape

    @pl.when(i == 0)
    def _():
        @pl.when(j < n_plain)
        def _():
            wbf_ref[...] = wa_ref[0].astype(_BF16)

        @pl.when(j >= n_plain)
        def _():
            wbf_ref[0:tn - shift, :] = wa_ref[0, shift:tn, :].astype(_BF16)
            wbf_ref[tn - shift:tn, :] = we_ref[0].astype(_BF16)

    is_silu = functools.reduce(jnp.logical_or, [j == t for t in silu_tiles])
    is_sigmoid = functools.reduce(jnp.logical_or, [j == t for t in sigmoid_tiles])

    def slab(cols, activated):
        acc = lax.dot_general(x_ref[...], wbf_ref[cols, :], _NT, preferred_element_type=_F32)
        if activated:
            s = 0.5 * jnp.tanh(0.5 * acc) + 0.5
            acc = jnp.where(is_sigmoid, s, acc * s)
        h_ref[:, cols] = acc

    def tile(activated):
        finish_unit = None
        if sample_steps:
            finish_unit = _mixer_sample_unit(*unit_in, *unit_out, steps=sample_steps)
        slab(slice(0, tn // 4), activated)
        if finish_unit is not None:
            finish_unit()
        slab(slice(tn // 4, tn), activated)

    @pl.when(is_silu | is_sigmoid)
    def _():
        tile(True)

    @pl.when(jnp.logical_not(is_silu | is_sigmoid))
    def _():
        tile(False)


PROJ_TILES = 16


def _proj_in(x, w_t, layer, rank, tm, sample=None):
    m, kdim = x.shape
    d_model = kdim
    tn = d_model // 2
    nj = PROJ_TILES
    n_plain = 6
    bf16_rows = 2 * SUBLANES
    assert rank % bf16_rows == 0 and tn % rank == 0 and m % tm == 0
    ni = m // tm

    def row_block(j, i):
        return jnp.where(j % 2 == 0, i, ni - 1 - i)
    in_specs = [
        pl.BlockSpec((tm, kdim), lambda j, i: (row_block(j, i), 0)),
        pl.BlockSpec((1, tn, kdim), lambda j, i: (layer, j, 0)),
        pl.BlockSpec((1, rank, kdim), lambda j, i: (layer, (j + 1) * (tn // rank), 0)),
    ]
    out_specs = [pl.BlockSpec((tm, tn), lambda j, i: (row_block(j, i), j))]
    out_shape = [jax.ShapeDtypeStruct((m, nj * tn), _F32)]
    operands = [x, w_t, w_t]
    sample_steps = 0
    if sample is not None:
        unit_operands, sample_steps, n_units, unit_specs = sample
        assert n_units <= nj * ni

        def unit_of(j, i):
            u = jnp.minimum(j * ni + i, n_units - 1)
            return u // HEAD_GROUPS, u % HEAD_GROUPS
        unit_in, unit_out, unit_shape = unit_specs(unit_of)
        assert len(unit_in) == N_SAMPLE_IN
        in_specs += unit_in
        out_specs += unit_out
        out_shape += unit_shape
        operands += unit_operands
    kern = functools.partial(_proj_in_kernel, n_plain=n_plain, shift=rank,
                             silu_tiles=(4, 5, 10, 11), sigmoid_tiles=(12, 13, 14, 15),
                             sample_steps=sample_steps)
    scratch = [pltpu.VMEM((tn, kdim), _BF16)]
    return pl.pallas_call(
        kern,
        grid=(nj, ni),
        in_specs=in_specs,
        out_specs=out_specs,
        out_shape=out_shape,
        scratch_shapes=scratch,
        compiler_params=pltpu.CompilerParams(
            dimension_semantics=("arbitrary", "arbitrary"),
            vmem_limit_bytes=_vmem_limit(in_specs, operands, out_specs, out_shape, scratch)),
        name="proj_in_mix" if sample is not None else "proj_in",
    )(*operands)


def _h_cols(d_model, g):
    wq, wv = d_model // 8, d_model // 4
    d = d_model
    offs = dict(qa=(0, wq), ka=(d // 2, wq), va=(d, wv), za=(2 * d, wv),
                qb=(3 * d, wq), kb=(3 * d + d // 2, wq), vb=(4 * d, wv), zb=(5 * d, wv),
                ga=(6 * d, wv), gb=(7 * d, wv))
    return {n: slice(o + g * w, o + (g + 1) * w) for n, (o, w) in offs.items()}


def _mixer_prompt_kernel(h_ref, la_ref, gnw_ref, rnw_ref, rnb_ref,
                         cos_ref, sin_ref, lg_ref, wout_ref, m_ref, sg_ref, sr_ref, wout_bf_ref,
                         qdec_ref, kdec_ref, dmat_ref):
    wout_bf_ref[...] = wout_ref[0].astype(_BF16)
    chunk = h_ref.shape[0]
    d_model = m_ref.shape[1]
    dk_a, dv_a = sg_ref.shape[3:]
    dk_b, dv_b = sr_ref.shape[3:]

    @pl.when(pl.program_id(1) == 0)
    def _():
        sg_ref[...] = jnp.zeros(sg_ref.shape, _F32)
        sr_ref[...] = jnp.zeros(sr_ref.shape, _F32)

    @pl.when((pl.program_id(0) == 0) & (pl.program_id(1) == 0))
    def _():
        tpos = (_iota((chunk, 2 * dk_b), 0) + 1).astype(_F32)
        tdiff = _iota((chunk, chunk), 0) - _iota((chunk, chunk), 1)
        for g in range(HEAD_GROUPS):
            lg = lg_ref[g]
            qdec_ref[g] = jnp.exp(lg * tpos)
            kdec_ref[g] = jnp.exp(lg * (chunk - tpos))
            for hh in range(2):
                lg1 = lg[:, hh * dk_b:hh * dk_b + 1]
                dmat_ref[2 * g + hh] = jnp.exp(
                    jnp.where(tdiff >= 0, lg1 * tdiff.astype(_F32), -jnp.inf))

    rows = _iota((chunk, 1), 0)
    cos = jnp.concatenate([cos_ref[...]] * 2, axis=1)
    sin = jnp.concatenate([sin_ref[...]] * 2, axis=1)
    n_sub = chunk // GLA_SUB
    groups = range(HEAD_GROUPS)
    cols = [_h_cols(d_model, g) for g in groups]
    ksl = [slice(hh * dk_b, (hh + 1) * dk_b) for hh in range(2)]
    vsl = [slice(hh * dv_b, (hh + 1) * dv_b) for hh in range(2)]


    gla, ret = [], []
    for g in groups:
        c = cols[g]
        b = _cumsum_rows_mxu(la_ref[:, g * dk_a:(g + 1) * dk_a]) * LOG2_E
        q = h_ref[:, c["qa"]] * (dk_a ** -0.5)
        k = h_ref[:, c["ka"]]
        v = h_ref[:, c["va"]].astype(_BF16)
        s_gla = sg_ref[0, 0, g]
        gla.append(dict(b=b, q=q, k=k, v=v, s=s_gla, o=_dot(q * jnp.exp2(b), s_gla)))

        qb = _rope(h_ref[:, c["qb"]], cos, sin)
        kb = _rope(h_ref[:, c["kb"]], cos, sin) * (dk_b ** -0.5)
        vb = h_ref[:, c["vb"]].astype(_BF16)
        lg = lg_ref[g]
        q_dec = qb * qdec_ref[g]
        s_ret = [sr_ref[0, 0, 2 * g + hh] for hh in range(2)]
        ret.append(dict(
            vb=vb, lg=lg, s=s_ret, k_dec=kb * kdec_ref[g],
            o=[_dot(q_dec[:, ksl[hh]], s_ret[hh]) for hh in range(2)],
            qk=[_dot_nt(qb[:, ksl[hh]], kb[:, ksl[hh]]) for hh in range(2)]))

    for g in groups:
        q, k, b = gla[g]["q"], gla[g]["k"], gla[g]["b"]
        a_off = [jnp.zeros((GLA_SUB, chunk), _F32)]
        for i in range(1, n_sub):
            lo = i * GLA_SUB
            r = b[lo - 1:lo, :]
            k_pre = k * jnp.exp2(jnp.where(rows < lo, r - b, -jnp.inf))
            a_off.append(_dot_nt(q[lo:lo + GLA_SUB] * jnp.exp2(b[lo:lo + GLA_SUB] - r), k_pre))
        gla[g]["a"] = a_off

    for g in groups:
        q, k, b, v = gla[g]["q"], gla[g]["k"], gla[g]["b"], gla[g]["v"]
        intra = []
        for i in range(n_sub):
            lo = i * GLA_SUB
            sub = slice(lo, lo + GLA_SUB)
            a = _gla_pair_scores(gla[g]["a"][i], q[sub], k[sub], b[sub], lo, lambda s: (s, GLA_SUB))
            intra.append(_dot(a, v))
        gla[g]["o"] = gla[g]["o"] + jnp.concatenate(intra, axis=0)
        for hh in range(2):
            scores = ret[g]["qk"][hh] * dmat_ref[2 * g + hh]
            ret[g]["o"][hh] = ret[g]["o"][hh] + _dot(scores, ret[g]["vb"][:, vsl[hh]])

    for g in groups:
        k, b, v = gla[g]["k"], gla[g]["b"], gla[g]["v"]
        b_last = b[chunk - 1:chunk, :]
        decay_col = _rows_to_cols([jnp.exp2(b_last)])[:, 0:1]
        sg_ref[0, 0, g] = decay_col * gla[g]["s"] + _dot_tn(k * jnp.exp2(b_last - b), v)
        for hh in range(2):
            lg1 = ret[g]["lg"][:, hh * dk_b:hh * dk_b + 1]
            sr_ref[0, 0, 2 * g + hh] = (jnp.exp(lg1 * chunk) * ret[g]["s"][hh]
                                        + _dot_tn(ret[g]["k_dec"][:, ksl[hh]], ret[g]["vb"][:, vsl[hh]]))

    for g in groups:
        c = cols[g]
        gv = slice(g * dv_a, (g + 1) * dv_a)
        o_a = _rms_heads(gla[g]["o"], gnw_ref[...]) * h_ref[:, c["za"]]
        o_b = []
        for hh in range(2):
            nsl = slice(g * dv_a + hh * dv_b, g * dv_a + (hh + 1) * dv_b)
            o_b.append(_group_norm(ret[g]["o"][hh], rnw_ref[:, nsl], rnb_ref[:, nsl]))
        o_b = jnp.concatenate(o_b, axis=1) * h_ref[:, c["zb"]]
        merged = h_ref[:, c["ga"]] * o_a + h_ref[:, c["gb"]] * o_b
        m_ref[:, gv] = merged.astype(m_ref.dtype)


def _mixer_prompt(h, la, params, rope, lg, w_out, layer, batch, seq, d_model):
    wq, wv = d_model // 8, d_model // 4
    dk_a, dv_a = wq, wv
    dk_b, dv_b = wq // 2, wv // 2
    chunk = PROMPT_CHUNK
    nc = seq // chunk
    e_rows = w_out.shape[1]
    bf16_rows = 2 * SUBLANES
    assert e_rows % (batch * nc) == 0 and (e_rows // (batch * nc)) % bf16_rows == 0
    w_rows = e_rows // (batch * nc)
    whole = lambda a: pl.BlockSpec(a.shape, lambda b, c: (0,) * a.ndim)
    in_specs = [pl.BlockSpec((chunk, h.shape[1]), lambda b, c: (b * nc + c, 0)),
                pl.BlockSpec((chunk, la.shape[1]), lambda b, c: (b * nc + c, 0))]
    in_specs += [whole(p) for p in params]
    in_specs += [pl.BlockSpec((chunk, dk_b), lambda b, c: (c, 0)),
                 pl.BlockSpec((chunk, dk_b), lambda b, c: (c, 0)),
                 whole(lg),
                 pl.BlockSpec((1, w_rows, w_out.shape[2]), lambda b, c: (layer, b * nc + c, 0))]
    out_specs = [
        pl.BlockSpec((chunk, d_model), lambda b, c: (b * nc + c, 0)),
        pl.BlockSpec((1, 1, GLA_HEADS, dk_a, dv_a), lambda b, c: (0, b, 0, 0, 0)),
        pl.BlockSpec((1, 1, RET_HEADS, dk_b, dv_b), lambda b, c: (0, b, 0, 0, 0)),
        pl.BlockSpec((w_rows, w_out.shape[2]), lambda b, c: (b * nc + c, 0)),
    ]
    out_shape = [
        jax.ShapeDtypeStruct((batch * seq, d_model), _BF16),
        jax.ShapeDtypeStruct((1, batch, GLA_HEADS, dk_a, dv_a), _F32),
        jax.ShapeDtypeStruct((1, batch, RET_HEADS, dk_b, dv_b), _F32),
        jax.ShapeDtypeStruct(w_out.shape[1:], _BF16),
    ]
    operands = [h, la, *params, *rope, lg, w_out]
    scratch = [pltpu.VMEM((HEAD_GROUPS, chunk, wq), _F32),
               pltpu.VMEM((HEAD_GROUPS, chunk, wq), _F32),
               pltpu.VMEM((RET_HEADS, chunk, chunk), _F32)]
    return pl.pallas_call(
        _mixer_prompt_kernel,
        grid=(batch, nc),
        in_specs=in_specs,
        out_specs=out_specs,
        out_shape=out_shape,
        scratch_shapes=scratch,
        compiler_params=pltpu.CompilerParams(
            dimension_semantics=("arbitrary", "arbitrary"),
            vmem_limit_bytes=_vmem_limit(in_specs, operands, out_specs, out_shape, scratch)),
        name="mixer_prompt",
    )(*operands)


def _mixer_sample_unit(qa_ref, ka_ref, va_ref, za_ref, qb_ref, kb_ref, vb_ref, zb_ref,
                       ga_ref, gb_ref, la_ref, gnw_ref, rnw_ref, rnb_ref,
                       cos_ref, sin_ref, lg_ref, sg0_ref, sr0_ref, m_ref, sg_ref, sr_ref,
                       *, steps):
    rows_n, dk_a = qa_ref.shape
    nseq = rows_n // steps
    dk_b = qb_ref.shape[1] // 2
    dv_b = vb_ref.shape[1] // 2
    rows = _iota((rows_n, 1), 0)
    rows_p = _iota((LANES, 1), 0)
    pad = LANES - rows_n
    ksl = [slice(hh * dk_b, (hh + 1) * dk_b) for hh in range(2)]
    vsl = [slice(hh * dv_b, (hh + 1) * dv_b) for hh in range(2)]

    def pad_rows(x):
        return jnp.concatenate([x, jnp.zeros((pad, x.shape[1]), x.dtype)], axis=0)

    def seq_rows(x, n, r=rows):
        return jnp.where((r >= n * steps) & (r < (n + 1) * steps), x, 0.0)

    b = _seg_cumsum(la_ref[...], steps) * LOG2_E
    q = qa_ref[...] * (dk_a ** -0.5)
    k = ka_ref[...]
    v = pad_rows(va_ref[...]).astype(_BF16)
    q_dec = q * jnp.exp2(b)
    s_gla = [sg0_ref[0, n, 0] for n in range(nseq)]
    o_a = _dot(seq_rows(q_dec, 0), s_gla[0])
    for n in range(1, nseq):
        o_a = o_a + _dot(seq_rows(q_dec, n), s_gla[n])

    cos = jnp.concatenate([cos_ref[...]] * 2, axis=1)
    sin = jnp.concatenate([sin_ref[...]] * 2, axis=1)
    qb = _rope(qb_ref[...], cos, sin)
    kb = _rope(kb_ref[...], cos, sin) * (dk_b ** -0.5)
    vb = pad_rows(vb_ref[...]).astype(_BF16)
    lg = lg_ref[0]
    lg1 = [lg[:, hh * dk_b:hh * dk_b + 1] for hh in range(2)]
    tpos = ((_iota(qb.shape, 0) & (steps - 1)) + 1).astype(_F32)
    qb_dec = qb * jnp.exp(lg * tpos)
    kb_p = pad_rows(kb)
    s_ret = [[sr0_ref[0, n, hh] for n in range(nseq)] for hh in range(2)]
    o_r, qk = [], []
    for hh in range(2):
        o_h = _dot(seq_rows(qb_dec[:, ksl[hh]], 0), s_ret[hh][0])
        for n in range(1, nseq):
            o_h = o_h + _dot(seq_rows(qb_dec[:, ksl[hh]], n), s_ret[hh][n])
        o_r.append(o_h)
        qk.append(_dot_nt(qb[:, ksl[hh]], kb_p[:, ksl[hh]]))

    a = jnp.zeros((rows_n, LANES), _F32)
    a = _gla_pair_scores(a, q, k, b, 0, lambda s: (s, (s // steps + 1) * steps))
    o_a = o_a + _dot(a, v)
    r_i = _iota((rows_n, LANES), 0)
    c_i = _iota((rows_n, LANES), 1)
    same_seq = (r_i >> (steps.bit_length() - 1)) == (c_i >> (steps.bit_length() - 1))
    tdiff = r_i - c_i
    for hh in range(2):
        decay = jnp.exp(jnp.where((tdiff >= 0) & same_seq, lg1[hh] * tdiff.astype(_F32), -jnp.inf))
        o_r[hh] = o_r[hh] + _dot(qk[hh] * decay, vb[:, vsl[hh]])

    def finish():
        b_last = [b[(n + 1) * steps - 1:(n + 1) * steps, :] for n in range(nseq)]
        b_end = jnp.zeros_like(b)
        for n in range(nseq):
            b_end = b_end + seq_rows(jnp.broadcast_to(b_last[n], b.shape), n)
        k_dec = k * jnp.exp2(b_end - b)
        decay_cols = _rows_to_cols([jnp.exp2(r) for r in b_last])
        for n in range(nseq):
            sg_ref[0, n, 0] = (decay_cols[:, n:n + 1] * s_gla[n]
                               + _dot_tn(seq_rows(k_dec, n), v[:rows_n]))
        kb_dec = kb * jnp.exp(lg * (steps - tpos))
        for hh in range(2):
            for n in range(nseq):
                sr_ref[0, n, hh] = (jnp.exp(lg1[hh] * steps) * s_ret[hh][n]
                                    + _dot_tn(seq_rows(kb_dec[:, ksl[hh]], n), vb[:rows_n, vsl[hh]]))
        o_an = _rms_heads(o_a, gnw_ref[...]) * za_ref[...]
        o_bn = jnp.concatenate([_group_norm(o_r[hh], rnw_ref[:, vsl[hh]], rnb_ref[:, vsl[hh]])
                                for hh in range(2)], axis=1) * zb_ref[...]
        merged = ga_ref[...] * o_an + gb_ref[...] * o_bn
        m_ref[...] = merged.astype(m_ref.dtype)

    return finish


def _sample_units(h, la, params, rope, lg, state_gla, state_ret, layer, nseq_total, steps, d_model):
    wq, wv = d_model // 8, d_model // 4
    dk_a, dv_a = wq, wv
    dk_b, dv_b = wq // 2, wv // 2
    nseq = SAMPLE_SEQS
    rows_n = nseq * steps
    assert steps & (steps - 1) == 0 and nseq_total % nseq == 0
    d = d_model

    def specs(unit_of):
        def at(fn):
            return lambda *ids: fn(*unit_of(*ids))

        def h_spec(width, col_off):
            base = col_off // width
            return pl.BlockSpec((rows_n, width), at(lambda p, g: (p, base + g)))
        in_specs = [h_spec(wq, 0), h_spec(wq, d // 2), h_spec(wv, d), h_spec(wv, 2 * d),
                    h_spec(wq, 3 * d), h_spec(wq, 3 * d + d // 2), h_spec(wv, 4 * d),
                    h_spec(wv, 5 * d), h_spec(wv, 6 * d), h_spec(wv, 7 * d)]
        in_specs += [
            pl.BlockSpec((rows_n, wq), at(lambda p, g: (p, g))),
            pl.BlockSpec((1, wv), at(lambda p, g: (0, 0))),
            pl.BlockSpec((1, wv), at(lambda p, g: (0, g))),
            pl.BlockSpec((1, wv), at(lambda p, g: (0, g))),
            pl.BlockSpec((rows_n, dk_b), at(lambda p, g: (0, 0))),
            pl.BlockSpec((rows_n, dk_b), at(lambda p, g: (0, 0))),
            pl.BlockSpec((1, 1, wq), at(lambda p, g: (g, 0, 0))),
            pl.BlockSpec((1, nseq, 1, dk_a, dv_a), at(lambda p, g: (layer, p, g, 0, 0))),
            pl.BlockSpec((1, nseq, 2, dk_b, dv_b), at(lambda p, g: (layer, p, g, 0, 0))),
        ]
        out_specs = [
            pl.BlockSpec((rows_n, wv), at(lambda p, g: (p, g))),
            pl.BlockSpec((1, nseq, 1, dk_a, dv_a), at(lambda p, g: (0, p, g, 0, 0))),
            pl.BlockSpec((1, nseq, 2, dk_b, dv_b), at(lambda p, g: (0, p, g, 0, 0))),
        ]
        out_shape = [
            jax.ShapeDtypeStruct((nseq_total * steps, d_model), _BF16),
            jax.ShapeDtypeStruct((1, nseq_total, GLA_HEADS, dk_a, dv_a), _F32),
            jax.ShapeDtypeStruct((1, nseq_total, RET_HEADS, dk_b, dv_b), _F32),
        ]
        return in_specs, out_specs, out_shape

    operands = [h] * 10 + [la, *params, *rope, lg, state_gla, state_ret]
    n_units = (nseq_total // nseq) * HEAD_GROUPS
    return operands, steps, n_units, specs


PROJ_OUT_SLABS = 4

def _out_proj_ln(m, x, w, ln_w, ln_b, alpha):
    r = alpha * x + jnp.dot(m, w, preferred_element_type=_F32)
    mu = jnp.mean(r, axis=-1, keepdims=True)
    d = r - mu
    var = jnp.mean(d * d, axis=-1, keepdims=True)
    return d * lax.rsqrt(var + LN_EPS) * ln_w + ln_b


def _proj_out_kernel(m_ref, x_ref, w_ref, lnw_ref, lnb_ref, y_ref, *, alpha):
    tm = m_ref.shape[0]
    slab = min(tm, max(tm // PROJ_OUT_SLABS, LANES))
    for r in range(0, tm, slab):
        if x_ref.ndim == 3:
            steps = x_ref.shape[1]
            seqs = slice(r // steps, (r + slab) // steps)
            x = x_ref[seqs].reshape(slab, x_ref.shape[2])
        else:
            x = x_ref[r:r + slab, :]
        y = _out_proj_ln(m_ref[r:r + slab, :], x, w_ref[...], lnw_ref[...], lnb_ref[...], alpha)
        if x_ref.ndim == 3:
            y_ref[seqs] = y.reshape(slab // steps, steps, y.shape[1])
        else:
            y_ref[r:r + slab, :] = y


def _proj_out(merged, x, w_out, ln_w, ln_b, alpha, tm):
    d = x.shape[-1]
    m = x.size // d
    e = merged.shape[1]
    if x.ndim == 3:
        xy_spec = pl.BlockSpec((tm // x.shape[1], x.shape[1], d), lambda i: (i, 0, 0))
    else:
        xy_spec = pl.BlockSpec((tm, d), lambda i: (i, 0))
    operands = [merged, x, w_out, ln_w, ln_b]
    in_specs = [
        pl.BlockSpec((tm, e), lambda i: (i, 0)),
        xy_spec,
        pl.BlockSpec((e, d), lambda i: (0, 0), pipeline_mode=pl.Buffered(1)),
        pl.BlockSpec((1, d), lambda i: (0, 0)),
        pl.BlockSpec((1, d), lambda i: (0, 0)),
    ]
    out_shape = jax.ShapeDtypeStruct(x.shape, _F32)
    return pl.pallas_call(
        functools.partial(_proj_out_kernel, alpha=alpha),
        grid=(m // tm,),
        in_specs=in_specs,
        out_specs=xy_spec,
        out_shape=out_shape,
        compiler_params=pltpu.CompilerParams(
            dimension_semantics=("arbitrary",),
            vmem_limit_bytes=_vmem_limit(in_specs, operands, [xy_spec], [out_shape])),
        name="proj_out",
    )(*operands)


def _rope_tables(pos, dk):
    inv = 1.0 / (ROPE_BASE ** jnp.linspace(0.0, 1.0, dk // 2, dtype=_F32))
    inv_lanes = jnp.repeat(inv, 2)
    sign = jnp.tile(jnp.asarray([-1.0, 1.0], _F32), dk // 2)
    ang = pos.astype(_F32)[:, None] * inv_lanes[None, :]
    return jnp.cos(ang), jnp.sin(ang) * sign[None, :]


def _pick_tile(n, pref):
    t = min(n, pref)
    while n % t:
        t //= 2
    return t


def kernel(x_prompt, x_sample, state_gla, state_ret, w_in, w_lr, b_lr, gla_norm_w,
           ret_norm_w, ret_norm_b, w_out, ln_w, ln_b):
    depth, d_model, _ = w_in.shape
    batch, seq, _ = x_prompt.shape
    dec_batch, dec_seq, _ = x_sample.shape
    rank = w_lr.shape[1]
    dk_b = d_model // 2 // RET_HEADS
    assert state_gla.shape[2] == GLA_HEADS and state_ret.shape[2] == RET_HEADS
    assert rank <= LANES and seq % PROMPT_CHUNK == 0
    alpha = (2.0 * depth) ** 0.25

    lg_heads = jnp.log(1.0 - 2.0 ** (-5.0 - jnp.arange(RET_HEADS, dtype=_F32)))
    lg = jnp.repeat(lg_heads, dk_b).reshape(HEAD_GROUPS, 1, 2 * dk_b)
    rope_p = _rope_tables(jnp.arange(seq, dtype=jnp.int32), dk_b)
    pos_s = PAST_LEN + jnp.arange(dec_seq, dtype=jnp.int32)
    rope_s = _rope_tables(jnp.tile(pos_s, SAMPLE_SEQS), dk_b)

    hp = x_prompt.reshape(batch * seq, d_model)
    hs = x_sample
    n_rows_s = dec_batch * dec_seq
    w_t = jnp.swapaxes(w_in, 1, 2)
    gla_p, ret_p, gla_s, ret_s = [], [], [], []
    for l in range(depth):
        lr_lo = 3 * d_model
        w_lr_p = jnp.pad(w_lr[l], ((0, LANES - rank), (0, 0)))
        params = (gla_norm_w[l][None, :], ret_norm_w[l][None, :], ret_norm_b[l][None, :])
        lnw, lnb = ln_w[l][None, :], ln_b[l][None, :]

        xp_b, la_p = _prep_x(hp, w_t, l, lr_lo, rank, w_lr_p, b_lr[l][None, :],
                             _pick_tile(hp.shape[0], 1024))
        xs_b, la_s = _prep_x(hs, w_t, l, lr_lo, rank, w_lr_p, b_lr[l][None, :],
                             _pick_tile(n_rows_s, 512))
        (h_s,) = _proj_in(xs_b, w_t, l, rank, _pick_tile(xs_b.shape[0], 1024))
        units = _sample_units(h_s, la_s, params, rope_s, lg, state_gla, state_ret, l,
                              dec_batch, dec_seq, d_model)
        h_p, merged_s, sg_s, sr_s = _proj_in(xp_b, w_t, l, rank, _pick_tile(xp_b.shape[0], 1024),
                                             sample=units)

        merged, sg, sr, w_o = _mixer_prompt(h_p, la_p, params, rope_p, lg, w_out, l,
                                            batch, seq, d_model)
        hp = _proj_out(merged, hp, w_o, lnw, lnb, alpha, _pick_tile(hp.shape[0], 1024))
        gla_p.append(sg)
        ret_p.append(sr)

        merged, sg, sr = merged_s, sg_s, sr_s
        hs = _proj_out(merged, hs, w_o, lnw, lnb, alpha, _pick_tile(n_rows_s, 256))
        gla_s.append(sg)
        ret_s.append(sr)

    cat = lambda parts: parts[0] if len(parts) == 1 else jnp.concatenate(parts, axis=0)
    return (hp.reshape(batch, seq, d_model), hs,
            cat(gla_p), cat(ret_p), cat(gla_s), cat(ret_s))
```
